```python
import math, functools
import jax, jax.numpy as jnp
from jax import lax
import numpy as np

D_MODEL = 1024
BATCH = 32
SEQ = 256
DEPTH = 4
DEC_BATCH = 8
DEC_SEQ = 2048
PAST_LEN = 512

GRID_W = 64
N_MIXERS = 3
N_NA = (DEPTH + 2) // 3
N_MLA = (DEPTH + 1) // 3
N_GDN = DEPTH // 3

NA_HEADS = 16
NA_HEAD_DIM = D_MODEL // NA_HEADS
NA_ROWS = 8
NA_COLS = 16

MLA_HEADS = 16
MLA_NOPE_DIM = 64
MLA_ROPE_DIM = 32
MLA_QK_DIM = MLA_NOPE_DIM + MLA_ROPE_DIM
MLA_V_DIM = 64
MLA_Q_LORA = 256
MLA_KV_LORA = 128

GDN_HEADS = 8
GDN_HEAD_DIM = 128
GDN_KEY = GDN_HEADS * GDN_HEAD_DIM
GDN_QKV = 3 * GDN_KEY
GDN_CONV_W = 5
GDN_CHUNK = 64

N_EXPERTS = 32
TOP_K = 4
D_EXPERT = D_MODEL
SWIGLU_ALPHA = 1.702
SWIGLU_LIMIT = 7.0

ROPE_BASE = 10000.0
Q_BLOCK = 128
EPS = 1e-6

kernel_name = 'hybrid_diffusion_na_mla_gdn_moe_step'


def rms_norm(x, g):
    xf = x.astype(jnp.float32)
    y = xf * lax.rsqrt(jnp.mean(xf * xf, axis=-1, keepdims=True) + EPS)
    return (y * g.astype(jnp.float32)).astype(x.dtype)


def l2norm(x):
    return x * lax.rsqrt(jnp.sum(x * x, axis=-1, keepdims=True) + EPS)


def modulate(x, g, shift, scale):
    return rms_norm(x, g) * (1 + scale) + shift


def attend_blocks(q, k, v, scale):
    b, n, h, dq = q.shape
    nb = n // Q_BLOCK
    qb = q.reshape(b, nb, Q_BLOCK, h, dq).transpose(1, 0, 2, 3, 4)

    def one_block(qi):
        s = jnp.einsum('bqhd,bkhd->bhqk', qi, k, preferred_element_type=jnp.float32) * scale
        p = jax.nn.softmax(s, axis=-1).astype(v.dtype)
        return jnp.einsum('bhqk,bkhd->bqhd', p, v)

    o = lax.map(one_block, qb)
    return o.transpose(1, 0, 2, 3, 4).reshape(b, n, h, v.shape[-1])


def axial_rope(n_tokens, rot_dim):
    n_freq = rot_dim // 4
    inv = ROPE_BASE ** (-jnp.arange(n_freq, dtype=jnp.float32) / n_freq)
    t = jnp.arange(n_tokens)
    row = (t // GRID_W).astype(jnp.float32)
    col = (t % GRID_W).astype(jnp.float32)
    ang = jnp.concatenate([row[:, None] * inv, col[:, None] * inv], axis=-1)
    return jnp.cos(ang), jnp.sin(ang)


def apply_rope(x, cos, sin):
    half = x.shape[-1] // 2
    xf = x.astype(jnp.float32)
    x1, x2 = xf[..., :half], xf[..., half:]
    c, s = cos[:, None, :], sin[:, None, :]
    return jnp.concatenate([x1 * c - x2 * s, x1 * s + x2 * c], axis=-1).astype(x.dtype)


def na_qkv(h, w_in, q_g, k_g):
    b, n, _ = h.shape
    qkv = (h @ w_in).reshape(b, n, 3, NA_HEADS, NA_HEAD_DIM)
    return rms_norm(qkv[:, :, 0], q_g), rms_norm(qkv[:, :, 1], k_g), qkv[:, :, 2]


def na_context(h, w_in, q_g, k_g, w_out):
    b, n, _ = h.shape
    q, k, v = na_qkv(h, w_in, q_g, k_g)
    o = attend_blocks(q, k, v, NA_HEAD_DIM ** -0.5)
    return o.reshape(b, n, NA_HEADS * NA_HEAD_DIM) @ w_out, k, v


def na_latent(h, k_ctx, v_ctx, w_in, q_g, k_g, rpb, w_out):
    b, n, _ = h.shape
    rows = n // GRID_W
    kr = min(NA_ROWS, rows)
    q, k, v = na_qkv(h, w_in, q_g, k_g)
    qg = q.reshape(b, rows, GRID_W, NA_HEADS, NA_HEAD_DIM)
    kg = k.reshape(b, rows, GRID_W, NA_HEADS, NA_HEAD_DIM)
    vg = v.reshape(b, rows, GRID_W, NA_HEADS, NA_HEAD_DIM)
    col = np.arange(GRID_W)
    col_start = np.clip(col - NA_COLS // 2, 0, GRID_W - NA_COLS)
    col_idx = col_start[:, None] + np.arange(NA_COLS)[None, :]
    col_bias_idx = col_idx - col[:, None] + NA_COLS - 1
    scale = NA_HEAD_DIM ** -0.5
    n_loc = kr * NA_COLS

    def one_row(r):
        r0 = jnp.clip(r - kr // 2, 0, rows - kr)
        k_rows = lax.dynamic_slice_in_dim(kg, r0, kr, axis=1)
        v_rows = lax.dynamic_slice_in_dim(vg, r0, kr, axis=1)
        k_win = k_rows[:, :, col_idx]
        v_win = v_rows[:, :, col_idx]
        q_row = lax.dynamic_index_in_dim(qg, r, axis=1, keepdims=False)
        row_bias_idx = r0 + jnp.arange(kr) - r + NA_ROWS - 1
        bias = rpb[:, row_bias_idx][:, :, col_bias_idx].transpose(0, 2, 1, 3)
        s_loc = jnp.einsum('bqhd,brqchd->bhqrc', q_row, k_win, preferred_element_type=jnp.float32) * scale
        s_loc = s_loc + bias.astype(jnp.float32)
        s_ctx = jnp.einsum('bqhd,blhd->bhql', q_row, k_ctx, preferred_element_type=jnp.float32) * scale
        s = jnp.concatenate([s_loc.reshape(b, NA_HEADS, GRID_W, n_loc), s_ctx], axis=-1)
        p = jax.nn.softmax(s, axis=-1).astype(v.dtype)
        p_loc = p[..., :n_loc].reshape(b, NA_HEADS, GRID_W, kr, NA_COLS)
        return (jnp.einsum('bhqrc,brqchd->bqhd', p_loc, v_win)
                + jnp.einsum('bhql,blhd->bqhd', p[..., n_loc:], v_ctx))

    o = lax.map(one_row, jnp.arange(rows))
    o = o.transpose(1, 0, 2, 3, 4).reshape(b, n, NA_HEADS * NA_HEAD_DIM)
    return o @ w_out


def mla_latents(h, w_in, q_a_g, kv_a_g, w_q_up, q_g):
    b, n, _ = h.shape
    proj = h @ w_in
    cq = rms_norm(proj[..., :MLA_Q_LORA], q_a_g)
    ckv = rms_norm(proj[..., MLA_Q_LORA:MLA_Q_LORA + MLA_KV_LORA], kv_a_g)
    kpe = proj[..., MLA_Q_LORA + MLA_KV_LORA:]
    q = rms_norm((cq @ w_q_up).reshape(b, n, MLA_HEADS, MLA_QK_DIM), q_g)
    return q, ckv, kpe


def mla_keys_values(ckv, kpe, w_kv_up, k_g):
    b, n, _ = ckv.shape
    kv = (ckv @ w_kv_up).reshape(b, n, MLA_HEADS, MLA_NOPE_DIM + MLA_V_DIM)
    k_pe = jnp.broadcast_to(kpe[:, :, None, :], (b, n, MLA_HEADS, MLA_ROPE_DIM)).astype(kv.dtype)
    k = rms_norm(jnp.concatenate([kv[..., :MLA_NOPE_DIM], k_pe], axis=-1), k_g)
    return k, kv[..., MLA_NOPE_DIM:]


def rope_tail(x, cos, sin):
    return jnp.concatenate([x[..., :MLA_NOPE_DIM], apply_rope(x[..., MLA_NOPE_DIM:], cos, sin)], axis=-1)


def mla_context(h, w_in, q_a_g, kv_a_g, w_q_up, w_kv_up, q_g, k_g, w_out):
    b, n, _ = h.shape
    q, ckv, kpe = mla_latents(h, w_in, q_a_g, kv_a_g, w_q_up, q_g)
    k, v = mla_keys_values(ckv, kpe, w_kv_up, k_g)
    o = attend_blocks(q, k, v, MLA_QK_DIM ** -0.5)
    return o.reshape(b, n, MLA_HEADS * MLA_V_DIM) @ w_out, ckv, kpe


def mla_latent(h, ckv_ctx, kpe_ctx, w_in, q_a_g, kv_a_g, w_q_up, w_kv_up, q_g, k_g, w_out):
    b, n, _ = h.shape
    q, ckv, kpe = mla_latents(h, w_in, q_a_g, kv_a_g, w_q_up, q_g)
    k, v = mla_keys_values(ckv, kpe, w_kv_up, k_g)
    cos, sin = axial_rope(n, MLA_ROPE_DIM)
    q = rope_tail(q, cos, sin)
    k = rope_tail(k, cos, sin)
    k_c, v_c = mla_keys_values(ckv_ctx, kpe_ctx, w_kv_up, k_g)
    keys = jnp.concatenate([k, k_c.astype(k.dtype)], axis=1)
    vals = jnp.concatenate([v, v_c.astype(v.dtype)], axis=1)
    o = attend_blocks(q, keys, vals, MLA_QK_DIM ** -0.5)
    return o.reshape(b, n, MLA_HEADS * MLA_V_DIM) @ w_out


def centred_dwconv(x, w):
    return lax.conv_general_dilated(
        x, w[:, None, :], window_strides=(1,),
        padding=[(GDN_CONV_W // 2, GDN_CONV_W // 2)],
        dimension_numbers=('NWC', 'WIO', 'NWC'),
        feature_group_count=x.shape[-1])


def gated_delta_chunked(q, k, v, g, beta, s0):
    b, t, h, _ = q.shape
    dv = v.shape[-1]
    n = t // GDN_CHUNK
    c = GDN_CHUNK

    def to_chunks(x):
        return x.reshape(b, n, c, h, x.shape[-1]).transpose(1, 0, 3, 2, 4)

    qc, kc, vc = to_chunks(q), to_chunks(k), to_chunks(v)
    gc = jnp.cumsum(g.reshape(b, n, c, h).transpose(1, 0, 3, 2), axis=-1)
    bc = beta.reshape(b, n, c, h).transpose(1, 0, 3, 2)
    kb = kc * bc[..., None]
    vb = vc * bc[..., None]
    incl = np.tril(np.ones((c, c), dtype=bool))
    strict = np.tril(np.ones((c, c), dtype=bool), -1)
    diff = gc[..., :, None] - gc[..., None, :]
    decay = jnp.where(incl, jnp.exp(jnp.where(incl, diff, 0.0)), 0.0)
    a_strict = jnp.where(strict, jnp.einsum('nbhcd,nbhed->nbhce', kb, kc) * decay, 0.0)
    eye = jnp.eye(c, dtype=jnp.float32)
    tmat = lax.linalg.triangular_solve(eye + a_strict, jnp.broadcast_to(eye, a_strict.shape),
                                       left_side=True, lower=True, unit_diagonal=True)
    u = jnp.einsum('nbhce,nbhed->nbhcd', tmat, vb)
    w = jnp.einsum('nbhce,nbhed->nbhcd', tmat, kb * jnp.exp(gc)[..., None])

    def step(state, xs):
        qi, ki, ui, wi, gi, di = xs
        attn = jnp.einsum('bhcd,bhed->bhce', qi, ki) * di
        v_new = ui - jnp.einsum('bhcd,bhde->bhce', wi, state)
        o = (jnp.einsum('bhcd,bhde->bhce', qi * jnp.exp(gi)[..., None], state)
             + jnp.einsum('bhce,bhed->bhcd', attn, v_new))
        g_last = gi[..., -1]
        state = (state * jnp.exp(g_last)[..., None, None]
                 + jnp.einsum('bhcd,bhce->bhde', ki * jnp.exp(g_last[..., None] - gi)[..., None], v_new))
        return state, o

    s_final, o = lax.scan(step, s0, (qc, kc, u, w, gc, decay))
    o = o.transpose(1, 0, 3, 2, 4).reshape(b, t, h, dv)
    return o, s_final


def gdn_mix(h, s_fwd0, s_bwd0, w_in, conv_w, a_log, dt_bias, norm_g, w_out):
    b, n, _ = h.shape
    f32 = jnp.float32
    proj = h @ w_in
    qkv = jax.nn.silu(centred_dwconv(proj[..., :GDN_QKV], conv_w)).astype(f32)
    z = proj[..., GDN_QKV:GDN_QKV + GDN_KEY].reshape(b, n, GDN_HEADS, GDN_HEAD_DIM)
    ab = proj[..., GDN_QKV + GDN_KEY:].astype(f32).reshape(b, n, 2, 2, GDN_HEADS)
    q = l2norm(qkv[..., :GDN_KEY].reshape(b, n, GDN_HEADS, GDN_HEAD_DIM)) * (GDN_HEAD_DIM ** -0.5)
    k = l2norm(qkv[..., GDN_KEY:2 * GDN_KEY].reshape(b, n, GDN_HEADS, GDN_HEAD_DIM))
    v = qkv[..., 2 * GDN_KEY:].reshape(b, n, GDN_HEADS, GDN_HEAD_DIM)
    g = -jnp.exp(a_log.astype(f32)) * jax.nn.softplus(ab[:, :, :, 0] + dt_bias.astype(f32))
    beta = jax.nn.sigmoid(ab[:, :, :, 1])
    o_f, s_f = gated_delta_chunked(q, k, v, g[:, :, 0], beta[:, :, 0], s_fwd0.astype(f32))
    flip = lambda a: jnp.flip(a, axis=1)
    o_b, s_b = gated_delta_chunked(flip(q), flip(k), flip(v), flip(g[:, :, 1]), flip(beta[:, :, 1]),
                                   s_bwd0.astype(f32))
    o = (o_f + flip(o_b)).astype(h.dtype)
    o = rms_norm(o, norm_g) * jax.nn.silu(z)
    return o.reshape(b, n, GDN_KEY) @ w_out, s_f, s_b


def moe_ffn(h, w_router, b_router, w_gate_up, b_gate_up, w_down, b_down):
    shape = h.shape
    x = h.reshape(-1, shape[-1])
    logits = jnp.matmul(x, w_router, preferred_element_type=jnp.float32) + b_router.astype(jnp.float32)
    top_val, top_idx = lax.top_k(logits, TOP_K)
    top_w = jax.nn.softmax(top_val, axis=-1)
    gates = jnp.einsum('tk,tke->te', top_w,
                       jax.nn.one_hot(top_idx, N_EXPERTS, dtype=jnp.float32)).astype(x.dtype)
    out = jnp.zeros_like(x)
    for e in range(N_EXPERTS):
        gu = x @ w_gate_up[e] + b_gate_up[e]
        glu = jnp.minimum(gu[:, :D_EXPERT], SWIGLU_LIMIT)
        lin = jnp.clip(gu[:, D_EXPERT:], -SWIGLU_LIMIT, SWIGLU_LIMIT)
        act = (lin + 1) * glu * jax.nn.sigmoid(SWIGLU_ALPHA * glu)
        out = out + gates[:, e:e + 1] * (act @ w_down[e] + b_down[e])
    return out.reshape(shape)


def setup_inputs(seed: int = 0) -> dict:
    key = jax.random.key(seed)
    ks = iter(jax.random.split(key, 48))
    f32 = jnp.float32

    def nrm(shape, scale=1.0):
        return jax.random.normal(next(ks), shape, f32) * scale

    def gain(shape):
        return 1.0 + nrm(shape, 0.02)

    d = D_MODEL
    inp = {}
    inp['x_prompt'] = nrm((BATCH, SEQ, d))
    inp['x_sample'] = nrm((DEC_BATCH, DEC_SEQ, d))
    inp['cache_na_k'] = nrm((DEC_BATCH, N_NA, PAST_LEN, NA_HEADS, NA_HEAD_DIM))
    inp['cache_na_v'] = nrm((DEC_BATCH, N_NA, PAST_LEN, NA_HEADS, NA_HEAD_DIM))
    inp['cache_mla_ckv'] = nrm((DEC_BATCH, N_MLA, PAST_LEN, MLA_KV_LORA))
    inp['cache_mla_kpe'] = nrm((DEC_BATCH, N_MLA, PAST_LEN, MLA_ROPE_DIM))
    inp['state_gdn_fwd'] = nrm((DEC_BATCH, N_GDN, GDN_HEADS, GDN_HEAD_DIM, GDN_HEAD_DIM))
    inp['state_gdn_bwd'] = nrm((DEC_BATCH, N_GDN, GDN_HEADS, GDN_HEAD_DIM, GDN_HEAD_DIM))
    inp['c'] = nrm((DEC_BATCH, d))
    inp['c_ctx'] = nrm((d,))
    inp['ada_w'] = nrm((DEPTH, d, 6 * d), 0.5 * d ** -0.5)
    inp['ada_b'] = nrm((DEPTH, 6 * d), 0.02)
    inp['norm_mix_g'] = gain((DEPTH, d))
    inp['norm_ffn_g'] = gain((DEPTH, d))
    inp['na_w_in'] = nrm((N_NA, d, 3 * NA_HEADS * NA_HEAD_DIM), d ** -0.5)
    inp['na_q_g'] = gain((N_NA, NA_HEAD_DIM))
    inp['na_k_g'] = gain((N_NA, NA_HEAD_DIM))
    inp['na_rpb'] = nrm((N_NA, NA_HEADS, 2 * NA_ROWS - 1, 2 * NA_COLS - 1), 0.5)
    inp['na_w_out'] = nrm((N_NA, NA_HEADS * NA_HEAD_DIM, d), (NA_HEADS * NA_HEAD_DIM) ** -0.5)
    inp['mla_w_in'] = nrm((N_MLA, d, MLA_Q_LORA + MLA_KV_LORA + MLA_ROPE_DIM), d ** -0.5)
    inp['mla_q_a_g'] = gain((N_MLA, MLA_Q_LORA))
    inp['mla_kv_a_g'] = gain((N_MLA, MLA_KV_LORA))
    inp['mla_w_q_up'] = nrm((N_MLA, MLA_Q_LORA, MLA_HEADS * MLA_QK_DIM), MLA_Q_LORA ** -0.5)
    inp['mla_w_kv_up'] = nrm((N_MLA, MLA_KV_LORA, MLA_HEADS * (MLA_NOPE_DIM + MLA_V_DIM)), MLA_KV_LORA ** -0.5)
    inp['mla_q_g'] = gain((N_MLA, MLA_QK_DIM))
    inp['mla_k_g'] = gain((N_MLA, MLA_QK_DIM))
    inp['mla_w_out'] = nrm((N_MLA, MLA_HEADS * MLA_V_DIM, d), (MLA_HEADS * MLA_V_DIM) ** -0.5)
    inp['gdn_w_in'] = nrm((N_GDN, d, GDN_QKV + GDN_KEY + 4 * GDN_HEADS), d ** -0.5)
    inp['gdn_conv_w'] = nrm((N_GDN, GDN_CONV_W, GDN_QKV), GDN_CONV_W ** -0.5)
    inp['gdn_a_log'] = jnp.log(jax.random.uniform(next(ks), (N_GDN, 2, GDN_HEADS), f32, 1.0, 16.0))
    dt = jnp.exp(jax.random.uniform(next(ks), (N_GDN, 2, GDN_HEADS), f32, math.log(1e-3), math.log(1e-1)))
    inp['gdn_dt_bias'] = dt + jnp.log(-jnp.expm1(-dt))
    inp['gdn_norm_g'] = gain((N_GDN, GDN_HEAD_DIM))
    inp['gdn_w_out'] = nrm((N_GDN, GDN_KEY, d), GDN_KEY ** -0.5)
    inp['moe_w_router'] = nrm((DEPTH, d, N_EXPERTS), d ** -0.5)
    inp['moe_b_router'] = nrm((DEPTH, N_EXPERTS), 0.01)
    inp['moe_w_gate_up'] = nrm((DEPTH, N_EXPERTS, d, 2 * D_EXPERT), d ** -0.5)
    inp['moe_b_gate_up'] = nrm((DEPTH, N_EXPERTS, 2 * D_EXPERT), 0.02)
    inp['moe_w_down'] = nrm((DEPTH, N_EXPERTS, D_EXPERT, d), D_EXPERT ** -0.5)
    inp['moe_b_down'] = nrm((DEPTH, N_EXPERTS, d), 0.02)
    return inp


def reference(x_prompt, x_sample, cache_na_k, cache_na_v, cache_mla_ckv, cache_mla_kpe,
              state_gdn_fwd, state_gdn_bwd, c, c_ctx, ada_w, ada_b, norm_mix_g, norm_ffn_g,
              na_w_in, na_q_g, na_k_g, na_rpb, na_w_out,
              mla_w_in, mla_q_a_g, mla_kv_a_g, mla_w_q_up, mla_w_kv_up, mla_q_g, mla_k_g, mla_w_out,
              gdn_w_in, gdn_conv_w, gdn_a_log, gdn_dt_bias, gdn_norm_g, gdn_w_out,
              moe_w_router, moe_b_router, moe_w_gate_up, moe_b_gate_up, moe_w_down, moe_b_down):
    xp, xs = x_prompt, x_sample
    silu_ctx = jax.nn.silu(c_ctx)
    silu_c = jax.nn.silu(c)
    new_na_k, new_na_v, new_mla_ckv, new_mla_kpe, new_gdn_fwd, new_gdn_bwd = [], [], [], [], [], []
    for i in range(DEPTH):
        kind, j = i % N_MIXERS, i // N_MIXERS
        mod_p = (silu_ctx @ ada_w[i] + ada_b[i]).reshape(6, D_MODEL)
        mod_s = (silu_c @ ada_w[i] + ada_b[i]).reshape(-1, 6, 1, D_MODEL)
        hp = modulate(xp, norm_mix_g[i], mod_p[0], mod_p[1])
        hs = modulate(xs, norm_mix_g[i], mod_s[:, 0], mod_s[:, 1])
        if kind == 0:
            op, kc, vc = na_context(hp, na_w_in[j], na_q_g[j], na_k_g[j], na_w_out[j])
            ol = na_latent(hs, cache_na_k[:, j], cache_na_v[:, j], na_w_in[j], na_q_g[j], na_k_g[j],
                           na_rpb[j], na_w_out[j])
            new_na_k.append(kc)
            new_na_v.append(vc)
        elif kind == 1:
            mla_p = (mla_w_in[j], mla_q_a_g[j], mla_kv_a_g[j], mla_w_q_up[j], mla_w_kv_up[j],
                     mla_q_g[j], mla_k_g[j], mla_w_out[j])
            op, ckv, kpe = mla_context(hp, *mla_p)
            ol = mla_latent(hs, cache_mla_ckv[:, j], cache_mla_kpe[:, j], *mla_p)
            new_mla_ckv.append(ckv)
            new_mla_kpe.append(kpe)
        else:
            gdn_p = (gdn_w_in[j], gdn_conv_w[j], gdn_a_log[j], gdn_dt_bias[j], gdn_norm_g[j], gdn_w_out[j])
            zeros = jnp.zeros((xp.shape[0], GDN_HEADS, GDN_HEAD_DIM, GDN_HEAD_DIM), jnp.float32)
            op, s_f, s_b = gdn_mix(hp, zeros, zeros, *gdn_p)
            ol, _, _ = gdn_mix(hs, state_gdn_fwd[:, j], state_gdn_bwd[:, j], *gdn_p)
            new_gdn_fwd.append(s_f.astype(xp.dtype))
            new_gdn_bwd.append(s_b.astype(xp.dtype))
        xp = xp + mod_p[2] * op
        xs = xs + mod_s[:, 2] * ol
        moe_p = (moe_w_router[i], moe_b_router[i], moe_w_gate_up[i], moe_b_gate_up[i],
                 moe_w_down[i], moe_b_down[i])
        xp = xp + mod_p[5] * moe_ffn(modulate(xp, norm_ffn_g[i], mod_p[3], mod_p[4]), *moe_p)
        xs = xs + mod_s[:, 5] * moe_ffn(modulate(xs, norm_ffn_g[i], mod_s[:, 3], mod_s[:, 4]), *moe_p)
    return (xp, xs,
            jnp.stack(new_na_k, axis=1), jnp.stack(new_na_v, axis=1),
            jnp.stack(new_mla_ckv, axis=1), jnp.stack(new_mla_kpe, axis=1),
            jnp.stack(new_gdn_fwd, axis=1), jnp.stack(new_gdn_bwd, axis=1))
```

```python
import functools
import math

import numpy as np
import jax
import jax.numpy as jnp
from jax import lax
from jax.experimental import pallas as pl
from jax.experimental.pallas import tpu as pltpu

F32 = jnp.float32
BF16 = jnp.bfloat16
HIGHEST = lax.Precision.HIGHEST

D_MODEL = 1024
BATCH = 32
SEQ = 256
DEPTH = 4
DEC_BATCH = 8
DEC_SEQ = 2048
PAST_LEN = 512
GRID_W = 64
N_MIXERS = 3
NA_HEADS = 16
NA_HEAD_DIM = 64
NA_ROWS = 8
NA_COLS = 16
MLA_HEADS = 16
MLA_NOPE_DIM = 64
MLA_ROPE_DIM = 32
MLA_QK_DIM = MLA_NOPE_DIM + MLA_ROPE_DIM
MLA_V_DIM = 64
MLA_Q_LORA = 256
MLA_KV_LORA = 128
GDN_HEADS = 8
GDN_HEAD_DIM = 128
GDN_KEY = GDN_HEADS * GDN_HEAD_DIM
GDN_QKV = 3 * GDN_KEY
GDN_CONV_W = 5
GDN_CHUNK = 64
N_EXPERTS = 32
TOP_K = 4
D_EXPERT = D_MODEL
SWIGLU_ALPHA = 1.702
SWIGLU_LIMIT = 7.0
ROPE_BASE = 10000.0
EPS = 1e-6

PROMPT_TOK = BATCH * SEQ
SAMPLE_TOK = DEC_BATCH * DEC_SEQ
TOKENS = PROMPT_TOK + SAMPLE_TOK
N_GROUPS = 1 + DEC_BATCH

LANE = 128
TM = 512
MOE_TM = 512
MOE_ROWS = TOKENS * TOP_K
MOE_TILES = MOE_ROWS // MOE_TM + N_EXPERTS
MOE_PAD_ROWS = MOE_TILES * MOE_TM
ROW_DMA_BLOCK = 256
VMEM_LIMIT = 56 * 1024 * 1024
NEG_BIG = -1e30


def _cparams(n_axes, vmem=None):
    return pltpu.CompilerParams(dimension_semantics=("arbitrary",) * n_axes,
                                vmem_limit_bytes=vmem)


def _dot(a, b):
    return jnp.dot(a, b, preferred_element_type=F32)


def _dot_nt(a, b):
    return lax.dot_general(a, b, (((1,), (1,)), ((), ())), preferred_element_type=F32)


def _dot_tn(a, b):
    return lax.dot_general(a, b, (((0,), (0,)), ((), ())), preferred_element_type=F32)


def _dot_hi(a, b):
    return jnp.dot(a, b, preferred_element_type=F32, precision=HIGHEST)


def _group_of_tile(i, tm):
    npt = PROMPT_TOK // tm
    spb = DEC_SEQ // tm
    return jnp.where(i < npt, 0, 1 + (i - npt) // spb)


def _block_diag_ones(n, group):
    idx = np.arange(n) // group
    return jnp.asarray((idx[:, None] == idx[None, :]).astype(np.float32), BF16)


def _ada_kernel(c_ref, w_ref, b_ref, o_ref):
    c = c_ref[...]
    s = c * jax.nn.sigmoid(c)
    o_ref[0] = _dot_hi(s, w_ref[0]) + b_ref[0]


def _ada_mods(c_all, ada_w, ada_b):
    tn = 512
    n = 6 * D_MODEL
    return pl.pallas_call(
        _ada_kernel,
        grid=(DEPTH, n // tn),
        in_specs=[pl.BlockSpec((16, D_MODEL), lambda l, j: (0, 0)),
                  pl.BlockSpec((1, D_MODEL, tn), lambda l, j: (l, 0, j)),
                  pl.BlockSpec((1, 1, tn), lambda l, j: (l, 0, j))],
        out_specs=pl.BlockSpec((1, 16, tn), lambda l, j: (l, 0, j)),
        out_shape=jax.ShapeDtypeStruct((DEPTH, 16, n), F32),
        compiler_params=_cparams(2),
        name="ada_mods",
    )(c_all, ada_w, ada_b.reshape(DEPTH, 1, n))


def _modulate(x, g, mod_ref, shift_idx, scale_idx):
    ms = jnp.mean(x * x, axis=-1, keepdims=True)
    y = x * lax.rsqrt(ms + EPS) * g
    return y * (1.0 + mod_ref[0, scale_idx]) + mod_ref[0, shift_idx]


def _group_norm(acc, bd, gain, inv_count):
    ss = _dot((acc * acc).astype(BF16), bd)
    return acc * lax.rsqrt(ss * inv_count + EPS) * gain


def _modmm_kernel(*refs, shift_idx, scale_idx, n_norm_tiles, inv_count):
    if n_norm_tiles:
        x_ref, g_ref, mod_ref, w_ref, gain_ref, bd_ref, o_ref, h_ref = refs
    else:
        x_ref, g_ref, mod_ref, w_ref, o_ref, h_ref = refs
    j = pl.program_id(1)

    @pl.when(j == 0)
    def _():
        h_ref[...] = _modulate(x_ref[...], g_ref[...], mod_ref, shift_idx, scale_idx).astype(BF16)

    acc = _dot(h_ref[...], w_ref[...])
    if n_norm_tiles:
        @pl.when(j < n_norm_tiles)
        def _():
            o_ref[...] = _group_norm(acc, bd_ref[...], gain_ref[...], inv_count).astype(o_ref.dtype)

        @pl.when(j >= n_norm_tiles)
        def _():
            o_ref[...] = acc.astype(o_ref.dtype)
    else:
        o_ref[...] = acc.astype(o_ref.dtype)


def _modmm(x, g, mod, w, *, shift_idx, scale_idx, tn, out_dtype=F32, norm=None, name="modmm"):
    t, d = x.shape
    n = w.shape[1]
    in_specs = [pl.BlockSpec((TM, d), lambda i, j: (i, 0)),
                pl.BlockSpec((1, d), lambda i, j: (0, 0)),
                pl.BlockSpec((1, 6, 1, d), lambda i, j: (_group_of_tile(i, TM), 0, 0, 0)),
                pl.BlockSpec((d, tn), lambda i, j: (0, j))]
    args = [x, g, mod, w]
    n_norm_tiles, inv_count = 0, 1.0
    if norm is not None:
        n_norm_tiles, gains, group = norm
        inv_count = 1.0 / group
        in_specs += [pl.BlockSpec((1, tn), lambda i, j: (0, j)),
                     pl.BlockSpec((tn, tn), lambda i, j: (0, 0))]
        args += [gains, _block_diag_ones(tn, group)]
    return pl.pallas_call(
        functools.partial(_modmm_kernel, shift_idx=shift_idx, scale_idx=scale_idx,
                          n_norm_tiles=n_norm_tiles, inv_count=inv_count),
        grid=(t // TM, n // tn),
        in_specs=in_specs,
        out_specs=pl.BlockSpec((TM, tn), lambda i, j: (i, j)),
        out_shape=jax.ShapeDtypeStruct((t, n), out_dtype),
        scratch_shapes=[pltpu.VMEM((TM, d), BF16)],
        compiler_params=_cparams(2, VMEM_LIMIT),
        name=name,
    )(*args)


def _mm_res_kernel(a_ref, w_ref, res_ref, gate_ref, o_ref):
    o_ref[...] = res_ref[...] + gate_ref[0, 0] * _dot(a_ref[...], w_ref[...])


def _mm_res(a, w, res, mod, gate_idx, *, tn=512, name="mm_res"):
    t, k = a.shape
    n = w.shape[1]
    return pl.pallas_call(
        _mm_res_kernel,
        grid=(t // TM, n // tn),
        in_specs=[pl.BlockSpec((TM, k), lambda i, j: (i, 0)),
                  pl.BlockSpec((k, tn), lambda i, j: (0, j)),
                  pl.BlockSpec((TM, tn), lambda i, j: (i, j)),
                  pl.BlockSpec((1, 1, 1, tn), lambda i, j: (_group_of_tile(i, TM), gate_idx, 0, j))],
        out_specs=pl.BlockSpec((TM, tn), lambda i, j: (i, j)),
        out_shape=jax.ShapeDtypeStruct((t, n), F32),
        compiler_params=_cparams(2, VMEM_LIMIT),
        name=name,
    )(a, w, res, mod)


def _rope_slots(y, c, s1, s2):
    outs = []
    for s in range(y.shape[1] // LANE):
        ys = y[:, s * LANE:(s + 1) * LANE]
        outs.append(ys * c + pltpu.roll(ys, LANE - MLA_ROPE_DIM // 2, 1) * s1
                    + pltpu.roll(ys, MLA_ROPE_DIM // 2, 1) * s2)
    return jnp.concatenate(outs, axis=1)


def _mmn_kernel(*refs, n_norm_tiles, inv_count, rope):
    if rope:
        a_ref, w_ref, gain_ref, bd_ref, c_ref, s1_ref, s2_ref, o_ref = refs
    else:
        a_ref, w_ref, gain_ref, bd_ref, o_ref = refs
    j = pl.program_id(1)
    acc = _dot(a_ref[...], w_ref[...])

    @pl.when(j < n_norm_tiles)
    def _():
        y = _group_norm(acc, bd_ref[...], gain_ref[...], inv_count)
        if rope:
            y = _rope_slots(y, c_ref[0], s1_ref[0], s2_ref[0])
        o_ref[...] = y.astype(o_ref.dtype)

    @pl.when(j >= n_norm_tiles)
    def _():
        o_ref[...] = acc.astype(o_ref.dtype)


def _mmn(a, w, gains, *, n_norm_tiles, count, rope=None, tn=256, name="mmn"):
    t, k = a.shape
    n = w.shape[1]
    in_specs = [pl.BlockSpec((TM, k), lambda i, j: (i, 0)),
                pl.BlockSpec((k, tn), lambda i, j: (0, j)),
                pl.BlockSpec((1, tn), lambda i, j: (0, j)),
                pl.BlockSpec((tn, tn), lambda i, j: (0, 0))]
    args = [a, w, gains, _block_diag_ones(tn, LANE)]
    if rope is not None:
        tables, pos_fn = rope
        for q in range(3):
            in_specs.append(pl.BlockSpec((1, TM, LANE), lambda i, j, q=q: (q, pos_fn(i), 0)))
            args.append(tables)
    return pl.pallas_call(
        functools.partial(_mmn_kernel, n_norm_tiles=n_norm_tiles, inv_count=1.0 / count,
                          rope=rope is not None),
        grid=(t // TM, n // tn),
        in_specs=in_specs,
        out_specs=pl.BlockSpec((TM, tn), lambda i, j: (i, j)),
        out_shape=jax.ShapeDtypeStruct((t, n), BF16),
        compiler_params=_cparams(2, VMEM_LIMIT),
        name=name,
    )(*args)


def _lane_half_mask(h, width=LANE):
    lane = lax.broadcasted_iota(jnp.int32, (1, width), 1)
    return (lane // (width // 2)) == h


def _pair_attn_kernel(q_ref, k_ref, v_ref, o_ref, *, slot):
    q = q_ref[0]
    k = k_ref[0]
    v = v_ref[0].astype(BF16)
    out = jnp.zeros(o_ref.shape[1:], F32)
    for h in range(2):
        mask = _lane_half_mask(h)
        if slot == LANE:
            qh = q[:, h * LANE:(h + 1) * LANE].astype(BF16)
            kh = k[:, h * LANE:(h + 1) * LANE].astype(BF16)
        else:
            qh = jnp.where(mask, q, 0).astype(BF16)
            kh = k.astype(BF16)
        s = _dot_nt(qh, kh)
        m = jnp.max(s, axis=-1, keepdims=True)
        p = jnp.exp(s - m)
        l = jnp.sum(p, axis=-1, keepdims=True)
        vh = jnp.where(mask, v, jnp.zeros_like(v))
        out = out + _dot(p.astype(BF16), vh) * (1.0 / l)
    o_ref[0] = out.astype(o_ref.dtype)


def _pair_attn(q_arr, k_arr, v_arr, *, q_col, k_col, v_col, slot, tq, n_pairs, name):
    b, nq, _ = q_arr.shape
    nk = k_arr.shape[1]
    qw = 2 * slot
    return pl.pallas_call(
        functools.partial(_pair_attn_kernel, slot=slot),
        grid=(b, n_pairs, nq // tq),
        in_specs=[pl.BlockSpec((1, tq, qw), lambda bi, hp, qi: (bi, qi, q_col + hp)),
                  pl.BlockSpec((1, nk, qw), lambda bi, hp, qi: (bi, 0, k_col + hp)),
                  pl.BlockSpec((1, nk, LANE), lambda bi, hp, qi: (bi, 0, v_col + hp))],
        out_specs=pl.BlockSpec((1, tq, LANE), lambda bi, hp, qi: (bi, qi, hp)),
        out_shape=jax.ShapeDtypeStruct((b, nq, n_pairs * LANE), BF16),
        compiler_params=_cparams(3, VMEM_LIMIT),
        name=name,
    )(q_arr, k_arr, v_arr)


NA_QBLOCK_ROWS = 8
NA_WIN = NA_ROWS * GRID_W


def _na_lat_kernel(q_ref, k_ref, v_ref, kc_ref, vc_ref, bias_ref, o_ref, kbf, vbf, *, rows):
    qi = pl.program_id(2)

    @pl.when(qi == 0)
    def _():
        kbf[...] = k_ref[0].astype(BF16)
        v = v_ref[0]
        for h in range(2):
            vbf[h] = jnp.where(_lane_half_mask(h), v, 0.0).astype(BF16)

    q = q_ref[0]
    kc = kc_ref[0].astype(BF16)
    vc = vc_ref[0]
    out = [jnp.zeros((GRID_W, LANE), F32) for _ in range(NA_QBLOCK_ROWS)]
    for h in range(2):
        mask = _lane_half_mask(h)
        qh = jnp.where(mask, q, 0.0).astype(BF16)
        vch = jnp.where(mask, vc, 0.0).astype(BF16)
        s_c = _dot_nt(qh, kc)
        m_c = jnp.max(s_c, axis=-1, keepdims=True)
        p_c = jnp.exp(s_c - m_c)
        l_c = jnp.sum(p_c, axis=-1, keepdims=True)
        o_c = _dot(p_c.astype(BF16), vch)
        for rr in range(NA_QBLOCK_ROWS):
            r = qi * NA_QBLOCK_ROWS + rr
            r0 = jnp.clip(r - NA_ROWS // 2, 0, rows - NA_ROWS)
            start = pl.multiple_of(r0 * GRID_W, GRID_W)
            sl = slice(rr * GRID_W, (rr + 1) * GRID_W)
            k_rows = kbf[pl.ds(start, NA_WIN), :]
            v_rows = vbf[h, pl.ds(start, NA_WIN), :]
            s_l = _dot_nt(qh[sl], k_rows) + bias_ref[r - r0, h]
            m_l = jnp.max(s_l, axis=-1, keepdims=True)
            m = jnp.maximum(m_l, m_c[sl])
            p_l = jnp.exp(s_l - m)
            a_c = jnp.exp(m_c[sl] - m)
            l = jnp.sum(p_l, axis=-1, keepdims=True) + a_c * l_c[sl]
            o = _dot(p_l.astype(BF16), v_rows) + a_c * o_c[sl]
            out[rr] = out[rr] + o * (1.0 / l)
    for rr in range(NA_QBLOCK_ROWS):
        o_ref[0, rr * GRID_W:(rr + 1) * GRID_W, :] = out[rr].astype(o_ref.dtype)


def _na_bias_table(rpb):
    col = np.arange(GRID_W)
    col_start = np.clip(col - NA_COLS // 2, 0, GRID_W - NA_COLS)
    kc = np.arange(GRID_W)
    inside = (kc[None, :] >= col_start[:, None]) & (kc[None, :] < col_start[:, None] + NA_COLS)
    cidx = np.clip(kc[None, :] - col[:, None] + NA_COLS - 1, 0, 2 * NA_COLS - 2)
    p = np.arange(NA_ROWS)
    ridx = np.arange(NA_ROWS)[None, :] - p[:, None] + NA_ROWS - 1
    tab = rpb[:, ridx[:, :, None, None], cidx[None, None, :, :]]
    tab = jnp.where(jnp.asarray(inside)[None, None, None], tab, NEG_BIG)
    tab = tab.transpose(1, 0, 3, 2, 4)
    return tab.reshape(NA_ROWS, NA_HEADS, GRID_W, NA_WIN).astype(F32)


def _na_latent_attn(qkv, kctx, vctx, bias):
    b, n, _ = qkv.shape
    rows = n // GRID_W
    n_pairs = NA_HEADS // 2
    tq = NA_QBLOCK_ROWS * GRID_W
    past = kctx.shape[1]
    return pl.pallas_call(
        functools.partial(_na_lat_kernel, rows=rows),
        grid=(b, n_pairs, n // tq),
        in_specs=[pl.BlockSpec((1, tq, LANE), lambda bi, hp, qi: (bi, qi, hp)),
                  pl.BlockSpec((1, n, LANE), lambda bi, hp, qi: (bi, 0, n_pairs + hp)),
                  pl.BlockSpec((1, n, LANE), lambda bi, hp, qi: (bi, 0, 2 * n_pairs + hp)),
                  pl.BlockSpec((1, past, LANE), lambda bi, hp, qi: (bi, 0, hp)),
                  pl.BlockSpec((1, past, LANE), lambda bi, hp, qi: (bi, 0, hp)),
                  pl.BlockSpec((NA_ROWS, 2, GRID_W, NA_WIN), lambda bi, hp, qi: (0, hp, 0, 0))],
        out_specs=pl.BlockSpec((1, tq, LANE), lambda bi, hp, qi: (bi, qi, hp)),
        out_shape=jax.ShapeDtypeStruct((b, n, n_pairs * LANE), BF16),
        scratch_shapes=[pltpu.VMEM((n, LANE), BF16), pltpu.VMEM((2, n, LANE), BF16)],
        compiler_params=_cparams(3, VMEM_LIMIT),
        name="na_latent_attn",
    )(qkv, qkv, qkv, kctx, vctx, bias)


def _mla_lat_kernel(p_ref, qg_ref, kvg_ref, cq_ref, ckv_ref, kin_ref):
    p = p_ref[...]
    cq = p[:, :MLA_Q_LORA]
    cq = cq * lax.rsqrt(jnp.mean(cq * cq, axis=-1, keepdims=True) + EPS) * qg_ref[...]
    cq_ref[...] = cq.astype(BF16)
    ckv = p[:, MLA_Q_LORA:MLA_Q_LORA + MLA_KV_LORA]
    ckv = ckv * lax.rsqrt(jnp.mean(ckv * ckv, axis=-1, keepdims=True) + EPS) * kvg_ref[...]
    ckv_ref[...] = ckv
    tail = p[:, MLA_Q_LORA + MLA_KV_LORA:]
    kin_ref[...] = jnp.concatenate([ckv, tail], axis=1).astype(BF16)


def _mla_latents(proj, q_a_g, kv_a_g):
    t = proj.shape[0]
    w = proj.shape[1]
    return pl.pallas_call(
        _mla_lat_kernel,
        grid=(t // TM,),
        in_specs=[pl.BlockSpec((TM, w), lambda i: (i, 0)),
                  pl.BlockSpec((1, MLA_Q_LORA), lambda i: (0, 0)),
                  pl.BlockSpec((1, MLA_KV_LORA), lambda i: (0, 0))],
        out_specs=[pl.BlockSpec((TM, MLA_Q_LORA), lambda i: (i, 0)),
                   pl.BlockSpec((TM, MLA_KV_LORA), lambda i: (i, 0)),
                   pl.BlockSpec((TM, 2 * LANE), lambda i: (i, 0))],
        out_shape=[jax.ShapeDtypeStruct((t, MLA_Q_LORA), BF16),
                   jax.ShapeDtypeStruct((t, MLA_KV_LORA), F32),
                   jax.ShapeDtypeStruct((t, 2 * LANE), BF16)],
        compiler_params=_cparams(1),
        name="mla_latents",
    )(proj, q_a_g.reshape(1, -1), kv_a_g.reshape(1, -1))


def _rope_tables(n_pos, n_identity_rows_first=0, n_identity_rows_last=0):
    n_freq = MLA_ROPE_DIM // 4
    inv = ROPE_BASE ** (-jnp.arange(n_freq, dtype=F32) / n_freq)
    t = jnp.arange(n_pos)
    row = (t // GRID_W).astype(F32)
    col = (t % GRID_W).astype(F32)
    ang = jnp.concatenate([row[:, None] * inv, col[:, None] * inv], axis=-1)
    cos, sin = jnp.cos(ang), jnp.sin(ang)
    half = MLA_ROPE_DIM // 2
    ones = jnp.ones((n_pos, MLA_NOPE_DIM), F32)
    zeros = jnp.zeros((n_pos, MLA_NOPE_DIM), F32)
    pad1 = jnp.ones((n_pos, LANE - MLA_QK_DIM), F32)
    pad0 = jnp.zeros((n_pos, LANE - MLA_QK_DIM), F32)
    zh = jnp.zeros((n_pos, half), F32)
    c = jnp.concatenate([ones, cos, cos, pad1], axis=1)
    s1 = jnp.concatenate([zeros, -sin, zh, pad0], axis=1)
    s2 = jnp.concatenate([zeros, zh, sin, pad0], axis=1)
    tab = jnp.stack([c, s1, s2])

    def ident(nrows):
        return jnp.stack([jnp.ones((nrows, LANE), F32), jnp.zeros((nrows, LANE), F32),
                          jnp.zeros((nrows, LANE), F32)])

    parts = []
    if n_identity_rows_first:
        parts.append(ident(n_identity_rows_first))
    parts.append(tab)
    if n_identity_rows_last:
        parts.append(ident(n_identity_rows_last))
    return jnp.concatenate(parts, axis=1)


def _gdn_conv_kernel(x_ref, w_ref, o_ref, pad_ref, *, n):
    j = pl.program_id(1)
    halo = 8
    cw = x_ref.shape[2]
    pad_ref[0:halo, :] = jnp.zeros((halo, cw), F32)
    pad_ref[halo + n:2 * halo + n, :] = jnp.zeros((halo, cw), F32)
    pad_ref[halo:halo + n, :] = x_ref[0]
    acc = jnp.zeros((n, cw), F32)
    for t in range(GDN_CONV_W):
        off = halo - GDN_CONV_W // 2 + t
        acc = acc + pad_ref[off:off + n, :] * w_ref[t:t + 1, :]
    y = acc * jax.nn.sigmoid(acc)
    n_qk_blocks = 2 * GDN_KEY // cw
    n_q_blocks = GDN_KEY // cw
    outs = []
    for s in range(cw // LANE):
        ys = y[:, s * LANE:(s + 1) * LANE]
        ss = jnp.sum(ys * ys, axis=-1, keepdims=True)
        r = jnp.where(j < n_qk_blocks, lax.rsqrt(ss + EPS), 1.0)
        r = r * jnp.where(j < n_q_blocks, GDN_HEAD_DIM ** -0.5, 1.0)
        outs.append(ys * r)
    o_ref[0] = jnp.concatenate(outs, axis=1)


def _gdn_conv(proj3, conv_w):
    b, n, _ = proj3.shape
    cw = 512
    return pl.pallas_call(
        functools.partial(_gdn_conv_kernel, n=n),
        grid=(b, GDN_QKV // cw),
        in_specs=[pl.BlockSpec((1, n, cw), lambda bi, j: (bi, 0, j)),
                  pl.BlockSpec((GDN_CONV_W, cw), lambda bi, j: (0, j))],
        out_specs=pl.BlockSpec((1, n, cw), lambda bi, j: (bi, 0, j)),
        out_shape=jax.ShapeDtypeStruct((b, n, GDN_QKV), F32),
        scratch_shapes=[pltpu.VMEM((n + 16, cw), F32)],
        compiler_params=_cparams(2, VMEM_LIMIT),
        name="gdn_conv",
    )(proj3, conv_w)


def _gdn_gates_kernel(ab_ref, alog_ref, dtb_ref, uf_ref, ub_ref, gf_ref, gb_ref, beta_ref):
    ab = ab_ref[...]
    a = jnp.concatenate([ab[0:8], ab[16:24]], axis=0)
    bb = jnp.concatenate([ab[8:16], ab[24:32]], axis=0)
    z = a + dtb_ref[...]
    sp = jnp.maximum(z, 0.0) + jnp.log(1.0 + jnp.exp(-jnp.abs(z)))
    g = -jnp.exp(alog_ref[...]) * sp
    beta_ref[...] = jax.nn.sigmoid(bb)
    gf_ref[...] = _dot_hi(g, uf_ref[...])
    gb_ref[...] = _dot_hi(g, ub_ref[...])


def _gdn_gates(ab_t, a_log, dt_bias):
    t = ab_t.shape[1]
    tl = 256
    idx = np.arange(tl)
    same = (idx[:, None] // GDN_CHUNK) == (idx[None, :] // GDN_CHUNK)
    uf = jnp.asarray((same & (idx[:, None] <= idx[None, :])).astype(np.float32))
    ub = jnp.asarray((same & (idx[:, None] >= idx[None, :])).astype(np.float32))
    spec = pl.BlockSpec((16, tl), lambda i: (0, i))
    return pl.pallas_call(
        _gdn_gates_kernel,
        grid=(t // tl,),
        in_specs=[pl.BlockSpec((32, tl), lambda i: (0, i)),
                  pl.BlockSpec((16, 1), lambda i: (0, 0)),
                  pl.BlockSpec((16, 1), lambda i: (0, 0)),
                  pl.BlockSpec((tl, tl), lambda i: (0, 0)),
                  pl.BlockSpec((tl, tl), lambda i: (0, 0))],
        out_specs=[spec, spec, spec],
        out_shape=[jax.ShapeDtypeStruct((16, t), F32)] * 3,
        compiler_params=_cparams(1),
        name="gdn_gates",
    )(ab_t, a_log.reshape(16, 1), dt_bias.reshape(16, 1), uf, ub)


def _tri_inverse(a_strict):
    c = a_strict.shape[0]
    ri = lax.broadcasted_iota(jnp.int32, (c, c), 0)
    ci = lax.broadcasted_iota(jnp.int32, (c, c), 1)
    eye = (ri == ci).astype(F32)
    neg = -a_strict
    t = eye + neg
    pw = neg
    n_fac = int(math.log2(c)) - 1
    for _ in range(n_fac):
        pwb = pw.astype(BF16)
        pw = _dot(pwb, pwb)
        t = t + _dot(t.astype(BF16), pw.astype(BF16))
    a_hi = a_strict.astype(BF16)
    a_lo = (a_strict - a_hi.astype(F32)).astype(BF16)
    t_hi = t.astype(BF16)
    t_lo = (t - t_hi.astype(F32)).astype(BF16)
    at = _dot(a_hi, t_hi) + _dot(a_lo, t_hi) + _dot(a_hi, t_lo)
    resid = (eye - t) - at
    return t + _dot(t_hi, resid.astype(BF16))


def _gdn_chain_step(qc, kc, vc, kk, qk, cols, rows, s_ref, ci, direction):
    c = GDN_CHUNK
    gcol = cols[:, 2 * direction:2 * direction + 1]
    bcol = cols[:, 2 * direction + 1:2 * direction + 2]
    grow = rows[direction:direction + 1, :]
    ri = lax.broadcasted_iota(jnp.int32, (c, c), 0)
    cj = lax.broadcasted_iota(jnp.int32, (c, c), 1)
    if direction == 0:
        incl, strict = ri >= cj, ri > cj
        glast = gcol[c - 1:c, :]
    else:
        incl, strict = ri <= cj, ri < cj
        glast = gcol[0:1, :]
    decay = jnp.where(incl, jnp.exp(jnp.where(incl, gcol - grow, 0.0)), 0.0)
    a_strict = jnp.where(strict, kk * bcol * decay, 0.0)
    tmat = _tri_inverse(a_strict).astype(BF16)
    eg = jnp.exp(gcol)
    rhs = jnp.concatenate([vc * bcol, kc * (bcol * eg)], axis=1).astype(BF16)
    uw = _dot(tmat, rhs)
    u, w = uw[:, :GDN_HEAD_DIM], uw[:, GDN_HEAD_DIM:]
    state = s_ref[ci]
    sb = state.astype(BF16)
    v_new = u - _dot(w.astype(BF16), sb)
    attn = (qk * decay).astype(BF16)
    o = _dot((qc * eg).astype(BF16), sb) + _dot(attn, v_new.astype(BF16))
    kdec = (kc * jnp.exp(glast - gcol)).astype(BF16)
    s_ref[ci] = state * jnp.exp(glast) + _dot_tn(kdec, v_new.astype(BF16))
    return o


def _gdn_chunk_kernel(q_ref, k_ref, v_ref, cols_ref, rows_ref, s0f_ref, s0b_ref,
                      of_ref, ob_ref, sf_ref, sb_ref, s_scr, *, n_chunks, heads_blk):
    c = GDN_CHUNK
    for hh in range(heads_blk):
        s_scr[2 * hh] = s0f_ref[0, hh]
        s_scr[2 * hh + 1] = s0b_ref[0, hh]

    def body(i, carry):
        for hh in range(heads_blk):
            lanes = slice(hh * GDN_HEAD_DIM, (hh + 1) * GDN_HEAD_DIM)
            for direction in range(2):
                ch = i if direction == 0 else n_chunks - 1 - i
                t0 = pl.multiple_of(ch * c, c)
                qc = q_ref[0, pl.ds(t0, c), lanes]
                kc = k_ref[0, pl.ds(t0, c), lanes]
                vc = v_ref[0, pl.ds(t0, c), lanes]
                kb16 = kc.astype(BF16)
                kk = _dot_nt(kb16, kb16)
                qk = _dot_nt(qc.astype(BF16), kb16)
                o = _gdn_chain_step(qc, kc, vc, kk, qk, cols_ref[0, hh, ch], rows_ref[0, hh, ch],
                                    s_scr, 2 * hh + direction, direction)
                if direction == 0:
                    of_ref[0, pl.ds(t0, c), lanes] = o
                else:
                    ob_ref[0, pl.ds(t0, c), lanes] = o
        return carry

    lax.fori_loop(0, n_chunks, body, 0)
    for hh in range(heads_blk):
        sf_ref[0, hh] = s_scr[2 * hh]
        sb_ref[0, hh] = s_scr[2 * hh + 1]


def _gdn_chunked(qkv, cols, rows, s0f, s0b):
    b, n, _ = qkv.shape
    n_chunks = n // GDN_CHUNK
    hb = 2
    nhb = GDN_HEADS // hb
    w = hb * GDN_HEAD_DIM
    st_spec = pl.BlockSpec((1, hb, GDN_HEAD_DIM, GDN_HEAD_DIM), lambda bi, h: (bi, h, 0, 0))
    o_spec = pl.BlockSpec((1, n, w), lambda bi, h: (bi, 0, h))
    return pl.pallas_call(
        functools.partial(_gdn_chunk_kernel, n_chunks=n_chunks, heads_blk=hb),
        grid=(b, nhb),
        in_specs=[pl.BlockSpec((1, n, w), lambda bi, h: (bi, 0, h)),
                  pl.BlockSpec((1, n, w), lambda bi, h: (bi, 0, nhb + h)),
                  pl.BlockSpec((1, n, w), lambda bi, h: (bi, 0, 2 * nhb + h)),
                  pl.BlockSpec((1, hb, n_chunks, GDN_CHUNK, 8), lambda bi, h: (bi, h, 0, 0, 0)),
                  pl.BlockSpec((1, hb, n_chunks, 8, GDN_CHUNK), lambda bi, h: (bi, h, 0, 0, 0)),
                  st_spec, st_spec],
        out_specs=[o_spec, o_spec, st_spec, st_spec],
        out_shape=[jax.ShapeDtypeStruct((b, n, GDN_KEY), F32)] * 2
        + [jax.ShapeDtypeStruct((b, GDN_HEADS, GDN_HEAD_DIM, GDN_HEAD_DIM), F32)] * 2,
        scratch_shapes=[pltpu.VMEM((2 * hb, GDN_HEAD_DIM, GDN_HEAD_DIM), F32)],
        compiler_params=_cparams(2, VMEM_LIMIT),
        name="gdn_chunked",
    )(qkv, qkv, qkv, cols, rows, s0f, s0b)


def _gdn_gate_tables(gf, gb, beta, b, n):
    nc = n // GDN_CHUNK

    def split(x):
        return x.reshape(2, GDN_HEADS, b, nc, GDN_CHUNK)

    gf5, gb5, be5 = split(gf), split(gb), split(beta)
    zero = jnp.zeros_like(gf5[0])
    col_list = [gf5[0], be5[0], gb5[1], be5[1], zero, zero, zero, zero]
    cols = jnp.stack(col_list, axis=-1).transpose(1, 0, 2, 3, 4)
    row_list = [gf5[0], gb5[1], zero, zero, zero, zero, zero, zero]
    rows = jnp.stack(row_list, axis=-2).transpose(1, 0, 2, 3, 4)
    return cols, rows


def _gdn_out_kernel(of_ref, ob_ref, z_ref, g_ref, y_ref):
    o = of_ref[...] + ob_ref[...]
    z = z_ref[...]
    gate = z * jax.nn.sigmoid(z)
    outs = []
    for h in range(GDN_HEADS):
        sl = slice(h * GDN_HEAD_DIM, (h + 1) * GDN_HEAD_DIM)
        oh = o[:, sl]
        r = lax.rsqrt(jnp.mean(oh * oh, axis=-1, keepdims=True) + EPS)
        outs.append(oh * r * g_ref[...] * gate[:, sl])
    y_ref[...] = jnp.concatenate(outs, axis=1).astype(BF16)


def _gdn_out(o_f, o_b, proj, norm_g):
    t = o_f.shape[0]
    z_block = GDN_QKV // GDN_KEY
    spec = pl.BlockSpec((TM, GDN_KEY), lambda i: (i, 0))
    return pl.pallas_call(
        _gdn_out_kernel,
        grid=(t // TM,),
        in_specs=[spec, spec,
                  pl.BlockSpec((TM, GDN_KEY), lambda i: (i, z_block)),
                  pl.BlockSpec((1, GDN_HEAD_DIM), lambda i: (0, 0))],
        out_specs=spec,
        out_shape=jax.ShapeDtypeStruct((t, GDN_KEY), BF16),
        compiler_params=_cparams(1, VMEM_LIMIT),
        name="gdn_out",
    )(o_f, o_b, proj, norm_g.reshape(1, -1))


def _router_kernel(x_ref, g_ref, mod_ref, wr_ref, br_ref, h_ref, idx_ref, w_ref):
    h = _modulate(x_ref[...], g_ref[...], mod_ref, 3, 4)
    h_ref[...] = h
    logits = lax.dot_general(wr_ref[...], h, (((1,), (1,)), ((), ())),
                             preferred_element_type=F32, precision=HIGHEST) + br_ref[...]
    ne, tm = logits.shape
    iota = lax.broadcasted_iota(jnp.int32, (ne, tm), 0)
    vals, idxs = [], []
    cur = logits
    for _ in range(TOP_K):
        m = jnp.max(cur, axis=0, keepdims=True)
        idx = jnp.min(jnp.where(cur == m, iota, ne), axis=0, keepdims=True)
        vals.append(m)
        idxs.append(idx)
        cur = jnp.where(iota == idx, -jnp.inf, cur)
    es = [jnp.exp(v - vals[0]) for v in vals]
    tot = es[0] + es[1] + es[2] + es[3]
    for kk in range(TOP_K):
        idx_ref[kk:kk + 1, :] = idxs[kk]
        w_ref[kk:kk + 1, :] = es[kk] / tot


def _router(x, g, mod, w_router_t, b_router):
    t, d = x.shape
    return pl.pallas_call(
        _router_kernel,
        grid=(t // TM,),
        in_specs=[pl.BlockSpec((TM, d), lambda i: (i, 0)),
                  pl.BlockSpec((1, d), lambda i: (0, 0)),
                  pl.BlockSpec((1, 6, 1, d), lambda i: (_group_of_tile(i, TM), 0, 0, 0)),
                  pl.BlockSpec((N_EXPERTS, d), lambda i: (0, 0)),
                  pl.BlockSpec((N_EXPERTS, 1), lambda i: (0, 0))],
        out_specs=[pl.BlockSpec((TM, d), lambda i: (i, 0)),
                   pl.BlockSpec((TOP_K, TM), lambda i: (0, i)),
                   pl.BlockSpec((TOP_K, TM), lambda i: (0, i))],
        out_shape=[jax.ShapeDtypeStruct((t, d), F32),
                   jax.ShapeDtypeStruct((TOP_K, t), jnp.int32),
                   jax.ShapeDtypeStruct((TOP_K, t), F32)],
        compiler_params=_cparams(1, VMEM_LIMIT),
        name="moe_router",
    )(x, g, mod, w_router_t, b_router.reshape(N_EXPERTS, 1))


def _row_gather_kernel(src_ref, h_hbm, o_ref, sem):
    rows = o_ref.shape[0]

    def issue(r, carry):
        t = src_ref[0, 0, r]
        pltpu.make_async_copy(h_hbm.at[pl.ds(t, 1)], o_ref.at[pl.ds(r, 1)], sem).start()
        return carry

    lax.fori_loop(0, rows, issue, 0)
    pltpu.make_async_copy(h_hbm.at[pl.ds(0, rows)], o_ref, sem).wait()


def _row_gather(src, h):
    p = src.shape[0]
    d = h.shape[1]
    nb = p // ROW_DMA_BLOCK
    return pl.pallas_call(
        _row_gather_kernel,
        grid=(nb,),
        in_specs=[pl.BlockSpec((1, 1, ROW_DMA_BLOCK), lambda i: (i, 0, 0), memory_space=pltpu.SMEM),
                  pl.BlockSpec(memory_space=pl.ANY)],
        out_specs=pl.BlockSpec((ROW_DMA_BLOCK, d), lambda i: (i, 0)),
        out_shape=jax.ShapeDtypeStruct((p, d), h.dtype),
        scratch_shapes=[pltpu.SemaphoreType.DMA(())],
        compiler_params=_cparams(1),
        name="moe_row_gather",
    )(src.reshape(nb, 1, ROW_DMA_BLOCK), h)


def _row_scatter_kernel(dst_ref, nvalid_ref, y_ref, o_hbm, sem):
    nv = nvalid_ref[0, 0, 0]

    def issue(r, carry):
        d = dst_ref[0, 0, r]
        pltpu.make_async_copy(y_ref.at[pl.ds(r, 1)], o_hbm.at[pl.ds(d, 1)], sem).start()
        return carry

    lax.fori_loop(0, nv, issue, 0)

    def drain(r, carry):
        pltpu.make_async_copy(y_ref.at[pl.ds(0, 1)], o_hbm.at[pl.ds(0, 1)], sem).wait()
        return carry

    lax.fori_loop(0, nv, drain, 0)


def _row_scatter(dst, nvalid, y, n_out):
    p, d = y.shape
    nb = p // ROW_DMA_BLOCK
    return pl.pallas_call(
        _row_scatter_kernel,
        grid=(nb,),
        in_specs=[pl.BlockSpec((1, 1, ROW_DMA_BLOCK), lambda i: (i, 0, 0), memory_space=pltpu.SMEM),
                  pl.BlockSpec((1, 1, 1), lambda i: (i, 0, 0), memory_space=pltpu.SMEM),
                  pl.BlockSpec((ROW_DMA_BLOCK, d), lambda i: (i, 0))],
        out_specs=pl.BlockSpec(memory_space=pl.ANY),
        out_shape=jax.ShapeDtypeStruct((n_out, d), y.dtype),
        scratch_shapes=[pltpu.SemaphoreType.DMA(())],
        compiler_params=_cparams(1),
        name="moe_row_scatter",
    )(dst.reshape(nb, 1, ROW_DMA_BLOCK), nvalid.reshape(nb, 1, 1), y)


def _experts_kernel(te_ref, nu_ref, x_ref, wgu_ref, bgu_ref, wd_ref, bd_ref, wrow_ref, y_ref,
                    wgu_bf, wd_bf):
    i = pl.program_id(0)
    e = te_ref[i]
    prev = te_ref[jnp.maximum(i - 1, 0)]

    @pl.when((i == 0) | (e != prev))
    def _():
        wgu_bf[...] = wgu_ref[0].astype(BF16)
        wd_bf[...] = wd_ref[0].astype(BF16)

    @pl.when(i < nu_ref[0])
    def _():
        x = x_ref[...].astype(BF16)
        gu = _dot(x, wgu_bf[...]) + bgu_ref[0]
        glu = jnp.minimum(gu[:, :D_EXPERT], SWIGLU_LIMIT)
        lin = jnp.clip(gu[:, D_EXPERT:], -SWIGLU_LIMIT, SWIGLU_LIMIT)
        act = (lin + 1.0) * glu * jax.nn.sigmoid(SWIGLU_ALPHA * glu)
        y = _dot(act.astype(BF16), wd_bf[...]) + bd_ref[0]
        y_ref[...] = y * wrow_ref[...]

    @pl.when(i >= nu_ref[0])
    def _():
        y_ref[...] = jnp.zeros(y_ref.shape, y_ref.dtype)


def _experts(tile_expert, n_used, xs, w_gate_up, b_gate_up, w_down, b_down, w_rows):
    p, d = xs.shape
    grid_spec = pltpu.PrefetchScalarGridSpec(
        num_scalar_prefetch=2,
        grid=(p // MOE_TM,),
        in_specs=[pl.BlockSpec((MOE_TM, d), lambda i, te, nu: (i, 0)),
                  pl.BlockSpec((1, d, 2 * D_EXPERT), lambda i, te, nu: (te[i], 0, 0)),
                  pl.BlockSpec((1, 1, 2 * D_EXPERT), lambda i, te, nu: (te[i], 0, 0)),
                  pl.BlockSpec((1, D_EXPERT, d), lambda i, te, nu: (te[i], 0, 0)),
                  pl.BlockSpec((1, 1, d), lambda i, te, nu: (te[i], 0, 0)),
                  pl.BlockSpec((MOE_TM, 1), lambda i, te, nu: (i, 0))],
        out_specs=pl.BlockSpec((MOE_TM, d), lambda i, te, nu: (i, 0)),
        scratch_shapes=[pltpu.VMEM((d, 2 * D_EXPERT), BF16), pltpu.VMEM((D_EXPERT, d), BF16)],
    )
    return pl.pallas_call(
        _experts_kernel,
        grid_spec=grid_spec,
        out_shape=jax.ShapeDtypeStruct((p, d), F32),
        compiler_params=_cparams(1, VMEM_LIMIT),
        name="moe_experts",
    )(tile_expert, n_used, xs, w_gate_up, b_gate_up.reshape(N_EXPERTS, 1, -1), w_down,
      b_down.reshape(N_EXPERTS, 1, -1), w_rows)


def _moe_sum_kernel(x_ref, gate_ref, y0, y1, y2, y3, o_ref):
    o_ref[...] = x_ref[...] + gate_ref[0, 0] * (y0[...] + y1[...] + y2[...] + y3[...])


def _moe_sum(x, mod, ys):
    t, d = x.shape
    nt = t // TM
    y_specs = [pl.BlockSpec((TM, d), lambda i, k=k: (k * nt + i, 0)) for k in range(TOP_K)]
    return pl.pallas_call(
        _moe_sum_kernel,
        grid=(nt,),
        in_specs=[pl.BlockSpec((TM, d), lambda i: (i, 0)),
                  pl.BlockSpec((1, 1, 1, d), lambda i: (_group_of_tile(i, TM), 5, 0, 0))] + y_specs,
        out_specs=pl.BlockSpec((TM, d), lambda i: (i, 0)),
        out_shape=jax.ShapeDtypeStruct((t, d), F32),
        compiler_params=_cparams(1, VMEM_LIMIT),
        name="moe_sum",
    )(x, mod, ys, ys, ys, ys)


def _moe_plan(idx_t, w_t):
    t = idx_t.shape[1]
    a = TOP_K * t
    e_flat = idx_t.reshape(a)
    e_sorted, order = lax.sort((e_flat, jnp.arange(a, dtype=jnp.int32)), num_keys=1, is_stable=True)
    counts = jnp.sum(e_flat[None, :] == jnp.arange(N_EXPERTS, dtype=jnp.int32)[:, None], axis=1,
                     dtype=jnp.int32)
    starts = jnp.cumsum(counts) - counts
    tiles = (counts + MOE_TM - 1) // MOE_TM
    tile_ends = jnp.cumsum(tiles)
    tile_starts = tile_ends - tiles
    n_used = tile_ends[-1]
    tile_ids = jnp.arange(MOE_TILES, dtype=jnp.int32)
    tile_expert = jnp.minimum(jnp.searchsorted(tile_ends, tile_ids, side="right"),
                              N_EXPERTS - 1).astype(jnp.int32)
    last_expert = tile_expert[jnp.maximum(n_used - 1, 0)]
    tile_expert = jnp.where(tile_ids < n_used, tile_expert, last_expert)
    pos = jnp.arange(MOE_PAD_ROWS, dtype=jnp.int32)
    pe = tile_expert[pos // MOE_TM]
    off = pos - tile_starts[pe] * MOE_TM
    valid = (off < counts[pe]) & (pos // MOE_TM < n_used)
    assign = order[jnp.clip(starts[pe] + off, 0, a - 1)]
    src_tok = jnp.where(valid, assign % t, 0).astype(jnp.int32)
    dst_row = jnp.where(valid, assign, 0).astype(jnp.int32)
    w_rows = jnp.where(valid, w_t.reshape(a)[assign], 0.0).reshape(MOE_PAD_ROWS, 1)
    nvalid = jnp.sum(valid.reshape(-1, ROW_DMA_BLOCK), axis=1, dtype=jnp.int32)
    return tile_expert, n_used.reshape(1).astype(jnp.int32), src_tok, dst_row, w_rows, nvalid


def _moe_layer(x, g_ffn, mod, w_router, b_router, w_gate_up, b_gate_up, w_down, b_down):
    h, idx_t, w_t = _router(x, g_ffn.reshape(1, -1), mod, w_router.T, b_router)
    tile_expert, n_used, src_tok, dst_row, w_rows, nvalid = _moe_plan(idx_t, w_t)
    xs = _row_gather(src_tok, h)
    ys = _experts(tile_expert, n_used, xs, w_gate_up, b_gate_up, w_down, b_down, w_rows)
    y_tok = _row_scatter(dst_row, nvalid, ys, TOP_K * x.shape[0])
    return _moe_sum(x, mod, y_tok)


def _na_layer(x, g_mix, mod, cache_k, cache_v, w_in, q_g, k_g, rpb, w_out):
    hd = NA_HEADS * NA_HEAD_DIM
    gains = jnp.concatenate([jnp.tile(q_g * NA_HEAD_DIM ** -0.5, NA_HEADS), jnp.tile(k_g, NA_HEADS),
                             jnp.ones((hd,), F32)]).reshape(1, -1)
    tn = 512
    qkv = _modmm(x, g_mix.reshape(1, -1), mod, w_in.astype(BF16), shift_idx=0, scale_idx=1, tn=tn,
                 norm=(2 * hd // tn, gains, NA_HEAD_DIM), name="na_qkv")
    qkv_p = qkv[:PROMPT_TOK].reshape(BATCH, SEQ, 3 * hd)
    qkv_s = qkv[PROMPT_TOK:].reshape(DEC_BATCH, DEC_SEQ, 3 * hd)
    n_pairs = NA_HEADS // 2
    o_p = _pair_attn(qkv_p, qkv_p, qkv_p, q_col=0, k_col=n_pairs, v_col=2 * n_pairs,
                     slot=NA_HEAD_DIM, tq=SEQ, n_pairs=n_pairs, name="na_context_attn")
    bias = _na_bias_table(rpb)
    o_s = _na_latent_attn(qkv_s, cache_k.reshape(DEC_BATCH, PAST_LEN, hd),
                          cache_v.reshape(DEC_BATCH, PAST_LEN, hd), bias)
    o = jnp.concatenate([o_p.reshape(PROMPT_TOK, hd), o_s.reshape(SAMPLE_TOK, hd)], axis=0)
    x = _mm_res(o, w_out.astype(BF16), x, mod, 2, name="na_out")
    new_k = qkv_p[:, :, hd:2 * hd].reshape(BATCH, SEQ, NA_HEADS, NA_HEAD_DIM)
    new_v = qkv_p[:, :, 2 * hd:].reshape(BATCH, SEQ, NA_HEADS, NA_HEAD_DIM)
    return x, new_k, new_v


def _mla_weights(w_q_up, w_kv_up, q_g, k_g):
    wq = w_q_up.reshape(MLA_Q_LORA, MLA_HEADS, MLA_QK_DIM)
    wq = jnp.pad(wq, ((0, 0), (0, 0), (0, LANE - MLA_QK_DIM))).reshape(MLA_Q_LORA, MLA_HEADS * LANE)
    kv = w_kv_up.reshape(MLA_KV_LORA, MLA_HEADS, MLA_NOPE_DIM + MLA_V_DIM)
    wk_top = jnp.pad(kv[:, :, :MLA_NOPE_DIM], ((0, 0), (0, 0), (0, LANE - MLA_NOPE_DIM)))
    eye = jnp.eye(MLA_ROPE_DIM, dtype=F32)
    pe_rows = jnp.pad(eye, ((0, 0), (MLA_NOPE_DIM, LANE - MLA_QK_DIM)))
    pe_rows = jnp.broadcast_to(pe_rows[:, None, :], (MLA_ROPE_DIM, MLA_HEADS, LANE))
    k_in = 2 * LANE
    wk = jnp.concatenate([wk_top, pe_rows,
                          jnp.zeros((k_in - MLA_KV_LORA - MLA_ROPE_DIM, MLA_HEADS, LANE), F32)], axis=0)
    wk = wk.reshape(k_in, MLA_HEADS * LANE)
    wv = jnp.pad(kv[:, :, MLA_NOPE_DIM:].reshape(MLA_KV_LORA, MLA_HEADS * MLA_V_DIM),
                 ((0, k_in - MLA_KV_LORA), (0, 0)))
    wkv = jnp.concatenate([wk, wv], axis=1)
    pad_g = LANE - MLA_QK_DIM
    qgain = jnp.tile(jnp.pad(q_g * MLA_QK_DIM ** -0.5, (0, pad_g)), MLA_HEADS).reshape(1, -1)
    kgain = jnp.tile(jnp.pad(k_g, (0, pad_g)), MLA_HEADS)
    kvgain = jnp.concatenate([kgain, jnp.ones((MLA_HEADS * MLA_V_DIM,), F32)]).reshape(1, -1)
    return wq.astype(BF16), wkv.astype(BF16), qgain, kvgain


def _mla_layer(x, g_mix, mod, cache_ckv, cache_kpe, w_in, q_a_g, kv_a_g, w_q_up, w_kv_up, q_g, k_g,
               w_out):
    n_lat = MLA_Q_LORA + MLA_KV_LORA + MLA_ROPE_DIM
    w_in_p = jnp.pad(w_in, ((0, 0), (0, 512 - n_lat))).astype(BF16)
    proj = _modmm(x, g_mix.reshape(1, -1), mod, w_in_p, shift_idx=0, scale_idx=1, tn=512,
                  name="mla_in")
    cq, ckv, kin = _mla_latents(proj, q_a_g, kv_a_g)
    wq, wkv, qgain, kvgain = _mla_weights(w_q_up, w_kv_up, q_g, k_g)
    n_ktiles = MLA_HEADS * LANE // 256
    npt = PROMPT_TOK // TM
    spb = DEC_SEQ // TM
    q_tab = _rope_tables(DEC_SEQ, n_identity_rows_first=TM)
    q = _mmn(cq, wq, qgain, n_norm_tiles=n_ktiles, count=MLA_QK_DIM,
             rope=(q_tab, lambda i: jnp.where(i < npt, 0, 1 + (i - npt) % spb)), name="mla_q")
    kv_p = _mmn(kin[:PROMPT_TOK], wkv, kvgain, n_norm_tiles=n_ktiles, count=MLA_QK_DIM,
                name="mla_kv_prompt")
    kin_ctx = jnp.concatenate(
        [cache_ckv, cache_kpe, jnp.zeros((DEC_BATCH, PAST_LEN, 2 * LANE - MLA_KV_LORA - MLA_ROPE_DIM),
                                         F32)], axis=-1).astype(BF16)
    kin_s = jnp.concatenate([kin[PROMPT_TOK:].reshape(DEC_BATCH, DEC_SEQ, 2 * LANE), kin_ctx], axis=1)
    n_keys = DEC_SEQ + PAST_LEN
    kpb = n_keys // TM
    k_tab = _rope_tables(DEC_SEQ, n_identity_rows_last=PAST_LEN)
    kv_s = _mmn(kin_s.reshape(DEC_BATCH * n_keys, 2 * LANE), wkv, kvgain, n_norm_tiles=n_ktiles,
                count=MLA_QK_DIM, rope=(k_tab, lambda i: i % kpb), name="mla_kv_sample")
    hw = MLA_HEADS * LANE
    n_pairs = MLA_HEADS // 2
    q_p = q[:PROMPT_TOK].reshape(BATCH, SEQ, hw)
    q_s = q[PROMPT_TOK:].reshape(DEC_BATCH, DEC_SEQ, hw)
    kv_p3 = kv_p.reshape(BATCH, SEQ, -1)
    kv_s3 = kv_s.reshape(DEC_BATCH, n_keys, -1)
    o_p = _pair_attn(q_p, kv_p3, kv_p3, q_col=0, k_col=0, v_col=hw // LANE, slot=LANE, tq=SEQ,
                     n_pairs=n_pairs, name="mla_context_attn")
    o_s = _pair_attn(q_s, kv_s3, kv_s3, q_col=0, k_col=0, v_col=hw // LANE, slot=LANE, tq=256,
                     n_pairs=n_pairs, name="mla_latent_attn")
    ov = MLA_HEADS * MLA_V_DIM
    o = jnp.concatenate([o_p.reshape(PROMPT_TOK, ov), o_s.reshape(SAMPLE_TOK, ov)], axis=0)
    x = _mm_res(o, w_out.astype(BF16), x, mod, 2, name="mla_out")
    new_ckv = ckv[:PROMPT_TOK].reshape(BATCH, SEQ, MLA_KV_LORA)
    new_kpe = proj[:PROMPT_TOK, MLA_Q_LORA + MLA_KV_LORA:n_lat].reshape(BATCH, SEQ, MLA_ROPE_DIM)
    return x, new_ckv, new_kpe


def _gdn_layer(x, g_mix, mod, s_fwd, s_bwd, w_in, conv_w, a_log, dt_bias, norm_g, w_out):
    n_main = GDN_QKV + GDN_KEY
    gm = g_mix.reshape(1, -1)
    proj = _modmm(x, gm, mod, w_in[:, :n_main].astype(BF16), shift_idx=0, scale_idx=1, tn=512,
                  name="gdn_in")
    w_ab = jnp.pad(w_in[:, n_main:], ((0, 0), (0, LANE - 4 * GDN_HEADS))).astype(BF16)
    ab = _modmm(x, gm, mod, w_ab, shift_idx=0, scale_idx=1, tn=LANE, name="gdn_in_ab")
    gf, gb, beta = _gdn_gates(ab[:, :4 * GDN_HEADS].T, a_log, dt_bias)
    outs = []
    states = []
    for lo, hi, b, n, s0f, s0b in (
            (0, PROMPT_TOK, BATCH, SEQ, None, None),
            (PROMPT_TOK, TOKENS, DEC_BATCH, DEC_SEQ, s_fwd, s_bwd)):
        if s0f is None:
            s0f = jnp.zeros((b, GDN_HEADS, GDN_HEAD_DIM, GDN_HEAD_DIM), F32)
            s0b = s0f
        qkv = _gdn_conv(proj[lo:hi].reshape(b, n, n_main), conv_w)
        cols, rows = _gdn_gate_tables(gf[:, lo:hi], gb[:, lo:hi], beta[:, lo:hi], b, n)
        o_f, o_b, sf, sb = _gdn_chunked(qkv, cols, rows, s0f, s0b)
        outs.append((o_f.reshape(b * n, GDN_KEY), o_b.reshape(b * n, GDN_KEY)))
        states.append((sf, sb))
    o_f = jnp.concatenate([outs[0][0], outs[1][0]], axis=0)
    o_b = jnp.concatenate([outs[0][1], outs[1][1]], axis=0)
    y = _gdn_out(o_f, o_b, proj, norm_g)
    x = _mm_res(y, w_out.astype(BF16), x, mod, 2, name="gdn_out_proj")
    return x, states[0][0], states[0][1]


def kernel(x_prompt, x_sample, cache_na_k, cache_na_v, cache_mla_ckv, cache_mla_kpe, state_gdn_fwd,
           state_gdn_bwd, c, c_ctx, ada_w, ada_b, norm_mix_g, norm_ffn_g, na_w_in, na_q_g, na_k_g,
           na_rpb, na_w_out, mla_w_in, mla_q_a_g, mla_kv_a_g, mla_w_q_up, mla_w_kv_up, mla_q_g,
           mla_k_g, mla_w_out, gdn_w_in, gdn_conv_w, gdn_a_log, gdn_dt_bias, gdn_norm_g, gdn_w_out,
           moe_w_router, moe_b_router, moe_w_gate_up, moe_b_gate_up, moe_w_down, moe_b_down):
    x = jnp.concatenate([x_prompt.reshape(PROMPT_TOK, D_MODEL), x_sample.reshape(SAMPLE_TOK, D_MODEL)],
                        axis=0)
    c_all = jnp.concatenate([c_ctx[None], c, jnp.zeros((16 - N_GROUPS, D_MODEL), F32)], axis=0)
    mods = _ada_mods(c_all, ada_w, ada_b)[:, :N_GROUPS].reshape(DEPTH, N_GROUPS, 6, 1, D_MODEL)
    new_na_k, new_na_v, new_ckv, new_kpe, new_sf, new_sb = [], [], [], [], [], []
    for i in range(DEPTH):
        kind, j = i % N_MIXERS, i // N_MIXERS
        mod = mods[i]
        if kind == 0:
            x, nk, nv = _na_layer(x, norm_mix_g[i], mod, cache_na_k[:, j], cache_na_v[:, j],
                                  na_w_in[j], na_q_g[j], na_k_g[j], na_rpb[j], na_w_out[j])
            new_na_k.append(nk)
            new_na_v.append(nv)
        elif kind == 1:
            x, ckv, kpe = _mla_layer(x, norm_mix_g[i], mod, cache_mla_ckv[:, j], cache_mla_kpe[:, j],
                                     mla_w_in[j], mla_q_a_g[j], mla_kv_a_g[j], mla_w_q_up[j],
                                     mla_w_kv_up[j], mla_q_g[j], mla_k_g[j], mla_w_out[j])
            new_ckv.append(ckv)
            new_kpe.append(kpe)
        else:
            x, sf, sb = _gdn_layer(x, norm_mix_g[i], mod, state_gdn_fwd[:, j], state_gdn_bwd[:, j],
                                   gdn_w_in[j], gdn_conv_w[j], gdn_a_log[j], gdn_dt_bias[j],
                                   gdn_norm_g[j], gdn_w_out[j])
            new_sf.append(sf)
            new_sb.append(sb)
        x = _moe_layer(x, norm_ffn_g[i], mod, moe_w_router[i], moe_b_router[i], moe_w_gate_up[i],
                       moe_b_gate_up[i], moe_w_down[i], moe_b_down[i])
    return (x[:PROMPT_TOK].reshape(BATCH, SEQ, D_MODEL),
            x[PROMPT_TOK:].reshape(DEC_BATCH, DEC_SEQ, D_MODEL),
            jnp.stack(new_na_k, axis=1), jnp.stack(new_na_v, axis=1),
            jnp.stack(new_ckv, axis=1), jnp.stack(new_kpe, axis=1),
            jnp.stack(new_sf, axis=1), jnp.stack(new_sb, axis=1))
```

```python
import functools
import math

import numpy as np
import jax
import jax.numpy as jnp
from jax import lax
from jax.experimental import pallas as pl
from jax.experimental.pallas import tpu as pltpu

F32 = jnp.float32
BF16 = jnp.bfloat16
HIGHEST = lax.Precision.HIGHEST

D_MODEL = 1024
BATCH = 32
SEQ = 256
DEPTH = 4
DEC_BATCH = 8
DEC_SEQ = 2048
PAST_LEN = 512
GRID_W = 64
N_MIXERS = 3
NA_HEADS = 16
NA_HEAD_DIM = 64
NA_ROWS = 8
NA_COLS = 16
MLA_HEADS = 16
MLA_NOPE_DIM = 64
MLA_ROPE_DIM = 32
MLA_QK_DIM = MLA_NOPE_DIM + MLA_ROPE_DIM
MLA_V_DIM = 64
MLA_Q_LORA = 256
MLA_KV_LORA = 128
GDN_HEADS = 8
GDN_HEAD_DIM = 128
GDN_KEY = GDN_HEADS * GDN_HEAD_DIM
GDN_QKV = 3 * GDN_KEY
GDN_CONV_W = 5
GDN_CHUNK = 64
N_EXPERTS = 32
TOP_K = 4
D_EXPERT = D_MODEL
SWIGLU_ALPHA = 1.702
SWIGLU_LIMIT = 7.0
ROPE_BASE = 10000.0
EPS = 1e-6

PROMPT_TOK = BATCH * SEQ
SAMPLE_TOK = DEC_BATCH * DEC_SEQ
TOKENS = PROMPT_TOK + SAMPLE_TOK
N_GROUPS = 1 + DEC_BATCH

LANE = 128
TM = 512
MOE_TM = 512
MOE_ROWS = TOKENS * TOP_K
MOE_ITEMS = MOE_ROWS // MOE_TM + N_EXPERTS
TOKEN_TILE_ROWS = D_MODEL // LANE
ROW_DMA_BLOCK = 256
ROW_DMA_UNROLL = 8
VMEM_LIMIT = 56 * 1024 * 1024
NEG_BIG = -1e30


def _cparams(n_axes, vmem=None):
    return pltpu.CompilerParams(dimension_semantics=("arbitrary",) * n_axes,
                                vmem_limit_bytes=vmem)


def _dot(a, b):
    return jnp.dot(a, b, preferred_element_type=F32)


def _dot_nt(a, b):
    return lax.dot_general(a, b, (((1,), (1,)), ((), ())), preferred_element_type=F32)


def _dot_tn(a, b):
    return lax.dot_general(a, b, (((0,), (0,)), ((), ())), preferred_element_type=F32)


def _dot_hi(a, b):
    return jnp.dot(a, b, preferred_element_type=F32, precision=HIGHEST)


def _group_of_tile(i, tm):
    npt = PROMPT_TOK // tm
    spb = DEC_SEQ // tm
    return jnp.where(i < npt, 0, 1 + (i - npt) // spb)


def _block_diag_ones(n, group):
    idx = np.arange(n) // group
    return jnp.asarray((idx[:, None] == idx[None, :]).astype(np.float32), BF16)


def _ada_kernel(c_ref, w_ref, b_ref, o_ref):
    c = c_ref[...]
    s = c * jax.nn.sigmoid(c)
    o_ref[0] = _dot_hi(s, w_ref[0]) + b_ref[0]


def _ada_mods(c_all, ada_w, ada_b):
    tn = 512
    n = 6 * D_MODEL
    return pl.pallas_call(
        _ada_kernel,
        grid=(DEPTH, n // tn),
        in_specs=[pl.BlockSpec((16, D_MODEL), lambda l, j: (0, 0)),
                  pl.BlockSpec((1, D_MODEL, tn), lambda l, j: (l, 0, j)),
                  pl.BlockSpec((1, 1, tn), lambda l, j: (l, 0, j))],
        out_specs=pl.BlockSpec((1, 16, tn), lambda l, j: (l, 0, j)),
        out_shape=jax.ShapeDtypeStruct((DEPTH, 16, n), F32),
        compiler_params=_cparams(2),
        name="ada_mods",
    )(c_all, ada_w, ada_b.reshape(DEPTH, 1, n))


def _modulate(x, g, mod_ref, shift_idx, scale_idx):
    ms = jnp.mean(x * x, axis=-1, keepdims=True)
    y = x * lax.rsqrt(ms + EPS) * g
    return y * (1.0 + mod_ref[0, scale_idx]) + mod_ref[0, shift_idx]


def _group_norm(acc, bd, gain, inv_count):
    ss = _dot((acc * acc).astype(BF16), bd)
    return acc * lax.rsqrt(ss * inv_count + EPS) * gain


def _modmm_kernel(*refs, shift_idx, scale_idx, n_norm_tiles, inv_count):
    if n_norm_tiles:
        x_ref, g_ref, mod_ref, w_ref, gain_ref, bd_ref, o_ref, h_ref = refs
    else:
        x_ref, g_ref, mod_ref, w_ref, o_ref, h_ref = refs
    j = pl.program_id(1)

    @pl.when(j == 0)
    def _():
        h_ref[...] = _modulate(x_ref[...], g_ref[...], mod_ref, shift_idx, scale_idx).astype(BF16)

    acc = _dot(h_ref[...], w_ref[...])
    if n_norm_tiles:
        @pl.when(j < n_norm_tiles)
        def _():
            o_ref[...] = _group_norm(acc, bd_ref[...], gain_ref[...], inv_count).astype(o_ref.dtype)

        @pl.when(j >= n_norm_tiles)
        def _():
            o_ref[...] = acc.astype(o_ref.dtype)
    else:
        o_ref[...] = acc.astype(o_ref.dtype)


def _modmm(x, g, mod, w, *, shift_idx, scale_idx, tn, out_dtype=F32, norm=None, name="modmm"):
    t, d = x.shape
    n = w.shape[1]
    in_specs = [pl.BlockSpec((TM, d), lambda i, j: (i, 0)),
                pl.BlockSpec((1, d), lambda i, j: (0, 0)),
                pl.BlockSpec((1, 6, 1, d), lambda i, j: (_group_of_tile(i, TM), 0, 0, 0)),
                pl.BlockSpec((d, tn), lambda i, j: (0, j))]
    args = [x, g, mod, w]
    n_norm_tiles, inv_count = 0, 1.0
    if norm is not None:
        n_norm_tiles, gains, group = norm
        inv_count = 1.0 / group
        in_specs += [pl.BlockSpec((1, tn), lambda i, j: (0, j)),
                     pl.BlockSpec((tn, tn), lambda i, j: (0, 0))]
        args += [gains, _block_diag_ones(tn, group)]
    return pl.pallas_call(
        functools.partial(_modmm_kernel, shift_idx=shift_idx, scale_idx=scale_idx,
                          n_norm_tiles=n_norm_tiles, inv_count=inv_count),
        grid=(t // TM, n // tn),
        in_specs=in_specs,
        out_specs=pl.BlockSpec((TM, tn), lambda i, j: (i, j)),
        out_shape=jax.ShapeDtypeStruct((t, n), out_dtype),
        scratch_shapes=[pltpu.VMEM((TM, d), BF16)],
        compiler_params=_cparams(2, VMEM_LIMIT),
        name=name,
    )(*args)


def _mm_res_kernel(a_ref, w_ref, res_ref, gate_ref, o_ref):
    o_ref[...] = res_ref[...] + gate_ref[0, 0] * _dot(a_ref[...], w_ref[...])


def _mm_res(a, w, res, mod, gate_idx, *, tn=512, name="mm_res"):
    t, k = a.shape
    n = w.shape[1]
    return pl.pallas_call(
        _mm_res_kernel,
        grid=(t // TM, n // tn),
        in_specs=[pl.BlockSpec((TM, k), lambda i, j: (i, 0)),
                  pl.BlockSpec((k, tn), lambda i, j: (0, j)),
                  pl.BlockSpec((TM, tn), lambda i, j: (i, j)),
                  pl.BlockSpec((1, 1, 1, tn), lambda i, j: (_group_of_tile(i, TM), gate_idx, 0, j))],
        out_specs=pl.BlockSpec((TM, tn), lambda i, j: (i, j)),
        out_shape=jax.ShapeDtypeStruct((t, n), F32),
        compiler_params=_cparams(2, VMEM_LIMIT),
        name=name,
    )(a, w, res, mod)


def _rope_slots(y, c, s1, s2):
    outs = []
    for s in range(y.shape[1] // LANE):
        ys = y[:, s * LANE:(s + 1) * LANE]
        outs.append(ys * c + pltpu.roll(ys, LANE - MLA_ROPE_DIM // 2, 1) * s1
                    + pltpu.roll(ys, MLA_ROPE_DIM // 2, 1) * s2)
    return jnp.concatenate(outs, axis=1)


def _mmn_kernel(*refs, n_norm_tiles, inv_count, rope):
    if rope:
        a_ref, w_ref, gain_ref, bd_ref, c_ref, s1_ref, s2_ref, o_ref = refs
    else:
        a_ref, w_ref, gain_ref, bd_ref, o_ref = refs
    j = pl.program_id(1)
    acc = _dot(a_ref[...], w_ref[...])

    @pl.when(j < n_norm_tiles)
    def _():
        y = _group_norm(acc, bd_ref[...], gain_ref[...], inv_count)
        if rope:
            y = _rope_slots(y, c_ref[0], s1_ref[0], s2_ref[0])
        o_ref[...] = y.astype(o_ref.dtype)

    @pl.when(j >= n_norm_tiles)
    def _():
        o_ref[...] = acc.astype(o_ref.dtype)


def _mmn(a, w, gains, *, n_norm_tiles, count, rope=None, tn=256, name="mmn"):
    t, k = a.shape
    n = w.shape[1]
    in_specs = [pl.BlockSpec((TM, k), lambda i, j: (i, 0)),
                pl.BlockSpec((k, tn), lambda i, j: (0, j)),
                pl.BlockSpec((1, tn), lambda i, j: (0, j)),
                pl.BlockSpec((tn, tn), lambda i, j: (0, 0))]
    args = [a, w, gains, _block_diag_ones(tn, LANE)]
    if rope is not None:
        tables, pos_fn = rope
        for q in range(3):
            in_specs.append(pl.BlockSpec((1, TM, LANE), lambda i, j, q=q: (q, pos_fn(i), 0)))
            args.append(tables)
    return pl.pallas_call(
        functools.partial(_mmn_kernel, n_norm_tiles=n_norm_tiles, inv_count=1.0 / count,
                          rope=rope is not None),
        grid=(t // TM, n // tn),
        in_specs=in_specs,
        out_specs=pl.BlockSpec((TM, tn), lambda i, j: (i, j)),
        out_shape=jax.ShapeDtypeStruct((t, n), BF16),
        compiler_params=_cparams(2, VMEM_LIMIT),
        name=name,
    )(*args)


def _lane_half_mask(h, width=LANE):
    lane = lax.broadcasted_iota(jnp.int32, (1, width), 1)
    return (lane // (width // 2)) == h


def _pair_attn_kernel(q_ref, k_ref, v_ref, o_ref, *, slot):
    q = q_ref[0]
    k = k_ref[0]
    v = v_ref[0].astype(BF16)
    out = jnp.zeros(o_ref.shape[1:], F32)
    for h in range(2):
        mask = _lane_half_mask(h)
        if slot == LANE:
            qh = q[:, h * LANE:(h + 1) * LANE].astype(BF16)
            kh = k[:, h * LANE:(h + 1) * LANE].astype(BF16)
        else:
            qh = jnp.where(mask, q, 0).astype(BF16)
            kh = k.astype(BF16)
        s = _dot_nt(qh, kh)
        m = jnp.max(s, axis=-1, keepdims=True)
        p = jnp.exp(s - m)
        l = jnp.sum(p, axis=-1, keepdims=True)
        vh = jnp.where(mask, v, jnp.zeros_like(v))
        out = out + _dot(p.astype(BF16), vh) * (1.0 / l)
    o_ref[0] = out.astype(o_ref.dtype)


def _pair_attn(q_arr, k_arr, v_arr, *, b, q_batch0=0, kv_batch0=0, q_col, k_col, v_col, slot, tq,
               n_pairs, name):
    nq = q_arr.shape[1]
    nk = k_arr.shape[1]
    qw = 2 * slot
    return pl.pallas_call(
        functools.partial(_pair_attn_kernel, slot=slot),
        grid=(b, n_pairs, nq // tq),
        in_specs=[pl.BlockSpec((1, tq, qw), lambda bi, hp, qi: (q_batch0 + bi, qi, q_col + hp)),
                  pl.BlockSpec((1, nk, qw), lambda bi, hp, qi: (kv_batch0 + bi, 0, k_col + hp)),
                  pl.BlockSpec((1, nk, LANE), lambda bi, hp, qi: (kv_batch0 + bi, 0, v_col + hp))],
        out_specs=pl.BlockSpec((1, tq, LANE), lambda bi, hp, qi: (bi, qi, hp)),
        out_shape=jax.ShapeDtypeStruct((b, nq, n_pairs * LANE), BF16),
        compiler_params=_cparams(3, VMEM_LIMIT),
        name=name,
    )(q_arr, k_arr, v_arr)


NA_QBLOCK_ROWS = 8
NA_WIN = NA_ROWS * GRID_W


def _na_lat_kernel(q_ref, k_ref, v_ref, kc_ref, vc_ref, bias_ref, o_ref, kbf, vbf, *, rows):
    qi = pl.program_id(2)

    @pl.when(qi == 0)
    def _():
        kbf[...] = k_ref[0].astype(BF16)
        v = v_ref[0]
        for h in range(2):
            vbf[h] = jnp.where(_lane_half_mask(h), v, 0.0).astype(BF16)

    q = q_ref[0]
    kc = kc_ref[0].astype(BF16)
    vc = vc_ref[0]
    out = [jnp.zeros((GRID_W, LANE), F32) for _ in range(NA_QBLOCK_ROWS)]
    for h in range(2):
        mask = _lane_half_mask(h)
        qh = jnp.where(mask, q, 0.0).astype(BF16)
        vch = jnp.where(mask, vc, 0.0).astype(BF16)
        s_c = _dot_nt(qh, kc)
        m_c = jnp.max(s_c, axis=-1, keepdims=True)
        p_c = jnp.exp(s_c - m_c)
        l_c = jnp.sum(p_c, axis=-1, keepdims=True)
        o_c = _dot(p_c.astype(BF16), vch)
        for rr in range(NA_QBLOCK_ROWS):
            r = qi * NA_QBLOCK_ROWS + rr
            r0 = jnp.clip(r - NA_ROWS // 2, 0, rows - NA_ROWS)
            start = pl.multiple_of(r0 * GRID_W, GRID_W)
            sl = slice(rr * GRID_W, (rr + 1) * GRID_W)
            k_rows = kbf[pl.ds(start, NA_WIN), :]
            v_rows = vbf[h, pl.ds(start, NA_WIN), :]
            s_l = _dot_nt(qh[sl], k_rows) + bias_ref[r - r0, h]
            m_l = jnp.max(s_l, axis=-1, keepdims=True)
            m = jnp.maximum(m_l, m_c[sl])
            p_l = jnp.exp(s_l - m)
            a_c = jnp.exp(m_c[sl] - m)
            l = jnp.sum(p_l, axis=-1, keepdims=True) + a_c * l_c[sl]
            o = _dot(p_l.astype(BF16), v_rows) + a_c * o_c[sl]
            out[rr] = out[rr] + o * (1.0 / l)
    for rr in range(NA_QBLOCK_ROWS):
        o_ref[0, rr * GRID_W:(rr + 1) * GRID_W, :] = out[rr].astype(o_ref.dtype)


def _na_bias_table(rpb):
    col = np.arange(GRID_W)
    col_start = np.clip(col - NA_COLS // 2, 0, GRID_W - NA_COLS)
    kc = np.arange(GRID_W)
    inside = (kc[None, :] >= col_start[:, None]) & (kc[None, :] < col_start[:, None] + NA_COLS)
    cidx = np.clip(kc[None, :] - col[:, None] + NA_COLS - 1, 0, 2 * NA_COLS - 2)
    p = np.arange(NA_ROWS)
    ridx = np.arange(NA_ROWS)[None, :] - p[:, None] + NA_ROWS - 1
    tab = rpb[:, ridx[:, :, None, None], cidx[None, None, :, :]]
    tab = jnp.where(jnp.asarray(inside)[None, None, None], tab, NEG_BIG)
    tab = tab.transpose(1, 0, 3, 2, 4)
    return tab.reshape(NA_ROWS, NA_HEADS, GRID_W, NA_WIN).astype(F32)


def _na_latent_attn(qkv, kctx, vctx, bias, batch0):
    n = qkv.shape[1]
    b = kctx.shape[0]
    rows = n // GRID_W
    n_pairs = NA_HEADS // 2
    tq = NA_QBLOCK_ROWS * GRID_W
    past = kctx.shape[1]
    return pl.pallas_call(
        functools.partial(_na_lat_kernel, rows=rows),
        grid=(b, n_pairs, n // tq),
        in_specs=[pl.BlockSpec((1, tq, LANE), lambda bi, hp, qi: (batch0 + bi, qi, hp)),
                  pl.BlockSpec((1, n, LANE), lambda bi, hp, qi: (batch0 + bi, 0, n_pairs + hp)),
                  pl.BlockSpec((1, n, LANE), lambda bi, hp, qi: (batch0 + bi, 0, 2 * n_pairs + hp)),
                  pl.BlockSpec((1, past, LANE), lambda bi, hp, qi: (bi, 0, hp)),
                  pl.BlockSpec((1, past, LANE), lambda bi, hp, qi: (bi, 0, hp)),
                  pl.BlockSpec((NA_ROWS, 2, GRID_W, NA_WIN), lambda bi, hp, qi: (0, hp, 0, 0))],
        out_specs=pl.BlockSpec((1, tq, LANE), lambda bi, hp, qi: (bi, qi, hp)),
        out_shape=jax.ShapeDtypeStruct((b, n, n_pairs * LANE), BF16),
        scratch_shapes=[pltpu.VMEM((n, LANE), BF16), pltpu.VMEM((2, n, LANE), BF16)],
        compiler_params=_cparams(3, VMEM_LIMIT),
        name="na_latent_attn",
    )(qkv, qkv, qkv, kctx, vctx, bias)


def _mla_lat_kernel(p_ref, qg_ref, kvg_ref, cq_ref, ckv_ref, kin_ref):
    p = p_ref[...]
    cq = p[:, :MLA_Q_LORA]
    cq = cq * lax.rsqrt(jnp.mean(cq * cq, axis=-1, keepdims=True) + EPS) * qg_ref[...]
    cq_ref[...] = cq.astype(BF16)
    ckv = p[:, MLA_Q_LORA:MLA_Q_LORA + MLA_KV_LORA]
    ckv = ckv * lax.rsqrt(jnp.mean(ckv * ckv, axis=-1, keepdims=True) + EPS) * kvg_ref[...]
    ckv_ref[...] = ckv
    tail = p[:, MLA_Q_LORA + MLA_KV_LORA:]
    kin_ref[...] = jnp.concatenate([ckv, tail], axis=1).astype(BF16)


def _mla_latents(proj, q_a_g, kv_a_g):
    t = proj.shape[0]
    w = proj.shape[1]
    return pl.pallas_call(
        _mla_lat_kernel,
        grid=(t // TM,),
        in_specs=[pl.BlockSpec((TM, w), lambda i: (i, 0)),
                  pl.BlockSpec((1, MLA_Q_LORA), lambda i: (0, 0)),
                  pl.BlockSpec((1, MLA_KV_LORA), lambda i: (0, 0))],
        out_specs=[pl.BlockSpec((TM, MLA_Q_LORA), lambda i: (i, 0)),
                   pl.BlockSpec((TM, MLA_KV_LORA), lambda i: (i, 0)),
                   pl.BlockSpec((TM, 2 * LANE), lambda i: (i, 0))],
        out_shape=[jax.ShapeDtypeStruct((t, MLA_Q_LORA), BF16),
                   jax.ShapeDtypeStruct((t, MLA_KV_LORA), F32),
                   jax.ShapeDtypeStruct((t, 2 * LANE), BF16)],
        compiler_params=_cparams(1),
        name="mla_latents",
    )(proj, q_a_g.reshape(1, -1), kv_a_g.reshape(1, -1))


def _rope_tables(n_pos, n_identity_rows_first=0, n_identity_rows_last=0):
    n_freq = MLA_ROPE_DIM // 4
    inv = ROPE_BASE ** (-jnp.arange(n_freq, dtype=F32) / n_freq)
    t = jnp.arange(n_pos)
    row = (t // GRID_W).astype(F32)
    col = (t % GRID_W).astype(F32)
    ang = jnp.concatenate([row[:, None] * inv, col[:, None] * inv], axis=-1)
    cos, sin = jnp.cos(ang), jnp.sin(ang)
    half = MLA_ROPE_DIM // 2
    ones = jnp.ones((n_pos, MLA_NOPE_DIM), F32)
    zeros = jnp.zeros((n_pos, MLA_NOPE_DIM), F32)
    pad1 = jnp.ones((n_pos, LANE - MLA_QK_DIM), F32)
    pad0 = jnp.zeros((n_pos, LANE - MLA_QK_DIM), F32)
    zh = jnp.zeros((n_pos, half), F32)
    c = jnp.concatenate([ones, cos, cos, pad1], axis=1)
    s1 = jnp.concatenate([zeros, -sin, zh, pad0], axis=1)
    s2 = jnp.concatenate([zeros, zh, sin, pad0], axis=1)
    tab = jnp.stack([c, s1, s2])

    def ident(nrows):
        return jnp.stack([jnp.ones((nrows, LANE), F32), jnp.zeros((nrows, LANE), F32),
                          jnp.zeros((nrows, LANE), F32)])

    parts = []
    if n_identity_rows_first:
        parts.append(ident(n_identity_rows_first))
    parts.append(tab)
    if n_identity_rows_last:
        parts.append(ident(n_identity_rows_last))
    return jnp.concatenate(parts, axis=1)


def _gdn_conv_kernel(x_ref, w_ref, o_ref, pad_ref, *, n):
    j = pl.program_id(1)
    halo = 8
    cw = x_ref.shape[2]
    pad_ref[0:halo, :] = jnp.zeros((halo, cw), F32)
    pad_ref[halo + n:2 * halo + n, :] = jnp.zeros((halo, cw), F32)
    pad_ref[halo:halo + n, :] = x_ref[0]
    acc = jnp.zeros((n, cw), F32)
    for t in range(GDN_CONV_W):
        off = halo - GDN_CONV_W // 2 + t
        acc = acc + pad_ref[off:off + n, :] * w_ref[t:t + 1, :]
    y = acc * jax.nn.sigmoid(acc)
    n_qk_blocks = 2 * GDN_KEY // cw
    n_q_blocks = GDN_KEY // cw
    outs = []
    for s in range(cw // LANE):
        ys = y[:, s * LANE:(s + 1) * LANE]
        ss = jnp.sum(ys * ys, axis=-1, keepdims=True)
        r = jnp.where(j < n_qk_blocks, lax.rsqrt(ss + EPS), 1.0)
        r = r * jnp.where(j < n_q_blocks, GDN_HEAD_DIM ** -0.5, 1.0)
        outs.append(ys * r)
    o_ref[0] = jnp.concatenate(outs, axis=1)


def _gdn_conv(proj3, conv_w, b, batch0):
    n = proj3.shape[1]
    cw = 512
    return pl.pallas_call(
        functools.partial(_gdn_conv_kernel, n=n),
        grid=(b, GDN_QKV // cw),
        in_specs=[pl.BlockSpec((1, n, cw), lambda bi, j: (batch0 + bi, 0, j)),
                  pl.BlockSpec((GDN_CONV_W, cw), lambda bi, j: (0, j))],
        out_specs=pl.BlockSpec((1, n, cw), lambda bi, j: (bi, 0, j)),
        out_shape=jax.ShapeDtypeStruct((b, n, GDN_QKV), F32),
        scratch_shapes=[pltpu.VMEM((n + 16, cw), F32)],
        compiler_params=_cparams(2, VMEM_LIMIT),
        name="gdn_conv",
    )(proj3, conv_w)


def _gdn_gates_kernel(ab_ref, alog_ref, dtb_ref, uf_ref, ub_ref, gf_ref, gb_ref, beta_ref):
    ab = ab_ref[...]
    a = jnp.concatenate([ab[0:8], ab[16:24]], axis=0)
    bb = jnp.concatenate([ab[8:16], ab[24:32]], axis=0)
    z = a + dtb_ref[...]
    sp = jnp.maximum(z, 0.0) + jnp.log(1.0 + jnp.exp(-jnp.abs(z)))
    g = -jnp.exp(alog_ref[...]) * sp
    beta_ref[...] = jax.nn.sigmoid(bb)
    gf_ref[...] = _dot_hi(g, uf_ref[...])
    gb_ref[...] = _dot_hi(g, ub_ref[...])


def _gdn_gates(ab_t, a_log, dt_bias):
    t = ab_t.shape[1]
    tl = 256
    idx = np.arange(tl)
    same = (idx[:, None] // GDN_CHUNK) == (idx[None, :] // GDN_CHUNK)
    uf = jnp.asarray((same & (idx[:, None] <= idx[None, :])).astype(np.float32))
    ub = jnp.asarray((same & (idx[:, None] >= idx[None, :])).astype(np.float32))
    spec = pl.BlockSpec((16, tl), lambda i: (0, i))
    return pl.pallas_call(
        _gdn_gates_kernel,
        grid=(t // tl,),
        in_specs=[pl.BlockSpec((32, tl), lambda i: (0, i)),
                  pl.BlockSpec((16, 1), lambda i: (0, 0)),
                  pl.BlockSpec((16, 1), lambda i: (0, 0)),
                  pl.BlockSpec((tl, tl), lambda i: (0, 0)),
                  pl.BlockSpec((tl, tl), lambda i: (0, 0))],
        out_specs=[spec, spec, spec],
        out_shape=[jax.ShapeDtypeStruct((16, t), F32)] * 3,
        compiler_params=_cparams(1),
        name="gdn_gates",
    )(ab_t, a_log.reshape(16, 1), dt_bias.reshape(16, 1), uf, ub)


def _tri_inverse(a_strict, block):
    n = a_strict.shape[0]
    ri = lax.broadcasted_iota(jnp.int32, (n, n), 0)
    ci = lax.broadcasted_iota(jnp.int32, (n, n), 1)
    eye = (ri == ci).astype(F32)
    x = (-a_strict).astype(BF16)
    t = eye - a_strict
    pw = _dot(x, x)
    n_steps = int(math.log2(block)) - 1
    for step in range(n_steps):
        pwb = pw.astype(BF16)
        if step < n_steps - 1:
            st = _dot(jnp.concatenate([pwb, t.astype(BF16)], axis=0), pwb)
            pw, t = st[:n], t + st[n:]
        else:
            t = t + _dot(t.astype(BF16), pwb)
    a_hi = a_strict.astype(BF16)
    a_lo = (a_strict - a_hi.astype(F32)).astype(BF16)
    t_hi = t.astype(BF16)
    t_lo = (t - t_hi.astype(F32)).astype(BF16)
    at2 = _dot(jnp.concatenate([a_hi, a_lo], axis=0), t_hi)
    at = at2[:n] + at2[n:] + _dot(a_hi, t_lo)
    resid = (eye - t) - at
    return t + _dot(t_hi, resid.astype(BF16))


GDN_SLOT = 5 * GDN_HEAD_DIM


GDN_GROUP = 256


def _gdn_prep_slot(q, k, v, kk, qk, cols, rows, direction):
    c = GDN_CHUNK
    n = GDN_GROUP
    gcol = cols[:, 2 * direction:2 * direction + 1]
    bcol = cols[:, 2 * direction + 1:2 * direction + 2]
    glast = cols[:, 4 + direction:5 + direction]
    grow = rows[direction:direction + 1, :]
    ri = lax.broadcasted_iota(jnp.int32, (n, n), 0)
    cj = lax.broadcasted_iota(jnp.int32, (n, n), 1)
    same = (ri // c) == (cj // c)
    if direction == 0:
        incl, strict = same & (ri >= cj), same & (ri > cj)
    else:
        incl, strict = same & (ri <= cj), same & (ri < cj)
    decay = jnp.where(incl, jnp.exp(jnp.where(incl, gcol - grow, 0.0)), 0.0)
    a_strict = jnp.where(strict, kk * bcol * decay, 0.0)
    tmat = _tri_inverse(a_strict, c).astype(BF16)
    eg = jnp.exp(gcol)
    rhs = jnp.concatenate([v * bcol, k * (bcol * eg)], axis=1).astype(BF16)
    uw = _dot(tmat, rhs)
    attn = qk * decay
    folded = attn[:, :LANE] + attn[:, LANE:]
    folded = folded + pltpu.roll(folded, c, 1)
    egl = jnp.exp(glast)
    egl_hi = egl.astype(BF16).astype(F32)
    lane = lax.broadcasted_iota(jnp.int32, (n, LANE), 1)
    tail = jnp.where(lane < c + c // 2, egl_hi, egl - egl_hi)
    a128 = jnp.where(lane < c, folded, tail)
    return jnp.concatenate([uw, q * eg, k * jnp.exp(glast - gcol), a128], axis=1).astype(BF16)


def _gdn_prep_kernel(q_ref, k_ref, v_ref, cols_ref, rows_ref, pf_ref, pb_ref, *, heads_blk):
    for hh in range(heads_blk):
        lanes = slice(hh * GDN_HEAD_DIM, (hh + 1) * GDN_HEAD_DIM)
        slot_lanes = slice(hh * GDN_SLOT, (hh + 1) * GDN_SLOT)
        q = q_ref[0, :, lanes]
        k = k_ref[0, :, lanes]
        v = v_ref[0, :, lanes]
        kb16 = k.astype(BF16)
        kk = _dot_nt(kb16, kb16)
        qk = _dot_nt(q.astype(BF16), kb16)
        cols = cols_ref[0, hh, 0]
        rows = rows_ref[0, hh, 0]
        pf_ref[0, :, slot_lanes] = _gdn_prep_slot(q, k, v, kk, qk, cols, rows, 0)
        pb_ref[0, :, slot_lanes] = _gdn_prep_slot(q, k, v, kk, qk, cols, rows, 1)


def _gdn_prep(qkv, cols, rows):
    b, n, _ = qkv.shape
    hb = 2
    nhb = GDN_HEADS // hb
    tb = GDN_GROUP
    w = hb * GDN_HEAD_DIM
    p_spec = pl.BlockSpec((1, tb, hb * GDN_SLOT), lambda bi, h, t: (bi, t, h))
    return pl.pallas_call(
        functools.partial(_gdn_prep_kernel, heads_blk=hb),
        grid=(b, nhb, n // tb),
        in_specs=[pl.BlockSpec((1, tb, w), lambda bi, h, t: (bi, t, h)),
                  pl.BlockSpec((1, tb, w), lambda bi, h, t: (bi, t, nhb + h)),
                  pl.BlockSpec((1, tb, w), lambda bi, h, t: (bi, t, 2 * nhb + h)),
                  pl.BlockSpec((1, hb, 1, tb, 8), lambda bi, h, t: (bi, h, t, 0, 0)),
                  pl.BlockSpec((1, hb, 1, 8, tb), lambda bi, h, t: (bi, h, t, 0, 0))],
        out_specs=[p_spec, p_spec],
        out_shape=[jax.ShapeDtypeStruct((b, n, GDN_HEADS * GDN_SLOT), BF16)] * 2,
        compiler_params=_cparams(3, VMEM_LIMIT),
        name="gdn_prep",
    )(qkv, qkv, qkv, cols, rows)


def _gdn_scan_step(p_ref, t0, hh, s_scr, ci):
    c = GDN_CHUNK
    hd = GDN_HEAD_DIM
    base = hh * GDN_SLOT
    rows = pl.ds(t0, c)
    u = p_ref[0, rows, base:base + hd].astype(F32)
    wq = jnp.concatenate([p_ref[0, rows, base + hd:base + 2 * hd],
                          p_ref[0, rows, base + 2 * hd:base + 3 * hd]], axis=0)
    kdec = p_ref[0, rows, base + 3 * hd:base + 4 * hd]
    a128 = p_ref[0, rows, base + 4 * hd:base + 5 * hd]
    state = s_scr[ci]
    ws = _dot(wq, state.astype(BF16))
    v_new = (u - ws[:c]).astype(BF16)
    o = ws[c:] + _dot(a128[:, :c], v_new)
    egl = a128[0:1, c:c + 1].astype(F32) + a128[0:1, c + c // 2:c + c // 2 + 1].astype(F32)
    s_scr[ci] = state * egl + _dot_tn(kdec, v_new)
    return o


def _gdn_scan_kernel(pf_ref, pb_ref, s0f_ref, s0b_ref, of_ref, ob_ref, sf_ref, sb_ref, s_scr,
                     *, heads_blk, chunks_blk):
    c = GDN_CHUNK
    t = pl.program_id(2)

    @pl.when(t == 0)
    def _():
        for hh in range(heads_blk):
            s_scr[2 * hh] = s0f_ref[0, hh]
            s_scr[2 * hh + 1] = s0b_ref[0, hh]

    def body(i, carry):
        tf = pl.multiple_of(i * c, c)
        tbk = pl.multiple_of((chunks_blk - 1 - i) * c, c)
        for hh in range(heads_blk):
            lanes = slice(hh * GDN_HEAD_DIM, (hh + 1) * GDN_HEAD_DIM)
            of_ref[0, pl.ds(tf, c), lanes] = _gdn_scan_step(pf_ref, tf, hh, s_scr, 2 * hh)
            ob_ref[0, pl.ds(tbk, c), lanes] = _gdn_scan_step(pb_ref, tbk, hh, s_scr, 2 * hh + 1)
        return carry

    lax.fori_loop(0, chunks_blk, body, 0)

    @pl.when(t == pl.num_programs(2) - 1)
    def _():
        for hh in range(heads_blk):
            sf_ref[0, hh] = s_scr[2 * hh]
            sb_ref[0, hh] = s_scr[2 * hh + 1]


def _gdn_scan(pf, pb, s0f, s0b):
    b, n, _ = pf.shape
    hs = 4
    nhs = GDN_HEADS // hs
    tb = min(n, 512)
    nt = n // tb
    cb = tb // GDN_CHUNK
    st_spec = pl.BlockSpec((1, hs, GDN_HEAD_DIM, GDN_HEAD_DIM), lambda bi, h, t: (bi, h, 0, 0))
    return pl.pallas_call(
        functools.partial(_gdn_scan_kernel, heads_blk=hs, chunks_blk=cb),
        grid=(b, nhs, nt),
        in_specs=[pl.BlockSpec((1, tb, hs * GDN_SLOT), lambda bi, h, t: (bi, t, h)),
                  pl.BlockSpec((1, tb, hs * GDN_SLOT), lambda bi, h, t: (bi, nt - 1 - t, h)),
                  st_spec, st_spec],
        out_specs=[pl.BlockSpec((1, tb, hs * GDN_HEAD_DIM), lambda bi, h, t: (bi, t, h)),
                   pl.BlockSpec((1, tb, hs * GDN_HEAD_DIM), lambda bi, h, t: (bi, nt - 1 - t, h)),
                   st_spec, st_spec],
        out_shape=[jax.ShapeDtypeStruct((b, n, GDN_KEY), F32)] * 2
        + [jax.ShapeDtypeStruct((b, GDN_HEADS, GDN_HEAD_DIM, GDN_HEAD_DIM), F32)] * 2,
        scratch_shapes=[pltpu.VMEM((2 * hs, GDN_HEAD_DIM, GDN_HEAD_DIM), F32)],
        compiler_params=_cparams(3, VMEM_LIMIT),
        name="gdn_scan",
    )(pf, pb, s0f, s0b)


def _gdn_gate_tables(gf, gb, beta, b, n):
    nc = n // GDN_CHUNK
    ng = n // GDN_GROUP

    def split(x):
        return x.reshape(2, GDN_HEADS, b, nc, GDN_CHUNK)

    def group(x):
        return x.reshape(GDN_HEADS, b, ng, GDN_GROUP)

    gf5, gb5, be5 = split(gf), split(gb), split(beta)
    gcf, gcb = group(gf5[0]), group(gb5[1])
    glf = group(jnp.broadcast_to(gf5[0][..., GDN_CHUNK - 1:], gf5[0].shape))
    glb = group(jnp.broadcast_to(gb5[1][..., :1], gb5[1].shape))
    zero = jnp.zeros_like(gcf)
    col_list = [gcf, group(be5[0]), gcb, group(be5[1]), glf, glb, zero, zero]
    cols = jnp.stack(col_list, axis=-1).transpose(1, 0, 2, 3, 4)
    row_list = [gcf, gcb, zero, zero, zero, zero, zero, zero]
    rows = jnp.stack(row_list, axis=-2).transpose(1, 0, 2, 3, 4)
    return cols, rows


def _gdn_out_kernel(ofp_ref, obp_ref, ofs_ref, obs_ref, z_ref, g_ref, y_ref, o_scr, *, n_prompt_tiles):
    i = pl.program_id(0)

    @pl.when(i < n_prompt_tiles)
    def _():
        o_scr[...] = ofp_ref[...] + obp_ref[...]

    @pl.when(i >= n_prompt_tiles)
    def _():
        o_scr[...] = ofs_ref[...] + obs_ref[...]

    o = o_scr[...]
    z = z_ref[...]
    gate = z * jax.nn.sigmoid(z)
    outs = []
    for h in range(GDN_HEADS):
        sl = slice(h * GDN_HEAD_DIM, (h + 1) * GDN_HEAD_DIM)
        oh = o[:, sl]
        r = lax.rsqrt(jnp.mean(oh * oh, axis=-1, keepdims=True) + EPS)
        outs.append(oh * r * g_ref[...] * gate[:, sl])
    y_ref[...] = jnp.concatenate(outs, axis=1).astype(BF16)


def _gdn_out(o_prompt, o_sample, proj, norm_g):
    t = proj.shape[0]
    npt = o_prompt[0].shape[0] // TM
    z_block = GDN_QKV // GDN_KEY
    p_spec = pl.BlockSpec((TM, GDN_KEY), lambda i: (jnp.minimum(i, npt - 1), 0))
    s_spec = pl.BlockSpec((TM, GDN_KEY), lambda i: (jnp.maximum(i - npt, 0), 0))
    return pl.pallas_call(
        functools.partial(_gdn_out_kernel, n_prompt_tiles=npt),
        grid=(t // TM,),
        in_specs=[p_spec, p_spec, s_spec, s_spec,
                  pl.BlockSpec((TM, GDN_KEY), lambda i: (i, z_block)),
                  pl.BlockSpec((1, GDN_HEAD_DIM), lambda i: (0, 0))],
        out_specs=pl.BlockSpec((TM, GDN_KEY), lambda i: (i, 0)),
        out_shape=jax.ShapeDtypeStruct((t, GDN_KEY), BF16),
        scratch_shapes=[pltpu.VMEM((TM, GDN_KEY), F32)],
        compiler_params=_cparams(1, VMEM_LIMIT),
        name="gdn_out",
    )(*o_prompt, *o_sample, proj, norm_g.reshape(1, -1))


def _store_token_tiles(ref, x):
    for s in range(TOKEN_TILE_ROWS):
        ref[pl.ds(s, x.shape[0], stride=TOKEN_TILE_ROWS), :] = x[:, s * LANE:(s + 1) * LANE]


def _load_token_tiles(ref, rows):
    return [ref[pl.ds(s, rows, stride=TOKEN_TILE_ROWS), :] for s in range(TOKEN_TILE_ROWS)]


def _router_kernel(x_ref, g_ref, mod_ref, wr_ref, br_ref, h_ref, idx_ref, w_ref, cnt_ref):
    h = _modulate(x_ref[...], g_ref[...], mod_ref, 3, 4)
    _store_token_tiles(h_ref, h)
    logits = lax.dot_general(wr_ref[...], h, (((1,), (1,)), ((), ())),
                             preferred_element_type=F32, precision=HIGHEST) + br_ref[...]
    ne, tm = logits.shape
    iota = lax.broadcasted_iota(jnp.int32, (ne, tm), 0)
    vals, idxs = [], []
    cur = logits
    for _ in range(TOP_K):
        m = jnp.max(cur, axis=0, keepdims=True)
        idx = jnp.min(jnp.where(cur == m, iota, ne), axis=0, keepdims=True)
        vals.append(m)
        idxs.append(idx)
        cur = jnp.where(iota == idx, -jnp.inf, cur)
    es = [jnp.exp(v - vals[0]) for v in vals]
    tot = es[0] + es[1] + es[2] + es[3]
    cnt = jnp.zeros((ne, 1), jnp.int32)
    for kk in range(TOP_K):
        idx_ref[kk:kk + 1, :] = idxs[kk]
        w_ref[kk:kk + 1, :] = es[kk] / tot
        cnt = cnt + jnp.sum((iota == idxs[kk]).astype(jnp.int32), axis=1, keepdims=True)

    @pl.when(pl.program_id(0) == 0)
    def _():
        cnt_ref[...] = jnp.zeros(cnt_ref.shape, jnp.int32)

    cnt_ref[...] += cnt


def _router(x, g, mod, w_router_t, b_router):
    t, d = x.shape
    return pl.pallas_call(
        _router_kernel,
        grid=(t // TM,),
        in_specs=[pl.BlockSpec((TM, d), lambda i: (i, 0)),
                  pl.BlockSpec((1, d), lambda i: (0, 0)),
                  pl.BlockSpec((1, 6, 1, d), lambda i: (_group_of_tile(i, TM), 0, 0, 0)),
                  pl.BlockSpec((N_EXPERTS, d), lambda i: (0, 0)),
                  pl.BlockSpec((N_EXPERTS, 1), lambda i: (0, 0))],
        out_specs=[pl.BlockSpec((TM * TOKEN_TILE_ROWS, LANE), lambda i: (i, 0)),
                   pl.BlockSpec((TOP_K, TM), lambda i: (0, i)),
                   pl.BlockSpec((TOP_K, TM), lambda i: (0, i)),
                   pl.BlockSpec((N_EXPERTS, 1), lambda i: (0, 0))],
        out_shape=[jax.ShapeDtypeStruct((t * TOKEN_TILE_ROWS, LANE), F32),
                   jax.ShapeDtypeStruct((TOP_K, t), jnp.int32),
                   jax.ShapeDtypeStruct((TOP_K, t), F32),
                   jax.ShapeDtypeStruct((N_EXPERTS, 1), jnp.int32)],
        compiler_params=_cparams(1, VMEM_LIMIT),
        name="moe_router",
    )(x, g, mod, w_router_t, b_router.reshape(N_EXPERTS, 1))


def _tile_rows(i):
    return pl.ds(pl.multiple_of(i * TOKEN_TILE_ROWS, TOKEN_TILE_ROWS), TOKEN_TILE_ROWS)


def _token_gather_kernel(src_ref, h_hbm, o_ref, sem):
    def issue(r, carry):
        pltpu.make_async_copy(h_hbm.at[_tile_rows(src_ref[0, 0, r])], o_ref.at[_tile_rows(r)], sem).start()
        return carry

    lax.fori_loop(0, ROW_DMA_BLOCK, issue, 0, unroll=ROW_DMA_UNROLL)
    pltpu.make_async_copy(h_hbm.at[pl.ds(0, o_ref.shape[0])], o_ref, sem).wait()


def _token_gather(src, h):
    p = src.shape[0]
    nb = p // ROW_DMA_BLOCK
    blk = ROW_DMA_BLOCK * TOKEN_TILE_ROWS
    return pl.pallas_call(
        _token_gather_kernel,
        grid=(nb,),
        in_specs=[pl.BlockSpec((1, 1, ROW_DMA_BLOCK), lambda i: (i, 0, 0), memory_space=pltpu.SMEM),
                  pl.BlockSpec(memory_space=pl.ANY)],
        out_specs=pl.BlockSpec((blk, LANE), lambda i: (i, 0)),
        out_shape=jax.ShapeDtypeStruct((p * TOKEN_TILE_ROWS, LANE), h.dtype),
        scratch_shapes=[pltpu.SemaphoreType.DMA(())],
        compiler_params=_cparams(1),
        name="moe_token_gather",
    )(src.reshape(nb, 1, ROW_DMA_BLOCK), h)


def _token_scatter_kernel(dst_ref, y_ref, o_hbm, sem):
    def issue(r, carry):
        pltpu.make_async_copy(y_ref.at[_tile_rows(r)], o_hbm.at[_tile_rows(dst_ref[0, 0, r])], sem).start()
        return carry

    lax.fori_loop(0, ROW_DMA_BLOCK, issue, 0, unroll=ROW_DMA_UNROLL)
    pltpu.make_async_copy(y_ref, o_hbm.at[pl.ds(0, y_ref.shape[0])], sem).wait()


def _token_scatter(dst, y, n_out_tokens):
    p = dst.shape[0]
    nb = p // ROW_DMA_BLOCK
    blk = ROW_DMA_BLOCK * TOKEN_TILE_ROWS
    return pl.pallas_call(
        _token_scatter_kernel,
        grid=(nb,),
        in_specs=[pl.BlockSpec((1, 1, ROW_DMA_BLOCK), lambda i: (i, 0, 0), memory_space=pltpu.SMEM),
                  pl.BlockSpec((blk, LANE), lambda i: (i, 0))],
        out_specs=pl.BlockSpec(memory_space=pl.ANY),
        out_shape=jax.ShapeDtypeStruct((n_out_tokens * TOKEN_TILE_ROWS, LANE), y.dtype),
        scratch_shapes=[pltpu.SemaphoreType.DMA(())],
        compiler_params=_cparams(1),
        name="moe_token_scatter",
    )(dst.reshape(nb, 1, ROW_DMA_BLOCK), y)


ITEM_FIRST, ITEM_LAST, ITEM_VALID = 1, 2, 4


def _experts_kernel(tile_ref, exp_ref, lo_ref, hi_ref, flag_ref, x_ref, wgu_ref, bgu_ref, wd_ref,
                    bd_ref, wrow_ref, y_ref, wgu_bf, wd_bf, x_bf, acc):
    j = pl.program_id(0)
    e = exp_ref[j]
    prev = exp_ref[jnp.maximum(j - 1, 0)]
    flags = flag_ref[j]
    first = (flags & ITEM_FIRST) != 0

    @pl.when((j == 0) | (e != prev))
    def _():
        wgu_bf[...] = wgu_ref[0].astype(BF16)
        wd_bf[...] = wd_ref[0].astype(BF16)

    @pl.when(first)
    def _():
        for s, xs in enumerate(_load_token_tiles(x_ref, MOE_TM)):
            x_bf[:, s * LANE:(s + 1) * LANE] = xs.astype(BF16)

    @pl.when((flags & ITEM_VALID) != 0)
    def _():
        gu = _dot(x_bf[...], wgu_bf[...]) + bgu_ref[0]
        glu = jnp.minimum(gu[:, :D_EXPERT], SWIGLU_LIMIT)
        lin = jnp.clip(gu[:, D_EXPERT:], -SWIGLU_LIMIT, SWIGLU_LIMIT)
        act = (lin + 1.0) * glu * jax.nn.sigmoid(SWIGLU_ALPHA * glu)
        y = (_dot(act.astype(BF16), wd_bf[...]) + bd_ref[0]) * wrow_ref[...]
        row = lax.broadcasted_iota(jnp.int32, (MOE_TM, 1), 0)
        mine = (row >= lo_ref[j]) & (row < hi_ref[j])

        @pl.when(first)
        def _():
            acc[...] = jnp.where(mine, y, 0.0)

        @pl.when(jnp.logical_not(first))
        def _():
            acc[...] = jnp.where(mine, y, acc[...])

    @pl.when((flags & ITEM_LAST) != 0)
    def _():
        _store_token_tiles(y_ref, acc[...])


def _experts(plan, xs, w_gate_up, b_gate_up, w_down, b_down, w_sorted):
    d = D_MODEL
    blk = MOE_TM * TOKEN_TILE_ROWS

    def tile_map(j, tile, exp, lo, hi, flag):
        return (tile[j], 0)

    def exp_map(j, tile, exp, lo, hi, flag):
        return (exp[j], 0, 0)

    grid_spec = pltpu.PrefetchScalarGridSpec(
        num_scalar_prefetch=5,
        grid=(MOE_ITEMS,),
        in_specs=[pl.BlockSpec((blk, LANE), tile_map),
                  pl.BlockSpec((1, d, 2 * D_EXPERT), exp_map),
                  pl.BlockSpec((1, 1, 2 * D_EXPERT), exp_map),
                  pl.BlockSpec((1, D_EXPERT, d), exp_map),
                  pl.BlockSpec((1, 1, d), exp_map),
                  pl.BlockSpec((MOE_TM, 1), tile_map)],
        out_specs=pl.BlockSpec((blk, LANE), tile_map),
        scratch_shapes=[pltpu.VMEM((d, 2 * D_EXPERT), BF16), pltpu.VMEM((D_EXPERT, d), BF16),
                        pltpu.VMEM((MOE_TM, d), BF16), pltpu.VMEM((MOE_TM, d), F32)],
    )
    return pl.pallas_call(
        _experts_kernel,
        grid_spec=grid_spec,
        out_shape=jax.ShapeDtypeStruct(xs.shape, F32),
        compiler_params=_cparams(1, VMEM_LIMIT),
        name="moe_experts",
    )(*plan, xs, w_gate_up, b_gate_up.reshape(N_EXPERTS, 1, -1), w_down,
      b_down.reshape(N_EXPERTS, 1, -1), w_sorted)


def _moe_sum_kernel(x_ref, gate_ref, y0, y1, y2, y3, o_ref, ysum):
    ysum[...] = y0[...] + y1[...] + y2[...] + y3[...]
    gate = gate_ref[0, 0]
    for s, ys in enumerate(_load_token_tiles(ysum, TM)):
        cols = slice(s * LANE, (s + 1) * LANE)
        o_ref[:, cols] = x_ref[:, cols] + gate[:, cols] * ys


def _moe_sum(x, mod, y_tok):
    t, d = x.shape
    nt = t // TM
    blk = TM * TOKEN_TILE_ROWS
    y_specs = [pl.BlockSpec((blk, LANE), lambda i, k=k: (k * nt + i, 0)) for k in range(TOP_K)]
    return pl.pallas_call(
        _moe_sum_kernel,
        grid=(nt,),
        in_specs=[pl.BlockSpec((TM, d), lambda i: (i, 0)),
                  pl.BlockSpec((1, 1, 1, d), lambda i: (_group_of_tile(i, TM), 5, 0, 0))] + y_specs,
        out_specs=pl.BlockSpec((TM, d), lambda i: (i, 0)),
        out_shape=jax.ShapeDtypeStruct((t, d), F32),
        scratch_shapes=[pltpu.VMEM((blk, LANE), F32)],
        compiler_params=_cparams(1, VMEM_LIMIT),
        name="moe_sum",
    )(x, mod, y_tok, y_tok, y_tok, y_tok)


def _pick(onehot, vec):
    return jnp.sum(jnp.where(onehot, vec[None, :], 0), axis=1)


def _moe_plan(counts):
    ends = jnp.cumsum(counts)
    starts = ends - counts
    first_tile = starts // MOE_TM
    n_items_e = jnp.where(counts > 0, (ends - 1) // MOE_TM - first_tile + 1, 0)
    item_ends = jnp.cumsum(n_items_e)
    item_starts = item_ends - n_items_e
    n_items = item_ends[-1]
    j = jnp.arange(MOE_ITEMS, dtype=jnp.int32)
    jc = jnp.minimum(j, n_items - 1)
    e_j = jnp.sum(item_ends[None, :] <= jc[:, None], axis=1).astype(jnp.int32)
    onehot = e_j[:, None] == jnp.arange(N_EXPERTS, dtype=jnp.int32)[None, :]
    tile_j = _pick(onehot, first_tile) + (jc - _pick(onehot, item_starts))
    lo = jnp.maximum(_pick(onehot, starts), tile_j * MOE_TM) - tile_j * MOE_TM
    hi = jnp.minimum(_pick(onehot, ends), (tile_j + 1) * MOE_TM) - tile_j * MOE_TM
    valid = j < n_items
    prev_tile = jnp.concatenate([jnp.full((1,), -1, jnp.int32), tile_j[:-1]])
    next_tile = jnp.concatenate([tile_j[1:], jnp.full((1,), -1, jnp.int32)])
    first = valid & (tile_j != prev_tile)
    last = valid & ((tile_j != next_tile) | (j == n_items - 1))
    flags = first * ITEM_FIRST + last * ITEM_LAST + valid * ITEM_VALID
    i32 = lambda v: v.astype(jnp.int32)
    return i32(tile_j), e_j, i32(lo), i32(jnp.where(valid, hi, lo)), i32(flags)


def _moe_layer(x, g_ffn, mod, w_router, b_router, w_gate_up, b_gate_up, w_down, b_down):
    t = x.shape[0]
    a = TOP_K * t
    h, idx_t, w_t, counts = _router(x, g_ffn.reshape(1, -1), mod, w_router.T, b_router)
    _, order, w_sorted = lax.sort((idx_t.reshape(a), jnp.arange(a, dtype=jnp.int32), w_t.reshape(a)),
                                  num_keys=1, is_stable=False)
    plan = _moe_plan(counts.reshape(N_EXPERTS))
    xs = _token_gather(order % t, h)
    ys = _experts(plan, xs, w_gate_up, b_gate_up, w_down, b_down, w_sorted.reshape(a, 1))
    y_tok = _token_scatter(order, ys, a)
    return _moe_sum(x, mod, y_tok)


def _na_layer(x, g_mix, mod, cache_k, cache_v, w_in, q_g, k_g, rpb, w_out):
    hd = NA_HEADS * NA_HEAD_DIM
    gains = jnp.concatenate([jnp.tile(q_g * NA_HEAD_DIM ** -0.5, NA_HEADS), jnp.tile(k_g, NA_HEADS),
                             jnp.ones((hd,), F32)]).reshape(1, -1)
    tn = 512
    qkv = _modmm(x, g_mix.reshape(1, -1), mod, w_in.astype(BF16), shift_idx=0, scale_idx=1, tn=tn,
                 norm=(2 * hd // tn, gains, NA_HEAD_DIM), name="na_qkv")
    qkv_by_seq = qkv.reshape(TOKENS // SEQ, SEQ, 3 * hd)
    qkv_by_dec = qkv.reshape(TOKENS // DEC_SEQ, DEC_SEQ, 3 * hd)
    n_pairs = NA_HEADS // 2
    o_p = _pair_attn(qkv_by_seq, qkv_by_seq, qkv_by_seq, b=BATCH, q_col=0, k_col=n_pairs,
                     v_col=2 * n_pairs, slot=NA_HEAD_DIM, tq=SEQ, n_pairs=n_pairs,
                     name="na_context_attn")
    bias = _na_bias_table(rpb)
    o_s = _na_latent_attn(qkv_by_dec, cache_k.reshape(DEC_BATCH, PAST_LEN, hd),
                          cache_v.reshape(DEC_BATCH, PAST_LEN, hd), bias, PROMPT_TOK // DEC_SEQ)
    o = jnp.concatenate([o_p.reshape(PROMPT_TOK, hd), o_s.reshape(SAMPLE_TOK, hd)], axis=0)
    x = _mm_res(o, w_out.astype(BF16), x, mod, 2, name="na_out")
    new_k = qkv[:PROMPT_TOK, hd:2 * hd].reshape(BATCH, SEQ, NA_HEADS, NA_HEAD_DIM)
    new_v = qkv[:PROMPT_TOK, 2 * hd:].reshape(BATCH, SEQ, NA_HEADS, NA_HEAD_DIM)
    return x, new_k, new_v


def _mla_weights(w_q_up, w_kv_up, q_g, k_g):
    wq = w_q_up.reshape(MLA_Q_LORA, MLA_HEADS, MLA_QK_DIM)
    wq = jnp.pad(wq, ((0, 0), (0, 0), (0, LANE - MLA_QK_DIM))).reshape(MLA_Q_LORA, MLA_HEADS * LANE)
    kv = w_kv_up.reshape(MLA_KV_LORA, MLA_HEADS, MLA_NOPE_DIM + MLA_V_DIM)
    wk_top = jnp.pad(kv[:, :, :MLA_NOPE_DIM], ((0, 0), (0, 0), (0, LANE - MLA_NOPE_DIM)))
    eye = jnp.eye(MLA_ROPE_DIM, dtype=F32)
    pe_rows = jnp.pad(eye, ((0, 0), (MLA_NOPE_DIM, LANE - MLA_QK_DIM)))
    pe_rows = jnp.broadcast_to(pe_rows[:, None, :], (MLA_ROPE_DIM, MLA_HEADS, LANE))
    k_in = 2 * LANE
    wk = jnp.concatenate([wk_top, pe_rows,
                          jnp.zeros((k_in - MLA_KV_LORA - MLA_ROPE_DIM, MLA_HEADS, LANE), F32)], axis=0)
    wk = wk.reshape(k_in, MLA_HEADS * LANE)
    wv = jnp.pad(kv[:, :, MLA_NOPE_DIM:].reshape(MLA_KV_LORA, MLA_HEADS * MLA_V_DIM),
                 ((0, k_in - MLA_KV_LORA), (0, 0)))
    wkv = jnp.concatenate([wk, wv], axis=1)
    pad_g = LANE - MLA_QK_DIM
    qgain = jnp.tile(jnp.pad(q_g * MLA_QK_DIM ** -0.5, (0, pad_g)), MLA_HEADS).reshape(1, -1)
    kgain = jnp.tile(jnp.pad(k_g, (0, pad_g)), MLA_HEADS)
    kvgain = jnp.concatenate([kgain, jnp.ones((MLA_HEADS * MLA_V_DIM,), F32)]).reshape(1, -1)
    return wq.astype(BF16), wkv.astype(BF16), qgain, kvgain


def _mla_layer(x, g_mix, mod, cache_ckv, cache_kpe, w_in, q_a_g, kv_a_g, w_q_up, w_kv_up, q_g, k_g,
               w_out):
    n_lat = MLA_Q_LORA + MLA_KV_LORA + MLA_ROPE_DIM
    w_in_p = jnp.pad(w_in, ((0, 0), (0, 512 - n_lat))).astype(BF16)
    proj = _modmm(x, g_mix.reshape(1, -1), mod, w_in_p, shift_idx=0, scale_idx=1, tn=512,
                  name="mla_in")
    cq, ckv, kin = _mla_latents(proj, q_a_g, kv_a_g)
    wq, wkv, qgain, kvgain = _mla_weights(w_q_up, w_kv_up, q_g, k_g)
    n_ktiles = MLA_HEADS * LANE // 256
    npt = PROMPT_TOK // TM
    spb = DEC_SEQ // TM
    q_tab = _rope_tables(DEC_SEQ, n_identity_rows_first=TM)
    q = _mmn(cq, wq, qgain, n_norm_tiles=n_ktiles, count=MLA_QK_DIM,
             rope=(q_tab, lambda i: jnp.where(i < npt, 0, 1 + (i - npt) % spb)), name="mla_q")
    kv_p = _mmn(kin[:PROMPT_TOK], wkv, kvgain, n_norm_tiles=n_ktiles, count=MLA_QK_DIM,
                name="mla_kv_prompt")
    kin_ctx = jnp.concatenate(
        [cache_ckv, cache_kpe, jnp.zeros((DEC_BATCH, PAST_LEN, 2 * LANE - MLA_KV_LORA - MLA_ROPE_DIM),
                                         F32)], axis=-1).astype(BF16)
    kin_s = jnp.concatenate([kin[PROMPT_TOK:].reshape(DEC_BATCH, DEC_SEQ, 2 * LANE), kin_ctx], axis=1)
    n_keys = DEC_SEQ + PAST_LEN
    kpb = n_keys // TM
    k_tab = _rope_tables(DEC_SEQ, n_identity_rows_last=PAST_LEN)
    kv_s = _mmn(kin_s.reshape(DEC_BATCH * n_keys, 2 * LANE), wkv, kvgain, n_norm_tiles=n_ktiles,
                count=MLA_QK_DIM, rope=(k_tab, lambda i: i % kpb), name="mla_kv_sample")
    hw = MLA_HEADS * LANE
    n_pairs = MLA_HEADS // 2
    q_by_seq = q.reshape(TOKENS // SEQ, SEQ, hw)
    q_by_dec = q.reshape(TOKENS // DEC_SEQ, DEC_SEQ, hw)
    kv_p3 = kv_p.reshape(BATCH, SEQ, -1)
    kv_s3 = kv_s.reshape(DEC_BATCH, n_keys, -1)
    o_p = _pair_attn(q_by_seq, kv_p3, kv_p3, b=BATCH, q_col=0, k_col=0, v_col=hw // LANE, slot=LANE,
                     tq=SEQ, n_pairs=n_pairs, name="mla_context_attn")
    o_s = _pair_attn(q_by_dec, kv_s3, kv_s3, b=DEC_BATCH, q_batch0=PROMPT_TOK // DEC_SEQ, q_col=0,
                     k_col=0, v_col=hw // LANE, slot=LANE, tq=256, n_pairs=n_pairs,
                     name="mla_latent_attn")
    ov = MLA_HEADS * MLA_V_DIM
    o = jnp.concatenate([o_p.reshape(PROMPT_TOK, ov), o_s.reshape(SAMPLE_TOK, ov)], axis=0)
    x = _mm_res(o, w_out.astype(BF16), x, mod, 2, name="mla_out")
    new_ckv = ckv[:PROMPT_TOK].reshape(BATCH, SEQ, MLA_KV_LORA)
    new_kpe = proj[:PROMPT_TOK, MLA_Q_LORA + MLA_KV_LORA:n_lat].reshape(BATCH, SEQ, MLA_ROPE_DIM)
    return x, new_ckv, new_kpe


def _gdn_layer(x, g_mix, mod, s_fwd, s_bwd, w_in, conv_w, a_log, dt_bias, norm_g, w_out):
    n_main = GDN_QKV + GDN_KEY
    gm = g_mix.reshape(1, -1)
    proj = _modmm(x, gm, mod, w_in[:, :n_main].astype(BF16), shift_idx=0, scale_idx=1, tn=512,
                  name="gdn_in")
    w_ab = jnp.pad(w_in[:, n_main:], ((0, 0), (0, LANE - 4 * GDN_HEADS))).astype(BF16)
    ab = _modmm(x, gm, mod, w_ab, shift_idx=0, scale_idx=1, tn=LANE, name="gdn_in_ab")
    gf, gb, beta = _gdn_gates(ab[:, :4 * GDN_HEADS].T, a_log, dt_bias)
    outs = []
    states = []
    for lo, hi, b, n, s0f, s0b in (
            (0, PROMPT_TOK, BATCH, SEQ, None, None),
            (PROMPT_TOK, TOKENS, DEC_BATCH, DEC_SEQ, s_fwd, s_bwd)):
        if s0f is None:
            s0f = jnp.zeros((b, GDN_HEADS, GDN_HEAD_DIM, GDN_HEAD_DIM), F32)
            s0b = s0f
        qkv = _gdn_conv(proj.reshape(TOKENS // n, n, n_main), conv_w, b, lo // n)
        cols, rows = _gdn_gate_tables(gf[:, lo:hi], gb[:, lo:hi], beta[:, lo:hi], b, n)
        pf, pb = _gdn_prep(qkv, cols, rows)
        o_f, o_b, sf, sb = _gdn_scan(pf, pb, s0f, s0b)
        outs.append((o_f.reshape(b * n, GDN_KEY), o_b.reshape(b * n, GDN_KEY)))
        states.append((sf, sb))
    y = _gdn_out(outs[0], outs[1], proj, norm_g)
    x = _mm_res(y, w_out.astype(BF16), x, mod, 2, name="gdn_out_proj")
    return x, states[0][0], states[0][1]


def kernel(x_prompt, x_sample, cache_na_k, cache_na_v, cache_mla_ckv, cache_mla_kpe, state_gdn_fwd,
           state_gdn_bwd, c, c_ctx, ada_w, ada_b, norm_mix_g, norm_ffn_g, na_w_in, na_q_g, na_k_g,
           na_rpb, na_w_out, mla_w_in, mla_q_a_g, mla_kv_a_g, mla_w_q_up, mla_w_kv_up, mla_q_g,
           mla_k_g, mla_w_out, gdn_w_in, gdn_conv_w, gdn_a_log, gdn_dt_bias, gdn_norm_g, gdn_w_out,
           moe_w_router, moe_b_router, moe_w_gate_up, moe_b_gate_up, moe_w_down, moe_b_down):
    x = jnp.concatenate([x_prompt.reshape(PROMPT_TOK, D_MODEL), x_sample.reshape(SAMPLE_TOK, D_MODEL)],
                        axis=0)
    c_all = jnp.concatenate([c_ctx[None], c, jnp.zeros((16 - N_GROUPS, D_MODEL), F32)], axis=0)
    mods = _ada_mods(c_all, ada_w, ada_b)[:, :N_GROUPS].reshape(DEPTH, N_GROUPS, 6, 1, D_MODEL)
    new_na_k, new_na_v, new_ckv, new_kpe, new_sf, new_sb = [], [], [], [], [], []
    for i in range(DEPTH):
        kind, j = i % N_MIXERS, i // N_MIXERS
        mod = mods[i]
        if kind == 0:
            x, nk, nv = _na_layer(x, norm_mix_g[i], mod, cache_na_k[:, j], cache_na_v[:, j],
                                  na_w_in[j], na_q_g[j], na_k_g[j], na_rpb[j], na_w_out[j])
            new_na_k.append(nk)
            new_na_v.append(nv)
        elif kind == 1:
            x, ckv, kpe = _mla_layer(x, norm_mix_g[i], mod, cache_mla_ckv[:, j], cache_mla_kpe[:, j],
                                     mla_w_in[j], mla_q_a_g[j], mla_kv_a_g[j], mla_w_q_up[j],
                                     mla_w_kv_up[j], mla_q_g[j], mla_k_g[j], mla_w_out[j])
            new_ckv.append(ckv)
            new_kpe.append(kpe)
        else:
            x, sf, sb = _gdn_layer(x, norm_mix_g[i], mod, state_gdn_fwd[:, j], state_gdn_bwd[:, j],
                                   gdn_w_in[j], gdn_conv_w[j], gdn_a_log[j], gdn_dt_bias[j],
                                   gdn_norm_g[j], gdn_w_out[j])
            new_sf.append(sf)
            new_sb.append(sb)
        x = _moe_layer(x, norm_ffn_g[i], mod, moe_w_router[i], moe_b_router[i], moe_w_gate_up[i],
                       moe_b_gate_up[i], moe_w_down[i], moe_b_down[i])
    return (x[:PROMPT_TOK].reshape(BATCH, SEQ, D_MODEL),
            x[PROMPT_TOK:].reshape(DEC_BATCH, DEC_SEQ, D_MODEL),
            jnp.stack(new_na_k, axis=1), jnp.stack(new_na_v, axis=1),
            jnp.stack(new_ckv, axis=1), jnp.stack(new_kpe, axis=1),
            jnp.stack(new_sf, axis=1), jnp.stack(new_sb, axis=1))
```

```python
import functools
import math

import numpy as np
import jax
import jax.numpy as jnp
from jax import lax
from jax.experimental import pallas as pl
from jax.experimental.pallas import tpu as pltpu

F32 = jnp.float32
BF16 = jnp.bfloat16
HIGHEST = lax.Precision.HIGHEST

D_MODEL = 1024
BATCH = 32
SEQ = 256
DEPTH = 4
DEC_BATCH = 8
DEC_SEQ = 2048
PAST_LEN = 512
GRID_W = 64
N_MIXERS = 3
NA_HEADS = 16
NA_HEAD_DIM = 64
NA_ROWS = 8
NA_COLS = 16
MLA_HEADS = 16
MLA_NOPE_DIM = 64
MLA_ROPE_DIM = 32
MLA_QK_DIM = MLA_NOPE_DIM + MLA_ROPE_DIM
MLA_V_DIM = 64
MLA_Q_LORA = 256
MLA_KV_LORA = 128
GDN_HEADS = 8
GDN_HEAD_DIM = 128
GDN_KEY = GDN_HEADS * GDN_HEAD_DIM
GDN_QKV = 3 * GDN_KEY
GDN_CONV_W = 5
GDN_CHUNK = 64
N_EXPERTS = 32
TOP_K = 4
D_EXPERT = D_MODEL
SWIGLU_ALPHA = 1.702
SWIGLU_LIMIT = 7.0
ROPE_BASE = 10000.0
EPS = 1e-6

PROMPT_TOK = BATCH * SEQ
SAMPLE_TOK = DEC_BATCH * DEC_SEQ
TOKENS = PROMPT_TOK + SAMPLE_TOK
N_GROUPS = 1 + DEC_BATCH

LANE = 128
TM = 512
MOE_TM = 512
MOE_ROWS = TOKENS * TOP_K
MOE_ITEMS = MOE_ROWS // MOE_TM + N_EXPERTS
TOKEN_TILE_ROWS = D_MODEL // LANE
ROW_DMA_BLOCK = 256
ROW_DMA_UNROLL = 8
VMEM_LIMIT = 56 * 1024 * 1024
NEG_BIG = -1e30


def _cparams(n_axes, vmem=None):
    return pltpu.CompilerParams(dimension_semantics=("arbitrary",) * n_axes,
                                vmem_limit_bytes=vmem)


def _dot(a, b):
    return jnp.dot(a, b, preferred_element_type=F32)


def _dot_nt(a, b):
    return lax.dot_general(a, b, (((1,), (1,)), ((), ())), preferred_element_type=F32)


def _dot_tn(a, b):
    return lax.dot_general(a, b, (((0,), (0,)), ((), ())), preferred_element_type=F32)


def _dot_hi(a, b):
    return jnp.dot(a, b, preferred_element_type=F32, precision=HIGHEST)


def _group_of_tile(i, tm):
    npt = PROMPT_TOK // tm
    spb = DEC_SEQ // tm
    return jnp.where(i < npt, 0, 1 + (i - npt) // spb)


def _block_diag_ones(n, group):
    idx = np.arange(n) // group
    return jnp.asarray((idx[:, None] == idx[None, :]).astype(np.float32), BF16)


def _ada_kernel(c_ref, w_ref, b_ref, o_ref):
    c = c_ref[...]
    s = c * jax.nn.sigmoid(c)
    o_ref[0] = _dot_hi(s, w_ref[0]) + b_ref[0]


def _ada_mods(c_all, ada_w, ada_b):
    tn = 512
    n = 6 * D_MODEL
    return pl.pallas_call(
        _ada_kernel,
        grid=(DEPTH, n // tn),
        in_specs=[pl.BlockSpec((16, D_MODEL), lambda l, j: (0, 0)),
                  pl.BlockSpec((1, D_MODEL, tn), lambda l, j: (l, 0, j)),
                  pl.BlockSpec((1, 1, tn), lambda l, j: (l, 0, j))],
        out_specs=pl.BlockSpec((1, 16, tn), lambda l, j: (l, 0, j)),
        out_shape=jax.ShapeDtypeStruct((DEPTH, 16, n), F32),
        compiler_params=_cparams(2),
        name="ada_mods",
    )(c_all, ada_w, ada_b.reshape(DEPTH, 1, n))


def _modulate(x, g, mod_ref, shift_idx, scale_idx):
    ms = jnp.mean(x * x, axis=-1, keepdims=True)
    y = x * lax.rsqrt(ms + EPS) * g
    return y * (1.0 + mod_ref[0, scale_idx]) + mod_ref[0, shift_idx]


def _group_norm(acc, bd, gain, inv_count):
    ss = _dot((acc * acc).astype(BF16), bd)
    return acc * lax.rsqrt(ss * inv_count + EPS) * gain


def _modmm_kernel(*refs, shift_idx, scale_idx, n_norm_tiles, inv_count):
    if n_norm_tiles:
        x_ref, g_ref, mod_ref, w_ref, gain_ref, bd_ref, o_ref, h_ref = refs
    else:
        x_ref, g_ref, mod_ref, w_ref, o_ref, h_ref = refs
    j = pl.program_id(1)

    @pl.when(j == 0)
    def _():
        h_ref[...] = _modulate(x_ref[...], g_ref[...], mod_ref, shift_idx, scale_idx).astype(BF16)

    acc = _dot(h_ref[...], w_ref[...])
    if n_norm_tiles:
        @pl.when(j < n_norm_tiles)
        def _():
            o_ref[...] = _group_norm(acc, bd_ref[...], gain_ref[...], inv_count).astype(o_ref.dtype)

        @pl.when(j >= n_norm_tiles)
        def _():
            o_ref[...] = acc.astype(o_ref.dtype)
    else:
        o_ref[...] = acc.astype(o_ref.dtype)


def _modmm(x, g, mod, w, *, shift_idx, scale_idx, tn, out_dtype=F32, norm=None, name="modmm"):
    t, d = x.shape
    n = w.shape[1]
    in_specs = [pl.BlockSpec((TM, d), lambda i, j: (i, 0)),
                pl.BlockSpec((1, d), lambda i, j: (0, 0)),
                pl.BlockSpec((1, 6, 1, d), lambda i, j: (_group_of_tile(i, TM), 0, 0, 0)),
                pl.BlockSpec((d, tn), lambda i, j: (0, j))]
    args = [x, g, mod, w]
    n_norm_tiles, inv_count = 0, 1.0
    if norm is not None:
        n_norm_tiles, gains, group = norm
        inv_count = 1.0 / group
        in_specs += [pl.BlockSpec((1, tn), lambda i, j: (0, j)),
                     pl.BlockSpec((tn, tn), lambda i, j: (0, 0))]
        args += [gains, _block_diag_ones(tn, group)]
    return pl.pallas_call(
        functools.partial(_modmm_kernel, shift_idx=shift_idx, scale_idx=scale_idx,
                          n_norm_tiles=n_norm_tiles, inv_count=inv_count),
        grid=(t // TM, n // tn),
        in_specs=in_specs,
        out_specs=pl.BlockSpec((TM, tn), lambda i, j: (i, j)),
        out_shape=jax.ShapeDtypeStruct((t, n), out_dtype),
        scratch_shapes=[pltpu.VMEM((TM, d), BF16)],
        compiler_params=_cparams(2, VMEM_LIMIT),
        name=name,
    )(*args)


def _mm_res_kernel(a_ref, w_ref, res_ref, gate_ref, o_ref):
    o_ref[...] = res_ref[...] + gate_ref[0, 0] * _dot(a_ref[...], w_ref[...])


def _mm_res(a, w, res, mod, gate_idx, *, tn=512, name="mm_res"):
    t, k = a.shape
    n = w.shape[1]
    return pl.pallas_call(
        _mm_res_kernel,
        grid=(t // TM, n // tn),
        in_specs=[pl.BlockSpec((TM, k), lambda i, j: (i, 0)),
                  pl.BlockSpec((k, tn), lambda i, j: (0, j)),
                  pl.BlockSpec((TM, tn), lambda i, j: (i, j)),
                  pl.BlockSpec((1, 1, 1, tn), lambda i, j: (_group_of_tile(i, TM), gate_idx, 0, j))],
        out_specs=pl.BlockSpec((TM, tn), lambda i, j: (i, j)),
        out_shape=jax.ShapeDtypeStruct((t, n), F32),
        compiler_params=_cparams(2, VMEM_LIMIT),
        name=name,
    )(a, w, res, mod)


def _rope_slots(y, c, s1, s2):
    outs = []
    for s in range(y.shape[1] // LANE):
        ys = y[:, s * LANE:(s + 1) * LANE]
        outs.append(ys * c + pltpu.roll(ys, LANE - MLA_ROPE_DIM // 2, 1) * s1
                    + pltpu.roll(ys, MLA_ROPE_DIM // 2, 1) * s2)
    return jnp.concatenate(outs, axis=1)


def _mmn_kernel(*refs, n_norm_tiles, inv_count, rope):
    if rope:
        a_ref, w_ref, gain_ref, bd_ref, c_ref, s1_ref, s2_ref, o_ref = refs
    else:
        a_ref, w_ref, gain_ref, bd_ref, o_ref = refs
    j = pl.program_id(1)
    acc = _dot(a_ref[...], w_ref[...])

    @pl.when(j < n_norm_tiles)
    def _():
        y = _group_norm(acc, bd_ref[...], gain_ref[...], inv_count)
        if rope:
            y = _rope_slots(y, c_ref[0], s1_ref[0], s2_ref[0])
        o_ref[...] = y.astype(o_ref.dtype)

    @pl.when(j >= n_norm_tiles)
    def _():
        o_ref[...] = acc.astype(o_ref.dtype)


def _mmn(a, w, gains, *, n_norm_tiles, count, rope=None, tn=256, name="mmn"):
    t, k = a.shape
    n = w.shape[1]
    in_specs = [pl.BlockSpec((TM, k), lambda i, j: (i, 0)),
                pl.BlockSpec((k, tn), lambda i, j: (0, j)),
                pl.BlockSpec((1, tn), lambda i, j: (0, j)),
                pl.BlockSpec((tn, tn), lambda i, j: (0, 0))]
    args = [a, w, gains, _block_diag_ones(tn, LANE)]
    if rope is not None:
        tables, pos_fn = rope
        for q in range(3):
            in_specs.append(pl.BlockSpec((1, TM, LANE), lambda i, j, q=q: (q, pos_fn(i), 0)))
            args.append(tables)
    return pl.pallas_call(
        functools.partial(_mmn_kernel, n_norm_tiles=n_norm_tiles, inv_count=1.0 / count,
                          rope=rope is not None),
        grid=(t // TM, n // tn),
        in_specs=in_specs,
        out_specs=pl.BlockSpec((TM, tn), lambda i, j: (i, j)),
        out_shape=jax.ShapeDtypeStruct((t, n), BF16),
        compiler_params=_cparams(2, VMEM_LIMIT),
        name=name,
    )(*args)


def _lane_half_mask(h, width=LANE):
    lane = lax.broadcasted_iota(jnp.int32, (1, width), 1)
    return (lane // (width // 2)) == h


def _pair_attn_kernel(q_ref, k_ref, v_ref, o_ref, *, slot, pairs_blk):
    q = q_ref[0]
    k = k_ref[0]
    v = v_ref[0].astype(BF16)
    masks = [_lane_half_mask(h) for h in range(2)]
    heads = [(pp, h) for pp in range(pairs_blk) for h in range(2)]
    if slot == LANE:
        ss = [_dot_nt(q[:, (2 * pp + h) * LANE:(2 * pp + h + 1) * LANE].astype(BF16),
                      k[:, (2 * pp + h) * LANE:(2 * pp + h + 1) * LANE].astype(BF16)) for pp, h in heads]
    else:
        kb = k.astype(BF16)
        ss = [_dot_nt(jnp.where(masks[h], q[:, pp * LANE:(pp + 1) * LANE], 0).astype(BF16),
                      kb[:, pp * LANE:(pp + 1) * LANE]) for pp, h in heads]
    ps = [jnp.exp(s - jnp.max(s, axis=-1, keepdims=True)) for s in ss]
    os_ = [_dot(p.astype(BF16), jnp.where(masks[h], v[:, pp * LANE:(pp + 1) * LANE], 0).astype(BF16))
           for p, (pp, h) in zip(ps, heads)]
    scaled = [o * (1.0 / jnp.sum(p, axis=-1, keepdims=True)) for o, p in zip(os_, ps)]
    for pp in range(pairs_blk):
        o_ref[0, :, pp * LANE:(pp + 1) * LANE] = (scaled[2 * pp] + scaled[2 * pp + 1]).astype(o_ref.dtype)


def _pair_attn(q_arr, k_arr, v_arr, *, b, q_batch0=0, kv_batch0=0, q_col, k_col, v_col, slot, tq,
               n_pairs, pairs_blk, name):
    nq = q_arr.shape[1]
    nk = k_arr.shape[1]
    qw = 2 * slot * pairs_blk
    vw = LANE * pairs_blk
    qc, kc, vc = q_col // pairs_blk, k_col // pairs_blk, v_col // pairs_blk
    return pl.pallas_call(
        functools.partial(_pair_attn_kernel, slot=slot, pairs_blk=pairs_blk),
        grid=(b, n_pairs // pairs_blk, nq // tq),
        in_specs=[pl.BlockSpec((1, tq, qw), lambda bi, hp, qi: (q_batch0 + bi, qi, qc + hp)),
                  pl.BlockSpec((1, nk, qw), lambda bi, hp, qi: (kv_batch0 + bi, 0, kc + hp)),
                  pl.BlockSpec((1, nk, vw), lambda bi, hp, qi: (kv_batch0 + bi, 0, vc + hp))],
        out_specs=pl.BlockSpec((1, tq, vw), lambda bi, hp, qi: (bi, qi, hp)),
        out_shape=jax.ShapeDtypeStruct((b, nq, n_pairs * LANE), BF16),
        compiler_params=_cparams(3, VMEM_LIMIT),
        name=name,
    )(q_arr, k_arr, v_arr)


NA_QBLOCK_ROWS = 8
NA_WIN = NA_ROWS * GRID_W


def _na_lat_kernel(q_ref, k_ref, v_ref, kc_ref, vc_ref, bias_ref, o_ref, kbf, vbf, *, rows):
    qi = pl.program_id(2)

    @pl.when(qi == 0)
    def _():
        kbf[...] = k_ref[0].astype(BF16)
        v = v_ref[0]
        for h in range(2):
            vbf[h] = jnp.where(_lane_half_mask(h), v, 0.0).astype(BF16)

    q = q_ref[0]
    kc = kc_ref[0].astype(BF16)
    vc = vc_ref[0]
    masks = [_lane_half_mask(h) for h in range(2)]
    qhs = [jnp.where(mask, q, 0.0).astype(BF16) for mask in masks]
    s_cs = [_dot_nt(qh, kc) for qh in qhs]
    units = []
    for rr in range(NA_QBLOCK_ROWS):
        r = qi * NA_QBLOCK_ROWS + rr
        r0 = jnp.clip(r - NA_ROWS // 2, 0, rows - NA_ROWS)
        start = pl.multiple_of(r0 * GRID_W, GRID_W)
        for h in range(2):
            units.append((h, slice(rr * GRID_W, (rr + 1) * GRID_W), start, r - r0))
    s_ls = [_dot_nt(qhs[h][sl], kbf[pl.ds(start, NA_WIN), :]) + bias_ref[pat, h]
            for h, sl, start, pat in units]
    m_cs = [jnp.max(s_c, axis=-1, keepdims=True) for s_c in s_cs]
    p_cs = [jnp.exp(s_c - m_c) for s_c, m_c in zip(s_cs, m_cs)]
    l_cs = [jnp.sum(p_c, axis=-1, keepdims=True) for p_c in p_cs]
    o_cs = [_dot(p_c.astype(BF16), jnp.where(mask, vc, 0.0).astype(BF16)) for p_c, mask in zip(p_cs, masks)]
    ms = [jnp.maximum(jnp.max(s_l, axis=-1, keepdims=True), m_cs[h][sl])
          for s_l, (h, sl, _, _) in zip(s_ls, units)]
    p_ls = [jnp.exp(s_l - m) for s_l, m in zip(s_ls, ms)]
    o_ls = [_dot(p_l.astype(BF16), vbf[h, pl.ds(start, NA_WIN), :])
            for p_l, (h, _, start, _) in zip(p_ls, units)]
    out = [jnp.zeros((GRID_W, LANE), F32) for _ in range(NA_QBLOCK_ROWS)]
    for (h, sl, _, _), m, p_l, o_l in zip(units, ms, p_ls, o_ls):
        a_c = jnp.exp(m_cs[h][sl] - m)
        l = jnp.sum(p_l, axis=-1, keepdims=True) + a_c * l_cs[h][sl]
        rr = sl.start // GRID_W
        out[rr] = out[rr] + (o_l + a_c * o_cs[h][sl]) * (1.0 / l)
    for rr in range(NA_QBLOCK_ROWS):
        o_ref[0, rr * GRID_W:(rr + 1) * GRID_W, :] = out[rr].astype(o_ref.dtype)


def _na_bias_table(rpb):
    col = np.arange(GRID_W)
    col_start = np.clip(col - NA_COLS // 2, 0, GRID_W - NA_COLS)
    kc = np.arange(GRID_W)
    inside = (kc[None, :] >= col_start[:, None]) & (kc[None, :] < col_start[:, None] + NA_COLS)
    pad = GRID_W - NA_COLS
    rpb_pad = jnp.pad(rpb, ((0, 0), (0, 0), (pad, pad + 1)))
    by_col = jnp.stack([rpb_pad[:, :, GRID_W - 1 - c:2 * GRID_W - 1 - c] for c in range(GRID_W)], axis=2)
    by_col = jnp.where(jnp.asarray(inside)[None, None], by_col, NEG_BIG)
    tab = jnp.stack([by_col[:, NA_ROWS - 1 - p:2 * NA_ROWS - 1 - p] for p in range(NA_ROWS)], axis=0)
    tab = tab.transpose(0, 1, 3, 2, 4)
    return tab.reshape(NA_ROWS, NA_HEADS, GRID_W, NA_WIN).astype(F32)


def _na_latent_attn(qkv, kctx, vctx, bias, batch0):
    n = qkv.shape[1]
    b = kctx.shape[0]
    rows = n // GRID_W
    n_pairs = NA_HEADS // 2
    tq = NA_QBLOCK_ROWS * GRID_W
    past = kctx.shape[1]
    return pl.pallas_call(
        functools.partial(_na_lat_kernel, rows=rows),
        grid=(b, n_pairs, n // tq),
        in_specs=[pl.BlockSpec((1, tq, LANE), lambda bi, hp, qi: (batch0 + bi, qi, hp)),
                  pl.BlockSpec((1, n, LANE), lambda bi, hp, qi: (batch0 + bi, 0, n_pairs + hp)),
                  pl.BlockSpec((1, n, LANE), lambda bi, hp, qi: (batch0 + bi, 0, 2 * n_pairs + hp)),
                  pl.BlockSpec((1, past, LANE), lambda bi, hp, qi: (bi, 0, hp)),
                  pl.BlockSpec((1, past, LANE), lambda bi, hp, qi: (bi, 0, hp)),
                  pl.BlockSpec((NA_ROWS, 2, GRID_W, NA_WIN), lambda bi, hp, qi: (0, hp, 0, 0))],
        out_specs=pl.BlockSpec((1, tq, LANE), lambda bi, hp, qi: (bi, qi, hp)),
        out_shape=jax.ShapeDtypeStruct((b, n, n_pairs * LANE), BF16),
        scratch_shapes=[pltpu.VMEM((n, LANE), BF16), pltpu.VMEM((2, n, LANE), BF16)],
        compiler_params=_cparams(3, VMEM_LIMIT),
        name="na_latent_attn",
    )(qkv, qkv, qkv, kctx, vctx, bias)


def _mla_lat_kernel(p_ref, qg_ref, kvg_ref, cq_ref, ckv_ref, kin_ref):
    p = p_ref[...]
    cq = p[:, :MLA_Q_LORA]
    cq = cq * lax.rsqrt(jnp.mean(cq * cq, axis=-1, keepdims=True) + EPS) * qg_ref[...]
    cq_ref[...] = cq.astype(BF16)
    ckv = p[:, MLA_Q_LORA:MLA_Q_LORA + MLA_KV_LORA]
    ckv = ckv * lax.rsqrt(jnp.mean(ckv * ckv, axis=-1, keepdims=True) + EPS) * kvg_ref[...]
    ckv_ref[...] = ckv
    tail = p[:, MLA_Q_LORA + MLA_KV_LORA:]
    kin_ref[...] = jnp.concatenate([ckv, tail], axis=1).astype(BF16)


def _mla_latents(proj, q_a_g, kv_a_g):
    t = proj.shape[0]
    w = proj.shape[1]
    return pl.pallas_call(
        _mla_lat_kernel,
        grid=(t // TM,),
        in_specs=[pl.BlockSpec((TM, w), lambda i: (i, 0)),
                  pl.BlockSpec((1, MLA_Q_LORA), lambda i: (0, 0)),
                  pl.BlockSpec((1, MLA_KV_LORA), lambda i: (0, 0))],
        out_specs=[pl.BlockSpec((TM, MLA_Q_LORA), lambda i: (i, 0)),
                   pl.BlockSpec((TM, MLA_KV_LORA), lambda i: (i, 0)),
                   pl.BlockSpec((TM, 2 * LANE), lambda i: (i, 0))],
        out_shape=[jax.ShapeDtypeStruct((t, MLA_Q_LORA), BF16),
                   jax.ShapeDtypeStruct((t, MLA_KV_LORA), F32),
                   jax.ShapeDtypeStruct((t, 2 * LANE), BF16)],
        compiler_params=_cparams(1),
        name="mla_latents",
    )(proj, q_a_g.reshape(1, -1), kv_a_g.reshape(1, -1))


def _rope_tables(n_pos, n_identity_rows_first=0, n_identity_rows_last=0):
    n_freq = MLA_ROPE_DIM // 4
    inv = ROPE_BASE ** (-jnp.arange(n_freq, dtype=F32) / n_freq)
    t = jnp.arange(n_pos)
    row = (t // GRID_W).astype(F32)
    col = (t % GRID_W).astype(F32)
    ang = jnp.concatenate([row[:, None] * inv, col[:, None] * inv], axis=-1)
    cos, sin = jnp.cos(ang), jnp.sin(ang)
    half = MLA_ROPE_DIM // 2
    ones = jnp.ones((n_pos, MLA_NOPE_DIM), F32)
    zeros = jnp.zeros((n_pos, MLA_NOPE_DIM), F32)
    pad1 = jnp.ones((n_pos, LANE - MLA_QK_DIM), F32)
    pad0 = jnp.zeros((n_pos, LANE - MLA_QK_DIM), F32)
    zh = jnp.zeros((n_pos, half), F32)
    c = jnp.concatenate([ones, cos, cos, pad1], axis=1)
    s1 = jnp.concatenate([zeros, -sin, zh, pad0], axis=1)
    s2 = jnp.concatenate([zeros, zh, sin, pad0], axis=1)
    tab = jnp.stack([c, s1, s2])

    def ident(nrows):
        return jnp.stack([jnp.ones((nrows, LANE), F32), jnp.zeros((nrows, LANE), F32),
                          jnp.zeros((nrows, LANE), F32)])

    parts = []
    if n_identity_rows_first:
        parts.append(ident(n_identity_rows_first))
    parts.append(tab)
    if n_identity_rows_last:
        parts.append(ident(n_identity_rows_last))
    return jnp.concatenate(parts, axis=1)


def _gdn_conv_kernel(x_ref, w_ref, o_ref, pad_ref, *, n):
    j = pl.program_id(1)
    halo = 8
    cw = x_ref.shape[2]
    pad_ref[0:halo, :] = jnp.zeros((halo, cw), F32)
    pad_ref[halo + n:2 * halo + n, :] = jnp.zeros((halo, cw), F32)
    pad_ref[halo:halo + n, :] = x_ref[0]
    acc = jnp.zeros((n, cw), F32)
    for t in range(GDN_CONV_W):
        off = halo - GDN_CONV_W // 2 + t
        acc = acc + pad_ref[off:off + n, :] * w_ref[t:t + 1, :]
    y = acc * jax.nn.sigmoid(acc)
    n_qk_blocks = 2 * GDN_KEY // cw
    n_q_blocks = GDN_KEY // cw
    outs = []
    for s in range(cw // LANE):
        ys = y[:, s * LANE:(s + 1) * LANE]
        ss = jnp.sum(ys * ys, axis=-1, keepdims=True)
        r = jnp.where(j < n_qk_blocks, lax.rsqrt(ss + EPS), 1.0)
        r = r * jnp.where(j < n_q_blocks, GDN_HEAD_DIM ** -0.5, 1.0)
        outs.append(ys * r)
    o_ref[0] = jnp.concatenate(outs, axis=1)


def _gdn_conv(proj3, conv_w, b, batch0):
    n = proj3.shape[1]
    cw = 512
    return pl.pallas_call(
        functools.partial(_gdn_conv_kernel, n=n),
        grid=(b, GDN_QKV // cw),
        in_specs=[pl.BlockSpec((1, n, cw), lambda bi, j: (batch0 + bi, 0, j)),
                  pl.BlockSpec((GDN_CONV_W, cw), lambda bi, j: (0, j))],
        out_specs=pl.BlockSpec((1, n, cw), lambda bi, j: (bi, 0, j)),
        out_shape=jax.ShapeDtypeStruct((b, n, GDN_QKV), F32),
        scratch_shapes=[pltpu.VMEM((n + 16, cw), F32)],
        compiler_params=_cparams(2, VMEM_LIMIT),
        name="gdn_conv",
    )(proj3, conv_w)


def _gdn_gates_kernel(ab_ref, alog_ref, dtb_ref, uf_ref, ub_ref, gf_ref, gb_ref, beta_ref):
    ab = ab_ref[...]
    a = jnp.concatenate([ab[0:8], ab[16:24]], axis=0)
    bb = jnp.concatenate([ab[8:16], ab[24:32]], axis=0)
    z = a + dtb_ref[...]
    sp = jnp.maximum(z, 0.0) + jnp.log(1.0 + jnp.exp(-jnp.abs(z)))
    g = -jnp.exp(alog_ref[...]) * sp
    beta_ref[...] = jax.nn.sigmoid(bb)
    gf_ref[...] = _dot_hi(g, uf_ref[...])
    gb_ref[...] = _dot_hi(g, ub_ref[...])


def _gdn_gates(ab_t, a_log, dt_bias):
    t = ab_t.shape[1]
    tl = 256
    idx = np.arange(tl)
    same = (idx[:, None] // GDN_CHUNK) == (idx[None, :] // GDN_CHUNK)
    uf = jnp.asarray((same & (idx[:, None] <= idx[None, :])).astype(np.float32))
    ub = jnp.asarray((same & (idx[:, None] >= idx[None, :])).astype(np.float32))
    spec = pl.BlockSpec((16, tl), lambda i: (0, i))
    return pl.pallas_call(
        _gdn_gates_kernel,
        grid=(t // tl,),
        in_specs=[pl.BlockSpec((32, tl), lambda i: (0, i)),
                  pl.BlockSpec((16, 1), lambda i: (0, 0)),
                  pl.BlockSpec((16, 1), lambda i: (0, 0)),
                  pl.BlockSpec((tl, tl), lambda i: (0, 0)),
                  pl.BlockSpec((tl, tl), lambda i: (0, 0))],
        out_specs=[spec, spec, spec],
        out_shape=[jax.ShapeDtypeStruct((16, t), F32)] * 3,
        compiler_params=_cparams(1),
        name="gdn_gates",
    )(ab_t, a_log.reshape(16, 1), dt_bias.reshape(16, 1), uf, ub)


def _tri_inverse_many(a_list, block):
    n = a_list[0].shape[0]
    ri = lax.broadcasted_iota(jnp.int32, (n, n), 0)
    ci = lax.broadcasted_iota(jnp.int32, (n, n), 1)
    eye = (ri == ci).astype(F32)
    xs = [(-a).astype(BF16) for a in a_list]
    ts = [eye - a for a in a_list]
    pws = [_dot(x, x) for x in xs]
    n_steps = int(math.log2(block)) - 1
    for step in range(n_steps):
        pwbs = [pw.astype(BF16) for pw in pws]
        if step < n_steps - 1:
            sts = [_dot(jnp.concatenate([pwb, t.astype(BF16)], axis=0), pwb) for pwb, t in zip(pwbs, ts)]
            pws = [st[:n] for st in sts]
            ts = [t + st[n:] for t, st in zip(ts, sts)]
        else:
            ts = [t + _dot(t.astype(BF16), pwb) for t, pwb in zip(ts, pwbs)]
    a_his = [a.astype(BF16) for a in a_list]
    a_los = [(a - a_hi.astype(F32)).astype(BF16) for a, a_hi in zip(a_list, a_his)]
    t_his = [t.astype(BF16) for t in ts]
    t_los = [(t - t_hi.astype(F32)).astype(BF16) for t, t_hi in zip(ts, t_his)]
    at2s = [_dot(jnp.concatenate([a_hi, a_lo], axis=0), t_hi) for a_hi, a_lo, t_hi in zip(a_his, a_los, t_his)]
    at3s = [_dot(a_hi, t_lo) for a_hi, t_lo in zip(a_his, t_los)]
    resids = [((eye - t) - (at2[:n] + at2[n:] + at3)).astype(BF16) for t, at2, at3 in zip(ts, at2s, at3s)]
    return [t + _dot(t_hi, r) for t, t_hi, r in zip(ts, t_his, resids)]


GDN_SLOT = 5 * GDN_HEAD_DIM


GDN_GROUP = 256


def _gdn_prep_kernel(q_ref, k_ref, v_ref, cols_ref, rows_ref, pf_ref, pb_ref, *, heads_blk):
    c = GDN_CHUNK
    n = GDN_GROUP
    ri = lax.broadcasted_iota(jnp.int32, (n, n), 0)
    cj = lax.broadcasted_iota(jnp.int32, (n, n), 1)
    same = (ri // c) == (cj // c)
    lane = lax.broadcasted_iota(jnp.int32, (n, LANE), 1)
    heads = []
    for hh in range(heads_blk):
        lanes = slice(hh * GDN_HEAD_DIM, (hh + 1) * GDN_HEAD_DIM)
        q = q_ref[0, :, lanes]
        k = k_ref[0, :, lanes]
        kb16 = k.astype(BF16)
        heads.append((q, k, v_ref[0, :, lanes], _dot_nt(kb16, kb16), _dot_nt(q.astype(BF16), kb16)))
    chains = []
    for hh in range(heads_blk):
        cols = cols_ref[0, hh, 0]
        rows = rows_ref[0, hh, 0]
        kk = heads[hh][3]
        for direction in range(2):
            gcol = cols[:, 2 * direction:2 * direction + 1]
            bcol = cols[:, 2 * direction + 1:2 * direction + 2]
            glast = cols[:, 4 + direction:5 + direction]
            grow = rows[direction:direction + 1, :]
            if direction == 0:
                incl, strict = same & (ri >= cj), same & (ri > cj)
            else:
                incl, strict = same & (ri <= cj), same & (ri < cj)
            decay = jnp.where(incl, jnp.exp(jnp.where(incl, gcol - grow, 0.0)), 0.0)
            chains.append((hh, direction, gcol, bcol, glast, decay,
                           jnp.where(strict, kk * bcol * decay, 0.0)))
    tmats = _tri_inverse_many([ch[6] for ch in chains], c)
    uws = []
    for (hh, direction, gcol, bcol, glast, decay, _), tmat in zip(chains, tmats):
        q, k, v, _, _ = heads[hh]
        rhs = jnp.concatenate([v * bcol, k * (bcol * jnp.exp(gcol))], axis=1).astype(BF16)
        uws.append(_dot(tmat.astype(BF16), rhs))
    for (hh, direction, gcol, bcol, glast, decay, _), uw in zip(chains, uws):
        q, k, v, _, qk = heads[hh]
        attn = qk * decay
        folded = attn[:, :LANE] + attn[:, LANE:]
        folded = folded + pltpu.roll(folded, c, 1)
        egl = jnp.exp(glast)
        egl_hi = egl.astype(BF16).astype(F32)
        tail = jnp.where(lane < c + c // 2, egl_hi, egl - egl_hi)
        a128 = jnp.where(lane < c, folded, tail)
        slot = jnp.concatenate([uw, q * jnp.exp(gcol), k * jnp.exp(glast - gcol), a128], axis=1)
        out_ref = pf_ref if direction == 0 else pb_ref
        out_ref[0, :, hh * GDN_SLOT:(hh + 1) * GDN_SLOT] = slot.astype(BF16)


def _gdn_prep(qkv, cols, rows):
    b, n, _ = qkv.shape
    hb = 4
    nhb = GDN_HEADS // hb
    tb = GDN_GROUP
    w = hb * GDN_HEAD_DIM
    p_spec = pl.BlockSpec((1, tb, hb * GDN_SLOT), lambda bi, h, t: (bi, t, h))
    return pl.pallas_call(
        functools.partial(_gdn_prep_kernel, heads_blk=hb),
        grid=(b, nhb, n // tb),
        in_specs=[pl.BlockSpec((1, tb, w), lambda bi, h, t: (bi, t, h)),
                  pl.BlockSpec((1, tb, w), lambda bi, h, t: (bi, t, nhb + h)),
                  pl.BlockSpec((1, tb, w), lambda bi, h, t: (bi, t, 2 * nhb + h)),
                  pl.BlockSpec((1, hb, 1, tb, 8), lambda bi, h, t: (bi, h, t, 0, 0)),
                  pl.BlockSpec((1, hb, 1, 8, tb), lambda bi, h, t: (bi, h, t, 0, 0))],
        out_specs=[p_spec, p_spec],
        out_shape=[jax.ShapeDtypeStruct((b, n, GDN_HEADS * GDN_SLOT), BF16)] * 2,
        compiler_params=_cparams(3, VMEM_LIMIT),
        name="gdn_prep",
    )(qkv, qkv, qkv, cols, rows)


def _gdn_scan_kernel(pf_ref, pb_ref, s0f_ref, s0b_ref, of_ref, ob_ref, sf_ref, sb_ref, s_scr,
                     *, heads_blk, chunks_blk):
    c = GDN_CHUNK
    hd = GDN_HEAD_DIM
    t = pl.program_id(2)

    @pl.when(t == 0)
    def _():
        for hh in range(heads_blk):
            s_scr[2 * hh] = s0f_ref[0, hh]
            s_scr[2 * hh + 1] = s0b_ref[0, hh]

    def body(i, carry):
        tf = pl.multiple_of(i * c, c)
        tbk = pl.multiple_of((chunks_blk - 1 - i) * c, c)
        chains = [(pf_ref, of_ref, tf, hh, 2 * hh) for hh in range(heads_blk)]
        chains += [(pb_ref, ob_ref, tbk, hh, 2 * hh + 1) for hh in range(heads_blk)]
        states = [s_scr[ci] for _, _, _, _, ci in chains]
        wss = []
        for (p_ref, _, t0, hh, _), state in zip(chains, states):
            base = hh * GDN_SLOT
            wq = jnp.concatenate([p_ref[0, pl.ds(t0, c), base + hd:base + 2 * hd],
                                  p_ref[0, pl.ds(t0, c), base + 2 * hd:base + 3 * hd]], axis=0)
            wss.append(_dot(wq, state.astype(BF16)))
        v_news = []
        for (p_ref, _, t0, hh, _), ws in zip(chains, wss):
            base = hh * GDN_SLOT
            v_news.append((p_ref[0, pl.ds(t0, c), base:base + hd].astype(F32) - ws[:c]).astype(BF16))
        a128s = [p_ref[0, pl.ds(t0, c), hh * GDN_SLOT + 4 * hd:hh * GDN_SLOT + 5 * hd]
                 for p_ref, _, t0, hh, _ in chains]
        outs = [ws[c:] + _dot(a128[:, :c], v_new) for ws, a128, v_new in zip(wss, a128s, v_news)]
        upds = [_dot_tn(p_ref[0, pl.ds(t0, c), hh * GDN_SLOT + 3 * hd:hh * GDN_SLOT + 4 * hd], v_new)
                for (p_ref, _, t0, hh, _), v_new in zip(chains, v_news)]
        for (_, o_ref, t0, hh, ci), state, a128, o, upd in zip(chains, states, a128s, outs, upds):
            egl = a128[0:1, c:c + 1].astype(F32) + a128[0:1, c + c // 2:c + c // 2 + 1].astype(F32)
            s_scr[ci] = state * egl + upd
            o_ref[0, pl.ds(t0, c), hh * hd:(hh + 1) * hd] = o
        return carry

    lax.fori_loop(0, chunks_blk, body, 0)

    @pl.when(t == pl.num_programs(2) - 1)
    def _():
        for hh in range(heads_blk):
            sf_ref[0, hh] = s_scr[2 * hh]
            sb_ref[0, hh] = s_scr[2 * hh + 1]


def _gdn_scan(pf, pb, s0f, s0b):
    b, n, _ = pf.shape
    hs = 8
    nhs = GDN_HEADS // hs
    tb = min(n, 512)
    nt = n // tb
    cb = tb // GDN_CHUNK
    st_spec = pl.BlockSpec((1, hs, GDN_HEAD_DIM, GDN_HEAD_DIM), lambda bi, h, t: (bi, h, 0, 0))
    return pl.pallas_call(
        functools.partial(_gdn_scan_kernel, heads_blk=hs, chunks_blk=cb),
        grid=(b, nhs, nt),
        in_specs=[pl.BlockSpec((1, tb, hs * GDN_SLOT), lambda bi, h, t: (bi, t, h)),
                  pl.BlockSpec((1, tb, hs * GDN_SLOT), lambda bi, h, t: (bi, nt - 1 - t, h)),
                  st_spec, st_spec],
        out_specs=[pl.BlockSpec((1, tb, hs * GDN_HEAD_DIM), lambda bi, h, t: (bi, t, h)),
                   pl.BlockSpec((1, tb, hs * GDN_HEAD_DIM), lambda bi, h, t: (bi, nt - 1 - t, h)),
                   st_spec, st_spec],
        out_shape=[jax.ShapeDtypeStruct((b, n, GDN_KEY), F32)] * 2
        + [jax.ShapeDtypeStruct((b, GDN_HEADS, GDN_HEAD_DIM, GDN_HEAD_DIM), F32)] * 2,
        scratch_shapes=[pltpu.VMEM((2 * hs, GDN_HEAD_DIM, GDN_HEAD_DIM), F32)],
        compiler_params=_cparams(3, VMEM_LIMIT),
        name="gdn_scan",
    )(pf, pb, s0f, s0b)


def _gdn_gate_tables(gf, gb, beta, b, n):
    nc = n // GDN_CHUNK
    ng = n // GDN_GROUP

    def split(x):
        return x.reshape(2, GDN_HEADS, b, nc, GDN_CHUNK)

    def group(x):
        return x.reshape(GDN_HEADS, b, ng, GDN_GROUP)

    gf5, gb5, be5 = split(gf), split(gb), split(beta)
    gcf, gcb = group(gf5[0]), group(gb5[1])
    glf = group(jnp.broadcast_to(gf5[0][..., GDN_CHUNK - 1:], gf5[0].shape))
    glb = group(jnp.broadcast_to(gb5[1][..., :1], gb5[1].shape))
    zero = jnp.zeros_like(gcf)
    col_list = [gcf, group(be5[0]), gcb, group(be5[1]), glf, glb, zero, zero]
    cols_t = jnp.stack(col_list, axis=-2).transpose(1, 0, 2, 3, 4)
    cols = jnp.swapaxes(cols_t, -1, -2)
    row_list = [gcf, gcb, zero, zero, zero, zero, zero, zero]
    rows = jnp.stack(row_list, axis=-2).transpose(1, 0, 2, 3, 4)
    return cols, rows


def _gdn_out_kernel(ofp_ref, obp_ref, ofs_ref, obs_ref, z_ref, g_ref, y_ref, o_scr, *, n_prompt_tiles):
    i = pl.program_id(0)

    @pl.when(i < n_prompt_tiles)
    def _():
        o_scr[...] = ofp_ref[...] + obp_ref[...]

    @pl.when(i >= n_prompt_tiles)
    def _():
        o_scr[...] = ofs_ref[...] + obs_ref[...]

    o = o_scr[...]
    z = z_ref[...]
    gate = z * jax.nn.sigmoid(z)
    outs = []
    for h in range(GDN_HEADS):
        sl = slice(h * GDN_HEAD_DIM, (h + 1) * GDN_HEAD_DIM)
        oh = o[:, sl]
        r = lax.rsqrt(jnp.mean(oh * oh, axis=-1, keepdims=True) + EPS)
        outs.append(oh * r * g_ref[...] * gate[:, sl])
    y_ref[...] = jnp.concatenate(outs, axis=1).astype(BF16)


def _gdn_out(o_prompt, o_sample, proj, norm_g):
    t = proj.shape[0]
    npt = o_prompt[0].shape[0] // TM
    z_block = GDN_QKV // GDN_KEY
    p_spec = pl.BlockSpec((TM, GDN_KEY), lambda i: (jnp.minimum(i, npt - 1), 0))
    s_spec = pl.BlockSpec((TM, GDN_KEY), lambda i: (jnp.maximum(i - npt, 0), 0))
    return pl.pallas_call(
        functools.partial(_gdn_out_kernel, n_prompt_tiles=npt),
        grid=(t // TM,),
        in_specs=[p_spec, p_spec, s_spec, s_spec,
                  pl.BlockSpec((TM, GDN_KEY), lambda i: (i, z_block)),
                  pl.BlockSpec((1, GDN_HEAD_DIM), lambda i: (0, 0))],
        out_specs=pl.BlockSpec((TM, GDN_KEY), lambda i: (i, 0)),
        out_shape=jax.ShapeDtypeStruct((t, GDN_KEY), BF16),
        scratch_shapes=[pltpu.VMEM((TM, GDN_KEY), F32)],
        compiler_params=_cparams(1, VMEM_LIMIT),
        name="gdn_out",
    )(*o_prompt, *o_sample, proj, norm_g.reshape(1, -1))


def _store_token_tiles(ref, x):
    for s in range(TOKEN_TILE_ROWS):
        ref[pl.ds(s, x.shape[0], stride=TOKEN_TILE_ROWS), :] = x[:, s * LANE:(s + 1) * LANE]


def _load_token_tiles(ref, rows):
    return [ref[pl.ds(s, rows, stride=TOKEN_TILE_ROWS), :] for s in range(TOKEN_TILE_ROWS)]


def _router_kernel(x_ref, g_ref, mod_ref, wr_ref, br_ref, h_ref, idx_ref, w_ref, cnt_ref):
    h = _modulate(x_ref[...], g_ref[...], mod_ref, 3, 4)
    _store_token_tiles(h_ref, h)
    logits = lax.dot_general(wr_ref[...], h, (((1,), (1,)), ((), ())),
                             preferred_element_type=F32, precision=HIGHEST) + br_ref[...]
    ne, tm = logits.shape
    iota = lax.broadcasted_iota(jnp.int32, (ne, tm), 0)
    vals, idxs = [], []
    cur = logits
    for _ in range(TOP_K):
        m = jnp.max(cur, axis=0, keepdims=True)
        idx = jnp.min(jnp.where(cur == m, iota, ne), axis=0, keepdims=True)
        vals.append(m)
        idxs.append(idx)
        cur = jnp.where(iota == idx, -jnp.inf, cur)
    es = [jnp.exp(v - vals[0]) for v in vals]
    tot = es[0] + es[1] + es[2] + es[3]
    cnt = jnp.zeros((ne, 1), jnp.int32)
    for kk in range(TOP_K):
        idx_ref[kk:kk + 1, :] = idxs[kk]
        w_ref[kk:kk + 1, :] = es[kk] / tot
        cnt = cnt + jnp.sum((iota == idxs[kk]).astype(jnp.int32), axis=1, keepdims=True)

    @pl.when(pl.program_id(0) == 0)
    def _():
        cnt_ref[...] = jnp.zeros(cnt_ref.shape, jnp.int32)

    cnt_ref[...] += cnt


def _router(x, g, mod, w_router_t, b_router):
    t, d = x.shape
    return pl.pallas_call(
        _router_kernel,
        grid=(t // TM,),
        in_specs=[pl.BlockSpec((TM, d), lambda i: (i, 0)),
                  pl.BlockSpec((1, d), lambda i: (0, 0)),
                  pl.BlockSpec((1, 6, 1, d), lambda i: (_group_of_tile(i, TM), 0, 0, 0)),
                  pl.BlockSpec((N_EXPERTS, d), lambda i: (0, 0)),
                  pl.BlockSpec((N_EXPERTS, 1), lambda i: (0, 0))],
        out_specs=[pl.BlockSpec((TM * TOKEN_TILE_ROWS, LANE), lambda i: (i, 0)),
                   pl.BlockSpec((TOP_K, TM), lambda i: (0, i)),
                   pl.BlockSpec((TOP_K, TM), lambda i: (0, i)),
                   pl.BlockSpec((N_EXPERTS, 1), lambda i: (0, 0))],
        out_shape=[jax.ShapeDtypeStruct((t * TOKEN_TILE_ROWS, LANE), F32),
                   jax.ShapeDtypeStruct((TOP_K, t), jnp.int32),
                   jax.ShapeDtypeStruct((TOP_K, t), F32),
                   jax.ShapeDtypeStruct((N_EXPERTS, 1), jnp.int32)],
        compiler_params=_cparams(1, VMEM_LIMIT),
        name="moe_router",
    )(x, g, mod, w_router_t, b_router.reshape(N_EXPERTS, 1))


def _tile_rows(i):
    return pl.ds(pl.multiple_of(i * TOKEN_TILE_ROWS, TOKEN_TILE_ROWS), TOKEN_TILE_ROWS)


def _token_gather_kernel(src_ref, h_hbm, o_ref, sem):
    def issue(g, carry):
        for u in range(ROW_DMA_UNROLL):
            r = g * ROW_DMA_UNROLL + u
            pltpu.make_async_copy(h_hbm.at[_tile_rows(src_ref[0, 0, r])], o_ref.at[_tile_rows(r)],
                                  sem).start(priority=u % 2)
        return carry

    lax.fori_loop(0, ROW_DMA_BLOCK // ROW_DMA_UNROLL, issue, 0)
    pltpu.make_async_copy(h_hbm.at[pl.ds(0, o_ref.shape[0])], o_ref, sem).wait()


def _token_gather(src, h):
    p = src.shape[0]
    nb = p // ROW_DMA_BLOCK
    blk = ROW_DMA_BLOCK * TOKEN_TILE_ROWS
    return pl.pallas_call(
        _token_gather_kernel,
        grid=(nb,),
        in_specs=[pl.BlockSpec((1, 1, ROW_DMA_BLOCK), lambda i: (i, 0, 0), memory_space=pltpu.SMEM),
                  pl.BlockSpec(memory_space=pl.ANY)],
        out_specs=pl.BlockSpec((blk, LANE), lambda i: (i, 0)),
        out_shape=jax.ShapeDtypeStruct((p * TOKEN_TILE_ROWS, LANE), h.dtype),
        scratch_shapes=[pltpu.SemaphoreType.DMA(())],
        compiler_params=_cparams(1),
        name="moe_token_gather",
    )(src.reshape(nb, 1, ROW_DMA_BLOCK), h)


def _token_scatter_kernel(dst_ref, y_ref, o_hbm, sem):
    def issue(g, carry):
        for u in range(ROW_DMA_UNROLL):
            r = g * ROW_DMA_UNROLL + u
            pltpu.make_async_copy(y_ref.at[_tile_rows(r)], o_hbm.at[_tile_rows(dst_ref[0, 0, r])],
                                  sem).start(priority=u % 2)
        return carry

    lax.fori_loop(0, ROW_DMA_BLOCK // ROW_DMA_UNROLL, issue, 0)
    pltpu.make_async_copy(y_ref, o_hbm.at[pl.ds(0, y_ref.shape[0])], sem).wait()


def _token_scatter(dst, y, n_out_tokens):
    p = dst.shape[0]
    nb = p // ROW_DMA_BLOCK
    blk = ROW_DMA_BLOCK * TOKEN_TILE_ROWS
    return pl.pallas_call(
        _token_scatter_kernel,
        grid=(nb,),
        in_specs=[pl.BlockSpec((1, 1, ROW_DMA_BLOCK), lambda i: (i, 0, 0), memory_space=pltpu.SMEM),
                  pl.BlockSpec((blk, LANE), lambda i: (i, 0))],
        out_specs=pl.BlockSpec(memory_space=pl.ANY),
        out_shape=jax.ShapeDtypeStruct((n_out_tokens * TOKEN_TILE_ROWS, LANE), y.dtype),
        scratch_shapes=[pltpu.SemaphoreType.DMA(())],
        compiler_params=_cparams(1),
        name="moe_token_scatter",
    )(dst.reshape(nb, 1, ROW_DMA_BLOCK), y)


ITEM_FIRST, ITEM_LAST, ITEM_VALID = 1, 2, 4


def _experts_kernel(tile_ref, exp_ref, lo_ref, hi_ref, flag_ref, x_ref, wgu_ref, bgu_ref, wd_ref,
                    bd_ref, wrow_ref, y_ref, wgu_bf, wd_bf, x_bf, acc):
    j = pl.program_id(0)
    e = exp_ref[j]
    prev = exp_ref[jnp.maximum(j - 1, 0)]
    flags = flag_ref[j]
    first = (flags & ITEM_FIRST) != 0

    @pl.when((j == 0) | (e != prev))
    def _():
        wgu_bf[...] = wgu_ref[0, 0].astype(BF16)
        wd_bf[...] = wd_ref[0, 0].astype(BF16)

    @pl.when(first)
    def _():
        for s, xs in enumerate(_load_token_tiles(x_ref, MOE_TM)):
            x_bf[:, s * LANE:(s + 1) * LANE] = xs.astype(BF16)

    @pl.when((flags & ITEM_VALID) != 0)
    def _():
        gu = _dot(x_bf[...], wgu_bf[...]) + bgu_ref[0]
        glu = jnp.minimum(gu[:, :D_EXPERT], SWIGLU_LIMIT)
        lin = jnp.clip(gu[:, D_EXPERT:], -SWIGLU_LIMIT, SWIGLU_LIMIT)
        act = (lin + 1.0) * glu * jax.nn.sigmoid(SWIGLU_ALPHA * glu)
        y = (_dot(act.astype(BF16), wd_bf[...]) + bd_ref[0]) * wrow_ref[:, 0:1]
        row = lax.broadcasted_iota(jnp.int32, (MOE_TM, 1), 0)
        mine = (row >= lo_ref[j]) & (row < hi_ref[j])

        @pl.when(first)
        def _():
            acc[...] = jnp.where(mine, y, 0.0)

        @pl.when(jnp.logical_not(first))
        def _():
            acc[...] = jnp.where(mine, y, acc[...])

    @pl.when((flags & ITEM_LAST) != 0)
    def _():
        _store_token_tiles(y_ref, acc[...])


def _experts(plan, xs, layer, w_gate_up, b_gate_up, w_down, b_down, w_sorted):
    d = D_MODEL
    blk = MOE_TM * TOKEN_TILE_ROWS

    def tile_map(j, tile, exp, lo, hi, flag):
        return (tile[j], 0)

    def exp_map(j, tile, exp, lo, hi, flag):
        return (exp[j], 0, 0)

    def w_map(j, tile, exp, lo, hi, flag):
        return (layer, exp[j], 0, 0)

    grid_spec = pltpu.PrefetchScalarGridSpec(
        num_scalar_prefetch=5,
        grid=(MOE_ITEMS,),
        in_specs=[pl.BlockSpec((blk, LANE), tile_map),
                  pl.BlockSpec((1, 1, d, 2 * D_EXPERT), w_map),
                  pl.BlockSpec((1, 1, 2 * D_EXPERT), exp_map),
                  pl.BlockSpec((1, 1, D_EXPERT, d), w_map),
                  pl.BlockSpec((1, 1, d), exp_map),
                  pl.BlockSpec((MOE_TM, LANE), tile_map)],
        out_specs=pl.BlockSpec((blk, LANE), tile_map),
        scratch_shapes=[pltpu.VMEM((d, 2 * D_EXPERT), BF16), pltpu.VMEM((D_EXPERT, d), BF16),
                        pltpu.VMEM((MOE_TM, d), BF16), pltpu.VMEM((MOE_TM, d), F32)],
    )
    return pl.pallas_call(
        _experts_kernel,
        grid_spec=grid_spec,
        out_shape=jax.ShapeDtypeStruct(xs.shape, F32),
        compiler_params=_cparams(1, VMEM_LIMIT),
        name="moe_experts",
    )(*plan, xs, w_gate_up, b_gate_up.reshape(N_EXPERTS, 1, -1), w_down,
      b_down.reshape(N_EXPERTS, 1, -1), w_sorted)


def _moe_sum_kernel(x_ref, gate_ref, y0, y1, y2, y3, o_ref, ysum):
    ysum[...] = y0[...] + y1[...] + y2[...] + y3[...]
    gate = gate_ref[0, 0]
    for s, ys in enumerate(_load_token_tiles(ysum, TM)):
        cols = slice(s * LANE, (s + 1) * LANE)
        o_ref[:, cols] = x_ref[:, cols] + gate[:, cols] * ys


def _moe_sum(x, mod, y_tok):
    t, d = x.shape
    nt = t // TM
    blk = TM * TOKEN_TILE_ROWS
    y_specs = [pl.BlockSpec((blk, LANE), lambda i, k=k: (k * nt + i, 0)) for k in range(TOP_K)]
    return pl.pallas_call(
        _moe_sum_kernel,
        grid=(nt,),
        in_specs=[pl.BlockSpec((TM, d), lambda i: (i, 0)),
                  pl.BlockSpec((1, 1, 1, d), lambda i: (_group_of_tile(i, TM), 5, 0, 0))] + y_specs,
        out_specs=pl.BlockSpec((TM, d), lambda i: (i, 0)),
        out_shape=jax.ShapeDtypeStruct((t, d), F32),
        scratch_shapes=[pltpu.VMEM((blk, LANE), F32)],
        compiler_params=_cparams(1, VMEM_LIMIT),
        name="moe_sum",
    )(x, mod, y_tok, y_tok, y_tok, y_tok)


def _pick(onehot, vec):
    return jnp.sum(jnp.where(onehot, vec[None, :], 0), axis=1)


def _moe_plan(counts):
    ends = jnp.cumsum(counts)
    starts = ends - counts
    first_tile = starts // MOE_TM
    n_items_e = jnp.where(counts > 0, (ends - 1) // MOE_TM - first_tile + 1, 0)
    item_ends = jnp.cumsum(n_items_e)
    item_starts = item_ends - n_items_e
    n_items = item_ends[-1]
    j = jnp.arange(MOE_ITEMS, dtype=jnp.int32)
    jc = jnp.clip(j, 0, jnp.maximum(n_items - 1, 0))
    e_j = jnp.minimum(jnp.sum(item_ends[None, :] <= jc[:, None], axis=1), N_EXPERTS - 1).astype(jnp.int32)
    onehot = e_j[:, None] == jnp.arange(N_EXPERTS, dtype=jnp.int32)[None, :]
    tile_j = _pick(onehot, first_tile) + (jc - _pick(onehot, item_starts))
    lo = jnp.maximum(_pick(onehot, starts), tile_j * MOE_TM) - tile_j * MOE_TM
    hi = jnp.minimum(_pick(onehot, ends), (tile_j + 1) * MOE_TM) - tile_j * MOE_TM
    valid = j < n_items
    prev_tile = jnp.concatenate([jnp.full((1,), -1, jnp.int32), tile_j[:-1]])
    next_tile = jnp.concatenate([tile_j[1:], jnp.full((1,), -1, jnp.int32)])
    first = valid & (tile_j != prev_tile)
    last = valid & ((tile_j != next_tile) | (j == n_items - 1))
    flags = first * ITEM_FIRST + last * ITEM_LAST + valid * ITEM_VALID
    i32 = lambda v: v.astype(jnp.int32)
    return i32(tile_j), e_j, i32(lo), i32(jnp.where(valid, hi, lo)), i32(flags)


def _moe_layer(x, g_ffn, mod, layer, w_router, b_router, w_gate_up, b_gate_up, w_down, b_down):
    t = x.shape[0]
    a = TOP_K * t
    h, idx_t, w_t, counts = _router(x, g_ffn.reshape(1, -1), mod, w_router.T, b_router)
    _, order, w_sorted = lax.sort((idx_t.reshape(a), jnp.arange(a, dtype=jnp.int32), w_t.reshape(a)),
                                  num_keys=1, is_stable=False)
    plan = _moe_plan(counts.reshape(N_EXPERTS))
    xs = _token_gather(order % t, h)
    ys = _experts(plan, xs, layer, w_gate_up, b_gate_up, w_down, b_down, jnp.broadcast_to(w_sorted[:, None], (a, LANE)))
    y_tok = _token_scatter(order, ys, a)
    return _moe_sum(x, mod, y_tok)


def _na_layer(x, g_mix, mod, cache_k, cache_v, w_in, q_g, k_g, rpb, w_out):
    hd = NA_HEADS * NA_HEAD_DIM
    gains = jnp.concatenate([jnp.tile(q_g * NA_HEAD_DIM ** -0.5, NA_HEADS), jnp.tile(k_g, NA_HEADS),
                             jnp.ones((hd,), F32)]).reshape(1, -1)
    tn = 512
    qkv = _modmm(x, g_mix.reshape(1, -1), mod, w_in.astype(BF16), shift_idx=0, scale_idx=1, tn=tn,
                 norm=(2 * hd // tn, gains, NA_HEAD_DIM), name="na_qkv")
    qkv_by_seq = qkv.reshape(TOKENS // SEQ, SEQ, 3 * hd)
    qkv_by_dec = qkv.reshape(TOKENS // DEC_SEQ, DEC_SEQ, 3 * hd)
    n_pairs = NA_HEADS // 2
    o_p = _pair_attn(qkv_by_seq, qkv_by_seq, qkv_by_seq, b=BATCH, q_col=0, k_col=n_pairs,
                     v_col=2 * n_pairs, slot=NA_HEAD_DIM, tq=SEQ, n_pairs=n_pairs, pairs_blk=n_pairs,
                     name="na_context_attn")
    bias = _na_bias_table(rpb)
    o_s = _na_latent_attn(qkv_by_dec, cache_k.reshape(DEC_BATCH, PAST_LEN, hd),
                          cache_v.reshape(DEC_BATCH, PAST_LEN, hd), bias, PROMPT_TOK // DEC_SEQ)
    o = jnp.concatenate([o_p.reshape(PROMPT_TOK, hd), o_s.reshape(SAMPLE_TOK, hd)], axis=0)
    x = _mm_res(o, w_out.astype(BF16), x, mod, 2, name="na_out")
    new_k = qkv[:PROMPT_TOK, hd:2 * hd].reshape(BATCH, SEQ, NA_HEADS, NA_HEAD_DIM)
    new_v = qkv[:PROMPT_TOK, 2 * hd:].reshape(BATCH, SEQ, NA_HEADS, NA_HEAD_DIM)
    return x, new_k, new_v


def _mla_weights(w_q_up, w_kv_up, q_g, k_g):
    wq = w_q_up.reshape(MLA_Q_LORA, MLA_HEADS, MLA_QK_DIM)
    wq = jnp.pad(wq, ((0, 0), (0, 0), (0, LANE - MLA_QK_DIM))).reshape(MLA_Q_LORA, MLA_HEADS * LANE)
    kv = w_kv_up.reshape(MLA_KV_LORA, MLA_HEADS, MLA_NOPE_DIM + MLA_V_DIM)
    wk_top = jnp.pad(kv[:, :, :MLA_NOPE_DIM], ((0, 0), (0, 0), (0, LANE - MLA_NOPE_DIM)))
    eye = jnp.eye(MLA_ROPE_DIM, dtype=F32)
    pe_rows = jnp.pad(eye, ((0, 0), (MLA_NOPE_DIM, LANE - MLA_QK_DIM)))
    pe_rows = jnp.broadcast_to(pe_rows[:, None, :], (MLA_ROPE_DIM, MLA_HEADS, LANE))
    k_in = 2 * LANE
    wk = jnp.concatenate([wk_top, pe_rows,
                          jnp.zeros((k_in - MLA_KV_LORA - MLA_ROPE_DIM, MLA_HEADS, LANE), F32)], axis=0)
    wk = wk.reshape(k_in, MLA_HEADS * LANE)
    wv = jnp.pad(kv[:, :, MLA_NOPE_DIM:].reshape(MLA_KV_LORA, MLA_HEADS * MLA_V_DIM),
                 ((0, k_in - MLA_KV_LORA), (0, 0)))
    wkv = jnp.concatenate([wk, wv], axis=1)
    pad_g = LANE - MLA_QK_DIM
    qgain = jnp.tile(jnp.pad(q_g * MLA_QK_DIM ** -0.5, (0, pad_g)), MLA_HEADS).reshape(1, -1)
    kgain = jnp.tile(jnp.pad(k_g, (0, pad_g)), MLA_HEADS)
    kvgain = jnp.concatenate([kgain, jnp.ones((MLA_HEADS * MLA_V_DIM,), F32)]).reshape(1, -1)
    return wq.astype(BF16), wkv.astype(BF16), qgain, kvgain


def _mla_layer(x, g_mix, mod, cache_ckv, cache_kpe, w_in, q_a_g, kv_a_g, w_q_up, w_kv_up, q_g, k_g,
               w_out):
    n_lat = MLA_Q_LORA + MLA_KV_LORA + MLA_ROPE_DIM
    w_in_p = jnp.pad(w_in, ((0, 0), (0, 512 - n_lat))).astype(BF16)
    proj = _modmm(x, g_mix.reshape(1, -1), mod, w_in_p, shift_idx=0, scale_idx=1, tn=512,
                  name="mla_in")
    cq, ckv, kin = _mla_latents(proj, q_a_g, kv_a_g)
    wq, wkv, qgain, kvgain = _mla_weights(w_q_up, w_kv_up, q_g, k_g)
    n_ktiles = MLA_HEADS * LANE // 256
    npt = PROMPT_TOK // TM
    spb = DEC_SEQ // TM
    q_tab = _rope_tables(DEC_SEQ, n_identity_rows_first=TM)
    q = _mmn(cq, wq, qgain, n_norm_tiles=n_ktiles, count=MLA_QK_DIM,
             rope=(q_tab, lambda i: jnp.where(i < npt, 0, 1 + (i - npt) % spb)), name="mla_q")
    kv_p = _mmn(kin[:PROMPT_TOK], wkv, kvgain, n_norm_tiles=n_ktiles, count=MLA_QK_DIM,
                name="mla_kv_prompt")
    kin_ctx = jnp.concatenate(
        [cache_ckv, cache_kpe, jnp.zeros((DEC_BATCH, PAST_LEN, 2 * LANE - MLA_KV_LORA - MLA_ROPE_DIM),
                                         F32)], axis=-1).astype(BF16)
    kin_s = jnp.concatenate([kin[PROMPT_TOK:].reshape(DEC_BATCH, DEC_SEQ, 2 * LANE), kin_ctx], axis=1)
    n_keys = DEC_SEQ + PAST_LEN
    kpb = n_keys // TM
    k_tab = _rope_tables(DEC_SEQ, n_identity_rows_last=PAST_LEN)
    kv_s = _mmn(kin_s.reshape(DEC_BATCH * n_keys, 2 * LANE), wkv, kvgain, n_norm_tiles=n_ktiles,
                count=MLA_QK_DIM, rope=(k_tab, lambda i: i % kpb), name="mla_kv_sample")
    hw = MLA_HEADS * LANE
    n_pairs = MLA_HEADS // 2
    q_by_seq = q.reshape(TOKENS // SEQ, SEQ, hw)
    q_by_dec = q.reshape(TOKENS // DEC_SEQ, DEC_SEQ, hw)
    kv_p3 = kv_p.reshape(BATCH, SEQ, -1)
    kv_s3 = kv_s.reshape(DEC_BATCH, n_keys, -1)
    o_p = _pair_attn(q_by_seq, kv_p3, kv_p3, b=BATCH, q_col=0, k_col=0, v_col=hw // LANE, slot=LANE,
                     tq=SEQ, n_pairs=n_pairs, pairs_blk=n_pairs, name="mla_context_attn")
    o_s = _pair_attn(q_by_dec, kv_s3, kv_s3, b=DEC_BATCH, q_batch0=PROMPT_TOK // DEC_SEQ, q_col=0,
                     k_col=0, v_col=hw // LANE, slot=LANE, tq=256, n_pairs=n_pairs, pairs_blk=2,
                     name="mla_latent_attn")
    ov = MLA_HEADS * MLA_V_DIM
    o = jnp.concatenate([o_p.reshape(PROMPT_TOK, ov), o_s.reshape(SAMPLE_TOK, ov)], axis=0)
    x = _mm_res(o, w_out.astype(BF16), x, mod, 2, name="mla_out")
    new_ckv = ckv[:PROMPT_TOK].reshape(BATCH, SEQ, MLA_KV_LORA)
    new_kpe = proj[:PROMPT_TOK, MLA_Q_LORA + MLA_KV_LORA:n_lat].reshape(BATCH, SEQ, MLA_ROPE_DIM)
    return x, new_ckv, new_kpe


def _gdn_layer(x, g_mix, mod, s_fwd, s_bwd, w_in, conv_w, a_log, dt_bias, norm_g, w_out):
    n_main = GDN_QKV + GDN_KEY
    gm = g_mix.reshape(1, -1)
    proj = _modmm(x, gm, mod, w_in[:, :n_main].astype(BF16), shift_idx=0, scale_idx=1, tn=512,
                  name="gdn_in")
    w_ab = jnp.pad(w_in[:, n_main:], ((0, 0), (0, LANE - 4 * GDN_HEADS))).astype(BF16)
    ab = _modmm(x, gm, mod, w_ab, shift_idx=0, scale_idx=1, tn=LANE, name="gdn_in_ab")
    gf, gb, beta = _gdn_gates(ab[:, :4 * GDN_HEADS].T, a_log, dt_bias)
    outs = []
    states = []
    for lo, hi, b, n, s0f, s0b in (
            (0, PROMPT_TOK, BATCH, SEQ, None, None),
            (PROMPT_TOK, TOKENS, DEC_BATCH, DEC_SEQ, s_fwd, s_bwd)):
        if s0f is None:
            s0f = jnp.zeros((b, GDN_HEADS, GDN_HEAD_DIM, GDN_HEAD_DIM), F32)
            s0b = s0f
        qkv = _gdn_conv(proj.reshape(TOKENS // n, n, n_main), conv_w, b, lo // n)
        cols, rows = _gdn_gate_tables(gf[:, lo:hi], gb[:, lo:hi], beta[:, lo:hi], b, n)
        pf, pb = _gdn_prep(qkv, cols, rows)
        o_f, o_b, sf, sb = _gdn_scan(pf, pb, s0f, s0b)
        outs.append((o_f.reshape(b * n, GDN_KEY), o_b.reshape(b * n, GDN_KEY)))
        states.append((sf, sb))
    y = _gdn_out(outs[0], outs[1], proj, norm_g)
    x = _mm_res(y, w_out.astype(BF16), x, mod, 2, name="gdn_out_proj")
    return x, states[0][0], states[0][1]


def kernel(x_prompt, x_sample, cache_na_k, cache_na_v, cache_mla_ckv, cache_mla_kpe, state_gdn_fwd,
           state_gdn_bwd, c, c_ctx, ada_w, ada_b, norm_mix_g, norm_ffn_g, na_w_in, na_q_g, na_k_g,
           na_rpb, na_w_out, mla_w_in, mla_q_a_g, mla_kv_a_g, mla_w_q_up, mla_w_kv_up, mla_q_g,
           mla_k_g, mla_w_out, gdn_w_in, gdn_conv_w, gdn_a_log, gdn_dt_bias, gdn_norm_g, gdn_w_out,
           moe_w_router, moe_b_router, moe_w_gate_up, moe_b_gate_up, moe_w_down, moe_b_down):
    x = jnp.concatenate([x_prompt.reshape(PROMPT_TOK, D_MODEL), x_sample.reshape(SAMPLE_TOK, D_MODEL)],
                        axis=0)
    c_all = jnp.concatenate([c_ctx[None], c, jnp.zeros((16 - N_GROUPS, D_MODEL), F32)], axis=0)
    mods = _ada_mods(c_all, ada_w, ada_b)[:, :N_GROUPS].reshape(DEPTH, N_GROUPS, 6, 1, D_MODEL)
    new_na_k, new_na_v, new_ckv, new_kpe, new_sf, new_sb = [], [], [], [], [], []
    for i in range(DEPTH):
        kind, j = i % N_MIXERS, i // N_MIXERS
        mod = mods[i]
        if kind == 0:
            x, nk, nv = _na_layer(x, norm_mix_g[i], mod, cache_na_k[:, j], cache_na_v[:, j],
                                  na_w_in[j], na_q_g[j], na_k_g[j], na_rpb[j], na_w_out[j])
            new_na_k.append(nk)
            new_na_v.append(nv)
        elif kind == 1:
            x, ckv, kpe = _mla_layer(x, norm_mix_g[i], mod, cache_mla_ckv[:, j], cache_mla_kpe[:, j],
                                     mla_w_in[j], mla_q_a_g[j], mla_kv_a_g[j], mla_w_q_up[j],
                                     mla_w_kv_up[j], mla_q_g[j], mla_k_g[j], mla_w_out[j])
            new_ckv.append(ckv)
            new_kpe.append(kpe)
        else:
            x, sf, sb = _gdn_layer(x, norm_mix_g[i], mod, state_gdn_fwd[:, j], state_gdn_bwd[:, j],
                                   gdn_w_in[j], gdn_conv_w[j], gdn_a_log[j], gdn_dt_bias[j],
                                   gdn_norm_g[j], gdn_w_out[j])
            new_sf.append(sf)
            new_sb.append(sb)
        x = _moe_layer(x, norm_ffn_g[i], mod, i, moe_w_router[i], moe_b_router[i], moe_w_gate_up,
                       moe_b_gate_up[i], moe_w_down, moe_b_down[i])
    return (x[:PROMPT_TOK].reshape(BATCH, SEQ, D_MODEL),
            x[PROMPT_TOK:].reshape(DEC_BATCH, DEC_SEQ, D_MODEL),
            jnp.stack(new_na_k, axis=1), jnp.stack(new_na_v, axis=1),
            jnp.stack(new_ckv, axis=1), jnp.stack(new_kpe, axis=1),
            jnp.stack(new_sf, axis=1), jnp.stack(new_sb, axis=1))
```

```python
import functools
import math

import numpy as np
import jax
import jax.numpy as jnp
from jax import lax
from jax.experimental import pallas as pl
from jax.experimental.pallas import tpu as pltpu

F32 = jnp.float32
BF16 = jnp.bfloat16
HIGHEST = lax.Precision.HIGHEST

D_MODEL = 1024
BATCH = 32
SEQ = 256
DEPTH = 4
DEC_BATCH = 8
DEC_SEQ = 2048
PAST_LEN = 512
GRID_W = 64
N_MIXERS = 3
NA_HEADS = 16
NA_HEAD_DIM = 64
NA_ROWS = 8
NA_COLS = 16
MLA_HEADS = 16
MLA_NOPE_DIM = 64
MLA_ROPE_DIM = 32
MLA_QK_DIM = MLA_NOPE_DIM + MLA_ROPE_DIM
MLA_V_DIM = 64
MLA_Q_LORA = 256
MLA_KV_LORA = 128
GDN_HEADS = 8
GDN_HEAD_DIM = 128
GDN_KEY = GDN_HEADS * GDN_HEAD_DIM
GDN_QKV = 3 * GDN_KEY
GDN_CONV_W = 5
GDN_CHUNK = 64
N_EXPERTS = 32
TOP_K = 4
D_EXPERT = D_MODEL
SWIGLU_ALPHA = 1.702
SWIGLU_LIMIT = 7.0
ROPE_BASE = 10000.0
EPS = 1e-6

PROMPT_TOK = BATCH * SEQ
SAMPLE_TOK = DEC_BATCH * DEC_SEQ
TOKENS = PROMPT_TOK + SAMPLE_TOK
N_GROUPS = 1 + DEC_BATCH

LANE = 128
TM = 512
MOE_TM = 512
MOE_ROWS = TOKENS * TOP_K
MOE_ITEMS = MOE_ROWS // MOE_TM + N_EXPERTS
TOKEN_TILE_ROWS = D_MODEL // LANE
ROW_DMA_UNROLL = 8
NORM_W = 256
VMEM_LIMIT = 56 * 1024 * 1024
NEG_BIG = -1e30


def _cparams(n_axes, vmem=None):
    return pltpu.CompilerParams(dimension_semantics=("arbitrary",) * n_axes,
                                vmem_limit_bytes=vmem)


def _dot(a, b):
    return jnp.dot(a, b, preferred_element_type=F32)


def _dot_nt(a, b):
    return lax.dot_general(a, b, (((1,), (1,)), ((), ())), preferred_element_type=F32)


def _dot_tn(a, b):
    return lax.dot_general(a, b, (((0,), (0,)), ((), ())), preferred_element_type=F32)


def _dot_hi(a, b):
    return jnp.dot(a, b, preferred_element_type=F32, precision=HIGHEST)


def _group_of_tile(i, tm):
    npt = PROMPT_TOK // tm
    spb = DEC_SEQ // tm
    return jnp.where(i < npt, 0, 1 + (i - npt) // spb)


def _block_diag_ones(n, group):
    idx = np.arange(n) // group
    return jnp.asarray((idx[:, None] == idx[None, :]).astype(np.float32), BF16)


def _ada_kernel(c_ref, w_ref, b_ref, o_ref):
    c = c_ref[...]
    s = c * jax.nn.sigmoid(c)
    o_ref[0] = _dot_hi(s, w_ref[0]) + b_ref[0]


def _ada_mods(c_all, ada_w, ada_b):
    tn = 512
    n = 6 * D_MODEL
    return pl.pallas_call(
        _ada_kernel,
        grid=(DEPTH, n // tn),
        in_specs=[pl.BlockSpec((16, D_MODEL), lambda l, j: (0, 0)),
                  pl.BlockSpec((1, D_MODEL, tn), lambda l, j: (l, 0, j)),
                  pl.BlockSpec((1, 1, tn), lambda l, j: (l, 0, j))],
        out_specs=pl.BlockSpec((1, 16, tn), lambda l, j: (l, 0, j)),
        out_shape=jax.ShapeDtypeStruct((DEPTH, 16, n), F32),
        compiler_params=_cparams(2),
        name="ada_mods",
    )(c_all, ada_w, ada_b.reshape(DEPTH, 1, n))


def _modulate(x, g, mod_ref, shift_idx, scale_idx):
    ms = jnp.mean(x * x, axis=-1, keepdims=True)
    y = x * lax.rsqrt(ms + EPS) * g
    return y * (1.0 + mod_ref[0, scale_idx]) + mod_ref[0, shift_idx]


def _group_norm(acc, bd, gain, inv_count):
    sq = (acc * acc).astype(BF16)
    ss = jnp.concatenate([_dot(sq[:, s * NORM_W:(s + 1) * NORM_W], bd)
                          for s in range(acc.shape[1] // NORM_W)], axis=1)
    return acc * lax.rsqrt(ss * inv_count + EPS) * gain


def _modmm_kernel(*refs, shift_idx, scale_idx, n_norm_tiles, inv_count):
    if n_norm_tiles:
        x_ref, g_ref, mod_ref, w_ref, gain_ref, bd_ref, o_ref, h_ref = refs
    else:
        x_ref, g_ref, mod_ref, w_ref, o_ref, h_ref = refs
    j = pl.program_id(1)

    @pl.when(j == 0)
    def _():
        h_ref[...] = _modulate(x_ref[...], g_ref[...], mod_ref, shift_idx, scale_idx).astype(BF16)

    acc = _dot(h_ref[...], w_ref[...])
    if n_norm_tiles:
        @pl.when(j < n_norm_tiles)
        def _():
            o_ref[...] = _group_norm(acc, bd_ref[...], gain_ref[...], inv_count).astype(o_ref.dtype)

        @pl.when(j >= n_norm_tiles)
        def _():
            o_ref[...] = acc.astype(o_ref.dtype)
    else:
        o_ref[...] = acc.astype(o_ref.dtype)


def _modmm(x, g, mod, w, *, shift_idx, scale_idx, tn, out_dtype=F32, norm=None, name="modmm"):
    t, d = x.shape
    n = w.shape[1]
    in_specs = [pl.BlockSpec((TM, d), lambda i, j: (i, 0)),
                pl.BlockSpec((1, d), lambda i, j: (0, 0)),
                pl.BlockSpec((1, 6, 1, d), lambda i, j: (_group_of_tile(i, TM), 0, 0, 0)),
                pl.BlockSpec((d, tn), lambda i, j: (0, j))]
    args = [x, g, mod, w]
    n_norm_tiles, inv_count = 0, 1.0
    if norm is not None:
        n_norm_tiles, gains, group = norm
        inv_count = 1.0 / group
        in_specs += [pl.BlockSpec((1, tn), lambda i, j: (0, j)),
                     pl.BlockSpec((NORM_W, NORM_W), lambda i, j: (0, 0))]
        args += [gains, _block_diag_ones(NORM_W, group)]
    return pl.pallas_call(
        functools.partial(_modmm_kernel, shift_idx=shift_idx, scale_idx=scale_idx,
                          n_norm_tiles=n_norm_tiles, inv_count=inv_count),
        grid=(t // TM, n // tn),
        in_specs=in_specs,
        out_specs=pl.BlockSpec((TM, tn), lambda i, j: (i, j)),
        out_shape=jax.ShapeDtypeStruct((t, n), out_dtype),
        scratch_shapes=[pltpu.VMEM((TM, d), BF16)],
        compiler_params=_cparams(2, VMEM_LIMIT),
        name=name,
    )(*args)


def _mm_res_kernel(a_ref, w_ref, res_ref, gate_ref, o_ref):
    o_ref[...] = res_ref[...] + gate_ref[0, 0] * _dot(a_ref[...], w_ref[...])


def _mm_res(a, w, res, mod, gate_idx, *, tn=1024, name="mm_res"):
    t, k = a.shape
    n = w.shape[1]
    return pl.pallas_call(
        _mm_res_kernel,
        grid=(t // TM, n // tn),
        in_specs=[pl.BlockSpec((TM, k), lambda i, j: (i, 0)),
                  pl.BlockSpec((k, tn), lambda i, j: (0, j)),
                  pl.BlockSpec((TM, tn), lambda i, j: (i, j)),
                  pl.BlockSpec((1, 1, 1, tn), lambda i, j: (_group_of_tile(i, TM), gate_idx, 0, j))],
        out_specs=pl.BlockSpec((TM, tn), lambda i, j: (i, j)),
        out_shape=jax.ShapeDtypeStruct((t, n), F32),
        compiler_params=_cparams(2, VMEM_LIMIT),
        name=name,
    )(a, w, res, mod)


def _rope_slots(y, c, s1, s2):
    outs = []
    for s in range(y.shape[1] // LANE):
        ys = y[:, s * LANE:(s + 1) * LANE]
        outs.append(ys * c + pltpu.roll(ys, LANE - MLA_ROPE_DIM // 2, 1) * s1
                    + pltpu.roll(ys, MLA_ROPE_DIM // 2, 1) * s2)
    return jnp.concatenate(outs, axis=1)


def _mmn_kernel(*refs, n_norm_tiles, inv_count, rope):
    if rope:
        a_ref, w_ref, gain_ref, bd_ref, c_ref, s1_ref, s2_ref, o_ref = refs
    else:
        a_ref, w_ref, gain_ref, bd_ref, o_ref = refs
    j = pl.program_id(1)
    acc = _dot(a_ref[...], w_ref[...])

    @pl.when(j < n_norm_tiles)
    def _():
        y = _group_norm(acc, bd_ref[...], gain_ref[...], inv_count)
        if rope:
            y = _rope_slots(y, c_ref[0], s1_ref[0], s2_ref[0])
        o_ref[...] = y.astype(o_ref.dtype)

    @pl.when(j >= n_norm_tiles)
    def _():
        o_ref[...] = acc.astype(o_ref.dtype)


def _mmn(a, w, gains, *, n_norm_tiles, count, rope=None, tn=256, name="mmn"):
    t, k = a.shape
    n = w.shape[1]
    in_specs = [pl.BlockSpec((TM, k), lambda i, j: (i, 0)),
                pl.BlockSpec((k, tn), lambda i, j: (0, j)),
                pl.BlockSpec((1, tn), lambda i, j: (0, j)),
                pl.BlockSpec((tn, tn), lambda i, j: (0, 0))]
    args = [a, w, gains, _block_diag_ones(tn, LANE)]
    if rope is not None:
        tables, pos_fn = rope
        for q in range(3):
            in_specs.append(pl.BlockSpec((1, TM, LANE), lambda i, j, q=q: (q, pos_fn(i), 0)))
            args.append(tables)
    return pl.pallas_call(
        functools.partial(_mmn_kernel, n_norm_tiles=n_norm_tiles, inv_count=1.0 / count,
                          rope=rope is not None),
        grid=(t // TM, n // tn),
        in_specs=in_specs,
        out_specs=pl.BlockSpec((TM, tn), lambda i, j: (i, j)),
        out_shape=jax.ShapeDtypeStruct((t, n), BF16),
        compiler_params=_cparams(2, VMEM_LIMIT),
        name=name,
    )(*args)


def _lane_half_mask(h, width=LANE):
    lane = lax.broadcasted_iota(jnp.int32, (1, width), 1)
    return (lane // (width // 2)) == h


def _pair_attn_kernel(q_ref, k_ref, v_ref, o_ref, *, slot, pairs_blk):
    q = q_ref[0]
    k = k_ref[0]
    v = v_ref[0].astype(BF16)
    masks = [_lane_half_mask(h) for h in range(2)]
    heads = [(pp, h) for pp in range(pairs_blk) for h in range(2)]
    if slot == LANE:
        ss = [_dot_nt(q[:, (2 * pp + h) * LANE:(2 * pp + h + 1) * LANE].astype(BF16),
                      k[:, (2 * pp + h) * LANE:(2 * pp + h + 1) * LANE].astype(BF16)) for pp, h in heads]
    else:
        kb = k.astype(BF16)
        ss = [_dot_nt(jnp.where(masks[h], q[:, pp * LANE:(pp + 1) * LANE], 0).astype(BF16),
                      kb[:, pp * LANE:(pp + 1) * LANE]) for pp, h in heads]
    ps = [jnp.exp(s - jnp.max(s, axis=-1, keepdims=True)) for s in ss]
    os_ = [_dot(p.astype(BF16), jnp.where(masks[h], v[:, pp * LANE:(pp + 1) * LANE], 0).astype(BF16))
           for p, (pp, h) in zip(ps, heads)]
    scaled = [o * (1.0 / jnp.sum(p, axis=-1, keepdims=True)) for o, p in zip(os_, ps)]
    for pp in range(pairs_blk):
        o_ref[0, :, pp * LANE:(pp + 1) * LANE] = (scaled[2 * pp] + scaled[2 * pp + 1]).astype(o_ref.dtype)


def _pair_attn(q_arr, k_arr, v_arr, *, b, q_batch0=0, kv_batch0=0, q_col, k_col, v_col, slot, tq,
               n_pairs, pairs_blk, name):
    nq = q_arr.shape[1]
    nk = k_arr.shape[1]
    qw = 2 * slot * pairs_blk
    vw = LANE * pairs_blk
    qc, kc, vc = q_col // pairs_blk, k_col // pairs_blk, v_col // pairs_blk
    return pl.pallas_call(
        functools.partial(_pair_attn_kernel, slot=slot, pairs_blk=pairs_blk),
        grid=(b, n_pairs // pairs_blk, nq // tq),
        in_specs=[pl.BlockSpec((1, tq, qw), lambda bi, hp, qi: (q_batch0 + bi, qi, qc + hp)),
                  pl.BlockSpec((1, nk, qw), lambda bi, hp, qi: (kv_batch0 + bi, 0, kc + hp)),
                  pl.BlockSpec((1, nk, vw), lambda bi, hp, qi: (kv_batch0 + bi, 0, vc + hp))],
        out_specs=pl.BlockSpec((1, tq, vw), lambda bi, hp, qi: (bi, qi, hp)),
        out_shape=jax.ShapeDtypeStruct((b, nq, n_pairs * LANE), BF16),
        compiler_params=_cparams(3, VMEM_LIMIT),
        name=name,
    )(q_arr, k_arr, v_arr)


NA_QBLOCK_ROWS = 8
NA_WIN = NA_ROWS * GRID_W


def _na_lat_kernel(q_ref, k_ref, v_ref, kc_ref, vc_ref, bias_ref, o_ref, kbf, vbf, *, rows):
    qi = pl.program_id(2)

    @pl.when(qi == 0)
    def _():
        kbf[...] = k_ref[0].astype(BF16)
        v = v_ref[0]
        for h in range(2):
            vbf[h] = jnp.where(_lane_half_mask(h), v, 0.0).astype(BF16)

    q = q_ref[0]
    kc = kc_ref[0].astype(BF16)
    vc = vc_ref[0]
    masks = [_lane_half_mask(h) for h in range(2)]
    qhs = [jnp.where(mask, q, 0.0).astype(BF16) for mask in masks]
    s_cs = [_dot_nt(qh, kc) for qh in qhs]
    units = []
    for rr in range(NA_QBLOCK_ROWS):
        r = qi * NA_QBLOCK_ROWS + rr
        r0 = jnp.clip(r - NA_ROWS // 2, 0, rows - NA_ROWS)
        start = pl.multiple_of(r0 * GRID_W, GRID_W)
        for h in range(2):
            units.append((h, slice(rr * GRID_W, (rr + 1) * GRID_W), start, r - r0))
    s_ls = [_dot_nt(qhs[h][sl], kbf[pl.ds(start, NA_WIN), :]) + bias_ref[pat, h]
            for h, sl, start, pat in units]
    m_cs = [jnp.max(s_c, axis=-1, keepdims=True) for s_c in s_cs]
    p_cs = [jnp.exp(s_c - m_c) for s_c, m_c in zip(s_cs, m_cs)]
    l_cs = [jnp.sum(p_c, axis=-1, keepdims=True) for p_c in p_cs]
    o_cs = [_dot(p_c.astype(BF16), jnp.where(mask, vc, 0.0).astype(BF16)) for p_c, mask in zip(p_cs, masks)]
    ms = [jnp.maximum(jnp.max(s_l, axis=-1, keepdims=True), m_cs[h][sl])
          for s_l, (h, sl, _, _) in zip(s_ls, units)]
    p_ls = [jnp.exp(s_l - m) for s_l, m in zip(s_ls, ms)]
    o_ls = [_dot(p_l.astype(BF16), vbf[h, pl.ds(start, NA_WIN), :])
            for p_l, (h, _, start, _) in zip(p_ls, units)]
    out = [jnp.zeros((GRID_W, LANE), F32) for _ in range(NA_QBLOCK_ROWS)]
    for (h, sl, _, _), m, p_l, o_l in zip(units, ms, p_ls, o_ls):
        a_c = jnp.exp(m_cs[h][sl] - m)
        l = jnp.sum(p_l, axis=-1, keepdims=True) + a_c * l_cs[h][sl]
        rr = sl.start // GRID_W
        out[rr] = out[rr] + (o_l + a_c * o_cs[h][sl]) * (1.0 / l)
    for rr in range(NA_QBLOCK_ROWS):
        o_ref[0, rr * GRID_W:(rr + 1) * GRID_W, :] = out[rr].astype(o_ref.dtype)


def _na_bias_table(rpb):
    col = np.arange(GRID_W)
    col_start = np.clip(col - NA_COLS // 2, 0, GRID_W - NA_COLS)
    kc = np.arange(GRID_W)
    inside = (kc[None, :] >= col_start[:, None]) & (kc[None, :] < col_start[:, None] + NA_COLS)
    pad = GRID_W - NA_COLS
    rpb_pad = jnp.pad(rpb, ((0, 0), (0, 0), (pad, pad + 1)))
    by_col = jnp.stack([rpb_pad[:, :, GRID_W - 1 - c:2 * GRID_W - 1 - c] for c in range(GRID_W)], axis=2)
    by_col = jnp.where(jnp.asarray(inside)[None, None], by_col, NEG_BIG)
    tab = jnp.stack([by_col[:, NA_ROWS - 1 - p:2 * NA_ROWS - 1 - p] for p in range(NA_ROWS)], axis=0)
    tab = tab.transpose(0, 1, 3, 2, 4)
    return tab.reshape(NA_ROWS, NA_HEADS, GRID_W, NA_WIN).astype(F32)


def _na_latent_attn(qkv, kctx, vctx, bias, batch0):
    n = qkv.shape[1]
    b = kctx.shape[0]
    rows = n // GRID_W
    n_pairs = NA_HEADS // 2
    tq = NA_QBLOCK_ROWS * GRID_W
    past = kctx.shape[1]
    return pl.pallas_call(
        functools.partial(_na_lat_kernel, rows=rows),
        grid=(b, n_pairs, n // tq),
        in_specs=[pl.BlockSpec((1, tq, LANE), lambda bi, hp, qi: (batch0 + bi, qi, hp)),
                  pl.BlockSpec((1, n, LANE), lambda bi, hp, qi: (batch0 + bi, 0, n_pairs + hp)),
                  pl.BlockSpec((1, n, LANE), lambda bi, hp, qi: (batch0 + bi, 0, 2 * n_pairs + hp)),
                  pl.BlockSpec((1, past, LANE), lambda bi, hp, qi: (bi, 0, hp)),
                  pl.BlockSpec((1, past, LANE), lambda bi, hp, qi: (bi, 0, hp)),
                  pl.BlockSpec((NA_ROWS, 2, GRID_W, NA_WIN), lambda bi, hp, qi: (0, hp, 0, 0))],
        out_specs=pl.BlockSpec((1, tq, LANE), lambda bi, hp, qi: (bi, qi, hp)),
        out_shape=jax.ShapeDtypeStruct((b, n, n_pairs * LANE), BF16),
        scratch_shapes=[pltpu.VMEM((n, LANE), BF16), pltpu.VMEM((2, n, LANE), BF16)],
        compiler_params=_cparams(3, VMEM_LIMIT),
        name="na_latent_attn",
    )(qkv, qkv, qkv, kctx, vctx, bias)


def _mla_lat_kernel(p_ref, qg_ref, kvg_ref, cq_ref, ckv_ref, kin_ref):
    p = p_ref[...]
    cq = p[:, :MLA_Q_LORA]
    cq = cq * lax.rsqrt(jnp.mean(cq * cq, axis=-1, keepdims=True) + EPS) * qg_ref[...]
    cq_ref[...] = cq.astype(BF16)
    ckv = p[:, MLA_Q_LORA:MLA_Q_LORA + MLA_KV_LORA]
    ckv = ckv * lax.rsqrt(jnp.mean(ckv * ckv, axis=-1, keepdims=True) + EPS) * kvg_ref[...]
    ckv_ref[...] = ckv
    tail = p[:, MLA_Q_LORA + MLA_KV_LORA:]
    kin_ref[...] = jnp.concatenate([ckv, tail], axis=1).astype(BF16)


def _mla_latents(proj, q_a_g, kv_a_g):
    t = proj.shape[0]
    w = proj.shape[1]
    return pl.pallas_call(
        _mla_lat_kernel,
        grid=(t // TM,),
        in_specs=[pl.BlockSpec((TM, w), lambda i: (i, 0)),
                  pl.BlockSpec((1, MLA_Q_LORA), lambda i: (0, 0)),
                  pl.BlockSpec((1, MLA_KV_LORA), lambda i: (0, 0))],
        out_specs=[pl.BlockSpec((TM, MLA_Q_LORA), lambda i: (i, 0)),
                   pl.BlockSpec((TM, MLA_KV_LORA), lambda i: (i, 0)),
                   pl.BlockSpec((TM, 2 * LANE), lambda i: (i, 0))],
        out_shape=[jax.ShapeDtypeStruct((t, MLA_Q_LORA), BF16),
                   jax.ShapeDtypeStruct((t, MLA_KV_LORA), F32),
                   jax.ShapeDtypeStruct((t, 2 * LANE), BF16)],
        compiler_params=_cparams(1),
        name="mla_latents",
    )(proj, q_a_g.reshape(1, -1), kv_a_g.reshape(1, -1))


def _rope_tables(n_pos, n_identity_rows_first=0, n_identity_rows_last=0):
    n_freq = MLA_ROPE_DIM // 4
    inv = ROPE_BASE ** (-jnp.arange(n_freq, dtype=F32) / n_freq)
    t = jnp.arange(n_pos)
    row = (t // GRID_W).astype(F32)
    col = (t % GRID_W).astype(F32)
    ang = jnp.concatenate([row[:, None] * inv, col[:, None] * inv], axis=-1)
    cos, sin = jnp.cos(ang), jnp.sin(ang)
    half = MLA_ROPE_DIM // 2
    ones = jnp.ones((n_pos, MLA_NOPE_DIM), F32)
    zeros = jnp.zeros((n_pos, MLA_NOPE_DIM), F32)
    pad1 = jnp.ones((n_pos, LANE - MLA_QK_DIM), F32)
    pad0 = jnp.zeros((n_pos, LANE - MLA_QK_DIM), F32)
    zh = jnp.zeros((n_pos, half), F32)
    c = jnp.concatenate([ones, cos, cos, pad1], axis=1)
    s1 = jnp.concatenate([zeros, -sin, zh, pad0], axis=1)
    s2 = jnp.concatenate([zeros, zh, sin, pad0], axis=1)
    tab = jnp.stack([c, s1, s2])

    def ident(nrows):
        return jnp.stack([jnp.ones((nrows, LANE), F32), jnp.zeros((nrows, LANE), F32),
                          jnp.zeros((nrows, LANE), F32)])

    parts = []
    if n_identity_rows_first:
        parts.append(ident(n_identity_rows_first))
    parts.append(tab)
    if n_identity_rows_last:
        parts.append(ident(n_identity_rows_last))
    return jnp.concatenate(parts, axis=1)


def _gdn_conv_kernel(x_ref, w_ref, o_ref, pad_ref, *, n):
    j = pl.program_id(1)
    halo = 8
    cw = x_ref.shape[2]
    pad_ref[0:halo, :] = jnp.zeros((halo, cw), F32)
    pad_ref[halo + n:2 * halo + n, :] = jnp.zeros((halo, cw), F32)
    pad_ref[halo:halo + n, :] = x_ref[0]
    acc = jnp.zeros((n, cw), F32)
    for t in range(GDN_CONV_W):
        off = halo - GDN_CONV_W // 2 + t
        acc = acc + pad_ref[off:off + n, :] * w_ref[t:t + 1, :]
    y = acc * jax.nn.sigmoid(acc)
    n_qk_blocks = 2 * GDN_KEY // cw
    n_q_blocks = GDN_KEY // cw
    outs = []
    for s in range(cw // LANE):
        ys = y[:, s * LANE:(s + 1) * LANE]
        ss = jnp.sum(ys * ys, axis=-1, keepdims=True)
        r = jnp.where(j < n_qk_blocks, lax.rsqrt(ss + EPS), 1.0)
        r = r * jnp.where(j < n_q_blocks, GDN_HEAD_DIM ** -0.5, 1.0)
        outs.append(ys * r)
    o_ref[0] = jnp.concatenate(outs, axis=1)


def _gdn_conv(proj3, conv_w, b, batch0):
    n = proj3.shape[1]
    cw = 512
    return pl.pallas_call(
        functools.partial(_gdn_conv_kernel, n=n),
        grid=(b, GDN_QKV // cw),
        in_specs=[pl.BlockSpec((1, n, cw), lambda bi, j: (batch0 + bi, 0, j)),
                  pl.BlockSpec((GDN_CONV_W, cw), lambda bi, j: (0, j))],
        out_specs=pl.BlockSpec((1, n, cw), lambda bi, j: (bi, 0, j)),
        out_shape=jax.ShapeDtypeStruct((b, n, GDN_QKV), F32),
        scratch_shapes=[pltpu.VMEM((n + 16, cw), F32)],
        compiler_params=_cparams(2, VMEM_LIMIT),
        name="gdn_conv",
    )(proj3, conv_w)


def _gdn_gates_kernel(ab_ref, alog_ref, dtb_ref, uf_ref, ub_ref, gf_ref, gb_ref, beta_ref):
    ab = ab_ref[...]
    a = jnp.concatenate([ab[0:8], ab[16:24]], axis=0)
    bb = jnp.concatenate([ab[8:16], ab[24:32]], axis=0)
    z = a + dtb_ref[...]
    sp = jnp.maximum(z, 0.0) + jnp.log(1.0 + jnp.exp(-jnp.abs(z)))
    g = -jnp.exp(alog_ref[...]) * sp
    beta_ref[...] = jax.nn.sigmoid(bb)
    gf_ref[...] = _dot_hi(g, uf_ref[...])
    gb_ref[...] = _dot_hi(g, ub_ref[...])


def _gdn_gates(ab_t, a_log, dt_bias):
    t = ab_t.shape[1]
    tl = 256
    idx = np.arange(tl)
    same = (idx[:, None] // GDN_CHUNK) == (idx[None, :] // GDN_CHUNK)
    uf = jnp.asarray((same & (idx[:, None] <= idx[None, :])).astype(np.float32))
    ub = jnp.asarray((same & (idx[:, None] >= idx[None, :])).astype(np.float32))
    spec = pl.BlockSpec((16, tl), lambda i: (0, i))
    return pl.pallas_call(
        _gdn_gates_kernel,
        grid=(t // tl,),
        in_specs=[pl.BlockSpec((32, tl), lambda i: (0, i)),
                  pl.BlockSpec((16, 1), lambda i: (0, 0)),
                  pl.BlockSpec((16, 1), lambda i: (0, 0)),
                  pl.BlockSpec((tl, tl), lambda i: (0, 0)),
                  pl.BlockSpec((tl, tl), lambda i: (0, 0))],
        out_specs=[spec, spec, spec],
        out_shape=[jax.ShapeDtypeStruct((16, t), F32)] * 3,
        compiler_params=_cparams(1),
        name="gdn_gates",
    )(ab_t, a_log.reshape(16, 1), dt_bias.reshape(16, 1), uf, ub)


def _tri_inverse_many(a_list, block):
    n = a_list[0].shape[0]
    ri = lax.broadcasted_iota(jnp.int32, (n, n), 0)
    ci = lax.broadcasted_iota(jnp.int32, (n, n), 1)
    eye = (ri == ci).astype(F32)
    xs = [(-a).astype(BF16) for a in a_list]
    ts = [eye - a for a in a_list]
    pws = [_dot(x, x) for x in xs]
    n_steps = int(math.log2(block)) - 1
    for step in range(n_steps):
        pwbs = [pw.astype(BF16) for pw in pws]
        if step < n_steps - 1:
            sts = [_dot(jnp.concatenate([pwb, t.astype(BF16)], axis=0), pwb) for pwb, t in zip(pwbs, ts)]
            pws = [st[:n] for st in sts]
            ts = [t + st[n:] for t, st in zip(ts, sts)]
        else:
            ts = [t + _dot(t.astype(BF16), pwb) for t, pwb in zip(ts, pwbs)]
    a_his = [a.astype(BF16) for a in a_list]
    a_los = [(a - a_hi.astype(F32)).astype(BF16) for a, a_hi in zip(a_list, a_his)]
    t_his = [t.astype(BF16) for t in ts]
    t_los = [(t - t_hi.astype(F32)).astype(BF16) for t, t_hi in zip(ts, t_his)]
    at2s = [_dot(jnp.concatenate([a_hi, a_lo], axis=0), t_hi) for a_hi, a_lo, t_hi in zip(a_his, a_los, t_his)]
    at3s = [_dot(a_hi, t_lo) for a_hi, t_lo in zip(a_his, t_los)]
    resids = [((eye - t) - (at2[:n] + at2[n:] + at3)).astype(BF16) for t, at2, at3 in zip(ts, at2s, at3s)]
    return [t + _dot(t_hi, r) for t, t_hi, r in zip(ts, t_his, resids)]


GDN_SLOT = 5 * GDN_HEAD_DIM


GDN_GROUP = 256


def _gdn_prep_kernel(q_ref, k_ref, v_ref, cols_ref, rows_ref, pf_ref, pb_ref, *, heads_blk):
    c = GDN_CHUNK
    n = GDN_GROUP
    ri = lax.broadcasted_iota(jnp.int32, (n, n), 0)
    cj = lax.broadcasted_iota(jnp.int32, (n, n), 1)
    same = (ri // c) == (cj // c)
    lane = lax.broadcasted_iota(jnp.int32, (n, LANE), 1)
    heads = []
    for hh in range(heads_blk):
        lanes = slice(hh * GDN_HEAD_DIM, (hh + 1) * GDN_HEAD_DIM)
        q = q_ref[0, :, lanes]
        k = k_ref[0, :, lanes]
        kb16 = k.astype(BF16)
        heads.append((q, k, v_ref[0, :, lanes], _dot_nt(kb16, kb16), _dot_nt(q.astype(BF16), kb16)))
    chains = []
    for hh in range(heads_blk):
        cols = cols_ref[0, hh, 0]
        rows = rows_ref[0, hh, 0]
        kk = heads[hh][3]
        for direction in range(2):
            gcol = cols[:, 2 * direction:2 * direction + 1]
            bcol = cols[:, 2 * direction + 1:2 * direction + 2]
            glast = cols[:, 4 + direction:5 + direction]
            grow = rows[direction:direction + 1, :]
            if direction == 0:
                incl, strict = same & (ri >= cj), same & (ri > cj)
            else:
                incl, strict = same & (ri <= cj), same & (ri < cj)
            decay = jnp.where(incl, jnp.exp(jnp.where(incl, gcol - grow, 0.0)), 0.0)
            chains.append((hh, direction, gcol, bcol, glast, decay,
                           jnp.where(strict, kk * bcol * decay, 0.0)))
    tmats = _tri_inverse_many([ch[6] for ch in chains], c)
    uws = []
    for (hh, direction, gcol, bcol, glast, decay, _), tmat in zip(chains, tmats):
        q, k, v, _, _ = heads[hh]
        rhs = jnp.concatenate([v * bcol, k * (bcol * jnp.exp(gcol))], axis=1).astype(BF16)
        uws.append(_dot(tmat.astype(BF16), rhs))
    for (hh, direction, gcol, bcol, glast, decay, _), uw in zip(chains, uws):
        q, k, v, _, qk = heads[hh]
        attn = qk * decay
        folded = attn[:, :LANE] + attn[:, LANE:]
        folded = folded + pltpu.roll(folded, c, 1)
        egl = jnp.exp(glast)
        egl_hi = egl.astype(BF16).astype(F32)
        tail = jnp.where(lane < c + c // 2, egl_hi, egl - egl_hi)
        a128 = jnp.where(lane < c, folded, tail)
        slot = jnp.concatenate([uw, q * jnp.exp(gcol), k * jnp.exp(glast - gcol), a128], axis=1)
        out_ref = pf_ref if direction == 0 else pb_ref
        out_ref[0, :, hh * GDN_SLOT:(hh + 1) * GDN_SLOT] = slot.astype(BF16)


def _gdn_prep(qkv, cols, rows):
    b, n, _ = qkv.shape
    hb = 4
    nhb = GDN_HEADS // hb
    tb = GDN_GROUP
    w = hb * GDN_HEAD_DIM
    p_spec = pl.BlockSpec((1, tb, hb * GDN_SLOT), lambda bi, h, t: (bi, t, h))
    return pl.pallas_call(
        functools.partial(_gdn_prep_kernel, heads_blk=hb),
        grid=(b, nhb, n // tb),
        in_specs=[pl.BlockSpec((1, tb, w), lambda bi, h, t: (bi, t, h)),
                  pl.BlockSpec((1, tb, w), lambda bi, h, t: (bi, t, nhb + h)),
                  pl.BlockSpec((1, tb, w), lambda bi, h, t: (bi, t, 2 * nhb + h)),
                  pl.BlockSpec((1, hb, 1, tb, 8), lambda bi, h, t: (bi, h, t, 0, 0)),
                  pl.BlockSpec((1, hb, 1, 8, tb), lambda bi, h, t: (bi, h, t, 0, 0))],
        out_specs=[p_spec, p_spec],
        out_shape=[jax.ShapeDtypeStruct((b, n, GDN_HEADS * GDN_SLOT), BF16)] * 2,
        compiler_params=_cparams(3, VMEM_LIMIT),
        name="gdn_prep",
    )(qkv, qkv, qkv, cols, rows)


def _gdn_scan_kernel(pf_ref, pb_ref, s0f_ref, s0b_ref, of_ref, ob_ref, sf_ref, sb_ref, s_scr,
                     *, heads_blk, chunks_blk):
    c = GDN_CHUNK
    hd = GDN_HEAD_DIM
    t = pl.program_id(2)

    @pl.when(t == 0)
    def _():
        for hh in range(heads_blk):
            s_scr[2 * hh] = s0f_ref[0, hh]
            s_scr[2 * hh + 1] = s0b_ref[0, hh]

    def body(i, carry):
        tf = pl.multiple_of(i * c, c)
        tbk = pl.multiple_of((chunks_blk - 1 - i) * c, c)
        chains = [(pf_ref, of_ref, tf, hh, 2 * hh) for hh in range(heads_blk)]
        chains += [(pb_ref, ob_ref, tbk, hh, 2 * hh + 1) for hh in range(heads_blk)]
        states = [s_scr[ci] for _, _, _, _, ci in chains]
        wss = []
        for (p_ref, _, t0, hh, _), state in zip(chains, states):
            base = hh * GDN_SLOT
            wq = jnp.concatenate([p_ref[0, pl.ds(t0, c), base + hd:base + 2 * hd],
                                  p_ref[0, pl.ds(t0, c), base + 2 * hd:base + 3 * hd]], axis=0)
            wss.append(_dot(wq, state.astype(BF16)))
        v_news = []
        for (p_ref, _, t0, hh, _), ws in zip(chains, wss):
            base = hh * GDN_SLOT
            v_news.append((p_ref[0, pl.ds(t0, c), base:base + hd].astype(F32) - ws[:c]).astype(BF16))
        a128s = [p_ref[0, pl.ds(t0, c), hh * GDN_SLOT + 4 * hd:hh * GDN_SLOT + 5 * hd]
                 for p_ref, _, t0, hh, _ in chains]
        outs = [ws[c:] + _dot(a128[:, :c], v_new) for ws, a128, v_new in zip(wss, a128s, v_news)]
        upds = [_dot_tn(p_ref[0, pl.ds(t0, c), hh * GDN_SLOT + 3 * hd:hh * GDN_SLOT + 4 * hd], v_new)
                for (p_ref, _, t0, hh, _), v_new in zip(chains, v_news)]
        for (_, o_ref, t0, hh, ci), state, a128, o, upd in zip(chains, states, a128s, outs, upds):
            egl = a128[0:1, c:c + 1].astype(F32) + a128[0:1, c + c // 2:c + c // 2 + 1].astype(F32)
            s_scr[ci] = state * egl + upd
            o_ref[0, pl.ds(t0, c), hh * hd:(hh + 1) * hd] = o
        return carry

    lax.fori_loop(0, chunks_blk, body, 0)

    @pl.when(t == pl.num_programs(2) - 1)
    def _():
        for hh in range(heads_blk):
            sf_ref[0, hh] = s_scr[2 * hh]
            sb_ref[0, hh] = s_scr[2 * hh + 1]


def _gdn_scan(pf, pb, s0f, s0b):
    b, n, _ = pf.shape
    hs = 8
    nhs = GDN_HEADS // hs
    tb = min(n, 512)
    nt = n // tb
    cb = tb // GDN_CHUNK
    st_spec = pl.BlockSpec((1, hs, GDN_HEAD_DIM, GDN_HEAD_DIM), lambda bi, h, t: (bi, h, 0, 0))
    return pl.pallas_call(
        functools.partial(_gdn_scan_kernel, heads_blk=hs, chunks_blk=cb),
        grid=(b, nhs, nt),
        in_specs=[pl.BlockSpec((1, tb, hs * GDN_SLOT), lambda bi, h, t: (bi, t, h)),
                  pl.BlockSpec((1, tb, hs * GDN_SLOT), lambda bi, h, t: (bi, nt - 1 - t, h)),
                  st_spec, st_spec],
        out_specs=[pl.BlockSpec((1, tb, hs * GDN_HEAD_DIM), lambda bi, h, t: (bi, t, h)),
                   pl.BlockSpec((1, tb, hs * GDN_HEAD_DIM), lambda bi, h, t: (bi, nt - 1 - t, h)),
                   st_spec, st_spec],
        out_shape=[jax.ShapeDtypeStruct((b, n, GDN_KEY), F32)] * 2
        + [jax.ShapeDtypeStruct((b, GDN_HEADS, GDN_HEAD_DIM, GDN_HEAD_DIM), F32)] * 2,
        scratch_shapes=[pltpu.VMEM((2 * hs, GDN_HEAD_DIM, GDN_HEAD_DIM), F32)],
        compiler_params=_cparams(3, VMEM_LIMIT),
        name="gdn_scan",
    )(pf, pb, s0f, s0b)


def _gdn_gate_tables(gf, gb, beta, b, n):
    nc = n // GDN_CHUNK
    ng = n // GDN_GROUP

    def split(x):
        return x.reshape(2, GDN_HEADS, b, nc, GDN_CHUNK)

    def group(x):
        return x.reshape(GDN_HEADS, b, ng, GDN_GROUP)

    gf5, gb5, be5 = split(gf), split(gb), split(beta)
    gcf, gcb = group(gf5[0]), group(gb5[1])
    glf = group(jnp.broadcast_to(gf5[0][..., GDN_CHUNK - 1:], gf5[0].shape))
    glb = group(jnp.broadcast_to(gb5[1][..., :1], gb5[1].shape))
    zero = jnp.zeros_like(gcf)
    col_list = [gcf, group(be5[0]), gcb, group(be5[1]), glf, glb, zero, zero]
    cols_t = jnp.stack(col_list, axis=-2).transpose(1, 0, 2, 3, 4)
    cols = jnp.swapaxes(cols_t, -1, -2)
    row_list = [gcf, gcb, zero, zero, zero, zero, zero, zero]
    rows = jnp.stack(row_list, axis=-2).transpose(1, 0, 2, 3, 4)
    return cols, rows


def _gdn_out_kernel(ofp_ref, obp_ref, ofs_ref, obs_ref, z_ref, g_ref, y_ref, o_scr, *, n_prompt_tiles):
    i = pl.program_id(0)

    @pl.when(i < n_prompt_tiles)
    def _():
        o_scr[...] = ofp_ref[...] + obp_ref[...]

    @pl.when(i >= n_prompt_tiles)
    def _():
        o_scr[...] = ofs_ref[...] + obs_ref[...]

    o = o_scr[...]
    z = z_ref[...]
    gate = z * jax.nn.sigmoid(z)
    outs = []
    for h in range(GDN_HEADS):
        sl = slice(h * GDN_HEAD_DIM, (h + 1) * GDN_HEAD_DIM)
        oh = o[:, sl]
        r = lax.rsqrt(jnp.mean(oh * oh, axis=-1, keepdims=True) + EPS)
        outs.append(oh * r * g_ref[...] * gate[:, sl])
    y_ref[...] = jnp.concatenate(outs, axis=1).astype(BF16)


def _gdn_out(o_prompt, o_sample, proj, norm_g):
    t = proj.shape[0]
    npt = o_prompt[0].shape[0] // TM
    z_block = GDN_QKV // GDN_KEY
    p_spec = pl.BlockSpec((TM, GDN_KEY), lambda i: (jnp.minimum(i, npt - 1), 0))
    s_spec = pl.BlockSpec((TM, GDN_KEY), lambda i: (jnp.maximum(i - npt, 0), 0))
    return pl.pallas_call(
        functools.partial(_gdn_out_kernel, n_prompt_tiles=npt),
        grid=(t // TM,),
        in_specs=[p_spec, p_spec, s_spec, s_spec,
                  pl.BlockSpec((TM, GDN_KEY), lambda i: (i, z_block)),
                  pl.BlockSpec((1, GDN_HEAD_DIM), lambda i: (0, 0))],
        out_specs=pl.BlockSpec((TM, GDN_KEY), lambda i: (i, 0)),
        out_shape=jax.ShapeDtypeStruct((t, GDN_KEY), BF16),
        scratch_shapes=[pltpu.VMEM((TM, GDN_KEY), F32)],
        compiler_params=_cparams(1, VMEM_LIMIT),
        name="gdn_out",
    )(*o_prompt, *o_sample, proj, norm_g.reshape(1, -1))


def _store_token_tiles(ref, x):
    for s in range(TOKEN_TILE_ROWS):
        ref[pl.ds(s, x.shape[0], stride=TOKEN_TILE_ROWS), :] = x[:, s * LANE:(s + 1) * LANE]


def _load_token_tiles(ref, rows):
    return [ref[pl.ds(s, rows, stride=TOKEN_TILE_ROWS), :] for s in range(TOKEN_TILE_ROWS)]


def _router_kernel(x_ref, g_ref, mod_ref, wr_ref, br_ref, h_ref, idx_ref, w_ref, cnt_ref):
    h = _modulate(x_ref[...], g_ref[...], mod_ref, 3, 4)
    _store_token_tiles(h_ref, h)
    logits = lax.dot_general(wr_ref[...], h, (((1,), (1,)), ((), ())),
                             preferred_element_type=F32, precision=HIGHEST) + br_ref[...]
    ne, tm = logits.shape
    iota = lax.broadcasted_iota(jnp.int32, (ne, tm), 0)
    vals, idxs = [], []
    cur = logits
    for _ in range(TOP_K):
        m = jnp.max(cur, axis=0, keepdims=True)
        idx = jnp.min(jnp.where(cur == m, iota, ne), axis=0, keepdims=True)
        vals.append(m)
        idxs.append(idx)
        cur = jnp.where(iota == idx, -jnp.inf, cur)
    es = [jnp.exp(v - vals[0]) for v in vals]
    tot = es[0] + es[1] + es[2] + es[3]
    cnt = jnp.zeros((ne, 1), jnp.int32)
    for kk in range(TOP_K):
        idx_ref[kk:kk + 1, :] = idxs[kk]
        w_ref[kk:kk + 1, :] = es[kk] / tot
        cnt = cnt + jnp.sum((iota == idxs[kk]).astype(jnp.int32), axis=1, keepdims=True)

    @pl.when(pl.program_id(0) == 0)
    def _():
        cnt_ref[...] = jnp.zeros(cnt_ref.shape, jnp.int32)

    cnt_ref[...] += cnt


def _router(x, g, mod, w_router_t, b_router):
    t, d = x.shape
    return pl.pallas_call(
        _router_kernel,
        grid=(t // TM,),
        in_specs=[pl.BlockSpec((TM, d), lambda i: (i, 0)),
                  pl.BlockSpec((1, d), lambda i: (0, 0)),
                  pl.BlockSpec((1, 6, 1, d), lambda i: (_group_of_tile(i, TM), 0, 0, 0)),
                  pl.BlockSpec((N_EXPERTS, d), lambda i: (0, 0)),
                  pl.BlockSpec((N_EXPERTS, 1), lambda i: (0, 0))],
        out_specs=[pl.BlockSpec((TM * TOKEN_TILE_ROWS, LANE), lambda i: (i, 0)),
                   pl.BlockSpec((TOP_K, TM), lambda i: (0, i)),
                   pl.BlockSpec((TOP_K, TM), lambda i: (0, i)),
                   pl.BlockSpec((N_EXPERTS, 1), lambda i: (0, 0))],
        out_shape=[jax.ShapeDtypeStruct((t * TOKEN_TILE_ROWS, LANE), F32),
                   jax.ShapeDtypeStruct((TOP_K, t), jnp.int32),
                   jax.ShapeDtypeStruct((TOP_K, t), F32),
                   jax.ShapeDtypeStruct((N_EXPERTS, 1), jnp.int32)],
        compiler_params=_cparams(1, VMEM_LIMIT),
        name="moe_router",
    )(x, g, mod, w_router_t, b_router.reshape(N_EXPERTS, 1))


def _tile_rows(i):
    return pl.ds(pl.multiple_of(i * TOKEN_TILE_ROWS, TOKEN_TILE_ROWS), TOKEN_TILE_ROWS)


ITEM_FIRST, ITEM_LAST, ITEM_VALID, ITEM_FINAL = 1, 2, 4, 8
MOE_ROW_TILES = MOE_ROWS // MOE_TM
MOE_TILE_SUBLANES = MOE_TM * TOKEN_TILE_ROWS


def _issue_token_dmas(make_copy):
    def issue(g, carry):
        for u in range(ROW_DMA_UNROLL):
            make_copy(g * ROW_DMA_UNROLL + u).start(priority=u % 2)
        return carry

    lax.fori_loop(0, MOE_TM // ROW_DMA_UNROLL, issue, 0)


def _experts_kernel(tile_ref, exp_ref, lo_ref, hi_ref, flag_ref, src_ref, src_next_ref, dst_ref,
                    h_hbm, wgu_ref, bgu_ref, wd_ref, bd_ref, wrow_ref, y_hbm,
                    wgu_bf, wd_bf, x_bf, acc, xg, yo, gsem, ssem):
    j = pl.program_id(0)
    t = tile_ref[j]
    e = exp_ref[j]
    prev = exp_ref[jnp.maximum(j - 1, 0)]
    flags = flag_ref[j]
    first = (flags & ITEM_FIRST) != 0
    last = (flags & ITEM_LAST) != 0
    odd = lax.rem(t, 2)

    def gather_copy(idx_ref, slot, r):
        return pltpu.make_async_copy(h_hbm.at[_tile_rows(idx_ref[0, 0, r])], xg.at[slot, _tile_rows(r)],
                                     gsem.at[slot])

    def gather_all(slot):
        return pltpu.make_async_copy(h_hbm.at[pl.ds(0, MOE_TILE_SUBLANES)], xg.at[slot], gsem.at[slot])

    def scatter_copy(slot, r):
        return pltpu.make_async_copy(yo.at[slot, _tile_rows(r)], y_hbm.at[_tile_rows(dst_ref[0, 0, r])],
                                     ssem.at[slot])

    def scatter_all(slot):
        return pltpu.make_async_copy(yo.at[slot], y_hbm.at[pl.ds(0, MOE_TILE_SUBLANES)], ssem.at[slot])

    @pl.when(j == 0)
    def _():
        _issue_token_dmas(functools.partial(gather_copy, src_ref, 0))

    @pl.when((j == 0) | (e != prev))
    def _():
        wgu_bf[...] = wgu_ref[0, 0].astype(BF16)
        wd_bf[...] = wd_ref[0, 0].astype(BF16)

    for slot in range(2):
        @pl.when(first & (odd == slot))
        def _(slot=slot):
            gather_all(slot).wait()

            @pl.when(t + 1 < MOE_ROW_TILES)
            def _():
                _issue_token_dmas(functools.partial(gather_copy, src_next_ref, 1 - slot))

            for s, xs in enumerate(_load_token_tiles(xg.at[slot], MOE_TM)):
                x_bf[:, s * LANE:(s + 1) * LANE] = xs.astype(BF16)

    @pl.when((flags & ITEM_VALID) != 0)
    def _():
        gu = _dot(x_bf[...], wgu_bf[...]) + bgu_ref[0]
        glu = jnp.minimum(gu[:, :D_EXPERT], SWIGLU_LIMIT)
        lin = jnp.clip(gu[:, D_EXPERT:], -SWIGLU_LIMIT, SWIGLU_LIMIT)
        act = (lin + 1.0) * glu * jax.nn.sigmoid(SWIGLU_ALPHA * glu)
        y = (_dot(act.astype(BF16), wd_bf[...]) + bd_ref[0]) * wrow_ref[:, 0:1]
        row = lax.broadcasted_iota(jnp.int32, (MOE_TM, 1), 0)
        mine = (row >= lo_ref[j]) & (row < hi_ref[j])

        @pl.when(first)
        def _():
            acc[...] = jnp.where(mine, y, 0.0)

        @pl.when(jnp.logical_not(first))
        def _():
            acc[...] = jnp.where(mine, y, acc[...])

    for slot in range(2):
        @pl.when(last & (odd == slot))
        def _(slot=slot):
            @pl.when(t >= 2)
            def _():
                scatter_all(slot).wait()

            _store_token_tiles(yo.at[slot], acc[...])
            _issue_token_dmas(functools.partial(scatter_copy, slot))

            @pl.when((flags & ITEM_FINAL) != 0)
            def _():
                scatter_all(1 - slot).wait()
                scatter_all(slot).wait()


def _experts(plan, h_tiles, src_tok, dst_tok, layer, w_gate_up, b_gate_up, w_down, b_down, w_sorted):
    d = D_MODEL
    a = src_tok.shape[0]

    def tile_map(j, tile, exp, lo, hi, flag):
        return (tile[j], 0)

    def idx_map(j, tile, exp, lo, hi, flag):
        return (tile[j], 0, 0)

    def idx_next_map(j, tile, exp, lo, hi, flag):
        return (jnp.minimum(tile[j] + 1, MOE_ROW_TILES - 1), 0, 0)

    def exp_map(j, tile, exp, lo, hi, flag):
        return (exp[j], 0, 0)

    def w_map(j, tile, exp, lo, hi, flag):
        return (layer, exp[j], 0, 0)

    idx_block = (1, 1, MOE_TM)
    src3 = src_tok.reshape(MOE_ROW_TILES, 1, MOE_TM)
    grid_spec = pltpu.PrefetchScalarGridSpec(
        num_scalar_prefetch=5,
        grid=(MOE_ITEMS,),
        in_specs=[pl.BlockSpec(idx_block, idx_map, memory_space=pltpu.SMEM),
                  pl.BlockSpec(idx_block, idx_next_map, memory_space=pltpu.SMEM),
                  pl.BlockSpec(idx_block, idx_map, memory_space=pltpu.SMEM),
                  pl.BlockSpec(memory_space=pl.ANY),
                  pl.BlockSpec((1, 1, d, 2 * D_EXPERT), w_map),
                  pl.BlockSpec((1, 1, 2 * D_EXPERT), exp_map),
                  pl.BlockSpec((1, 1, D_EXPERT, d), w_map),
                  pl.BlockSpec((1, 1, d), exp_map),
                  pl.BlockSpec((MOE_TM, LANE), tile_map)],
        out_specs=pl.BlockSpec(memory_space=pl.ANY),
        scratch_shapes=[pltpu.VMEM((d, 2 * D_EXPERT), BF16), pltpu.VMEM((D_EXPERT, d), BF16),
                        pltpu.VMEM((MOE_TM, d), BF16), pltpu.VMEM((MOE_TM, d), F32),
                        pltpu.VMEM((2, MOE_TILE_SUBLANES, LANE), F32),
                        pltpu.VMEM((2, MOE_TILE_SUBLANES, LANE), F32),
                        pltpu.SemaphoreType.DMA((2,)), pltpu.SemaphoreType.DMA((2,))],
    )
    return pl.pallas_call(
        _experts_kernel,
        grid_spec=grid_spec,
        out_shape=jax.ShapeDtypeStruct((a * TOKEN_TILE_ROWS, LANE), F32),
        compiler_params=_cparams(1, VMEM_LIMIT),
        name="moe_experts",
    )(*plan, src3, src3, dst_tok.reshape(MOE_ROW_TILES, 1, MOE_TM), h_tiles, w_gate_up,
      b_gate_up.reshape(N_EXPERTS, 1, -1), w_down, b_down.reshape(N_EXPERTS, 1, -1), w_sorted)


def _moe_sum_kernel(x_ref, gate_ref, y0, y1, y2, y3, o_ref, ysum):
    ysum[...] = y0[...] + y1[...] + y2[...] + y3[...]
    gate = gate_ref[0, 0]
    for s, ys in enumerate(_load_token_tiles(ysum, TM)):
        cols = slice(s * LANE, (s + 1) * LANE)
        o_ref[:, cols] = x_ref[:, cols] + gate[:, cols] * ys


def _moe_sum(x, mod, y_tok):
    t, d = x.shape
    nt = t // TM
    blk = TM * TOKEN_TILE_ROWS
    y_specs = [pl.BlockSpec((blk, LANE), lambda i, k=k: (k * nt + i, 0)) for k in range(TOP_K)]
    return pl.pallas_call(
        _moe_sum_kernel,
        grid=(nt,),
        in_specs=[pl.BlockSpec((TM, d), lambda i: (i, 0)),
                  pl.BlockSpec((1, 1, 1, d), lambda i: (_group_of_tile(i, TM), 5, 0, 0))] + y_specs,
        out_specs=pl.BlockSpec((TM, d), lambda i: (i, 0)),
        out_shape=jax.ShapeDtypeStruct((t, d), F32),
        scratch_shapes=[pltpu.VMEM((blk, LANE), F32)],
        compiler_params=_cparams(1, VMEM_LIMIT),
        name="moe_sum",
    )(x, mod, y_tok, y_tok, y_tok, y_tok)


def _pick(onehot, vec):
    return jnp.sum(jnp.where(onehot, vec[None, :], 0), axis=1)


def _moe_plan(counts):
    ends = jnp.cumsum(counts)
    starts = ends - counts
    first_tile = starts // MOE_TM
    n_items_e = jnp.where(counts > 0, (ends - 1) // MOE_TM - first_tile + 1, 0)
    item_ends = jnp.cumsum(n_items_e)
    item_starts = item_ends - n_items_e
    n_items = item_ends[-1]
    j = jnp.arange(MOE_ITEMS, dtype=jnp.int32)
    jc = jnp.clip(j, 0, jnp.maximum(n_items - 1, 0))
    e_j = jnp.minimum(jnp.sum(item_ends[None, :] <= jc[:, None], axis=1), N_EXPERTS - 1).astype(jnp.int32)
    onehot = e_j[:, None] == jnp.arange(N_EXPERTS, dtype=jnp.int32)[None, :]
    tile_j = _pick(onehot, first_tile) + (jc - _pick(onehot, item_starts))
    lo = jnp.maximum(_pick(onehot, starts), tile_j * MOE_TM) - tile_j * MOE_TM
    hi = jnp.minimum(_pick(onehot, ends), (tile_j + 1) * MOE_TM) - tile_j * MOE_TM
    valid = j < n_items
    prev_tile = jnp.concatenate([jnp.full((1,), -1, jnp.int32), tile_j[:-1]])
    next_tile = jnp.concatenate([tile_j[1:], jnp.full((1,), -1, jnp.int32)])
    first = valid & (tile_j != prev_tile)
    last = valid & ((tile_j != next_tile) | (j == n_items - 1))
    flags = (first * ITEM_FIRST + last * ITEM_LAST + valid * ITEM_VALID
             + (j == n_items - 1) * ITEM_FINAL)
    i32 = lambda v: v.astype(jnp.int32)
    return i32(tile_j), e_j, i32(lo), i32(jnp.where(valid, hi, lo)), i32(flags)


def _moe_layer(x, g_ffn, mod, layer, w_router, b_router, w_gate_up, b_gate_up, w_down, b_down):
    t = x.shape[0]
    a = TOP_K * t
    h, idx_t, w_t, counts = _router(x, g_ffn.reshape(1, -1), mod, w_router.T, b_router)
    _, order, w_sorted = lax.sort((idx_t.reshape(a), jnp.arange(a, dtype=jnp.int32), w_t.reshape(a)),
                                  num_keys=1, is_stable=False)
    plan = _moe_plan(counts.reshape(N_EXPERTS))
    y_tok = _experts(plan, h, order % t, order, layer, w_gate_up, b_gate_up, w_down, b_down,
                     jnp.broadcast_to(w_sorted[:, None], (a, LANE)))
    return _moe_sum(x, mod, y_tok)


def _na_layer(x, g_mix, mod, cache_k, cache_v, w_in, q_g, k_g, rpb, w_out):
    hd = NA_HEADS * NA_HEAD_DIM
    gains = jnp.concatenate([jnp.tile(q_g * NA_HEAD_DIM ** -0.5, NA_HEADS), jnp.tile(k_g, NA_HEADS),
                             jnp.ones((hd,), F32)]).reshape(1, -1)
    tn = 1024
    qkv = _modmm(x, g_mix.reshape(1, -1), mod, w_in.astype(BF16), shift_idx=0, scale_idx=1, tn=tn,
                 norm=(2 * hd // tn, gains, NA_HEAD_DIM), name="na_qkv")
    qkv_by_seq = qkv.reshape(TOKENS // SEQ, SEQ, 3 * hd)
    qkv_by_dec = qkv.reshape(TOKENS // DEC_SEQ, DEC_SEQ, 3 * hd)
    n_pairs = NA_HEADS // 2
    o_p = _pair_attn(qkv_by_seq, qkv_by_seq, qkv_by_seq, b=BATCH, q_col=0, k_col=n_pairs,
                     v_col=2 * n_pairs, slot=NA_HEAD_DIM, tq=SEQ, n_pairs=n_pairs, pairs_blk=n_pairs,
                     name="na_context_attn")
    bias = _na_bias_table(rpb)
    o_s = _na_latent_attn(qkv_by_dec, cache_k.reshape(DEC_BATCH, PAST_LEN, hd),
                          cache_v.reshape(DEC_BATCH, PAST_LEN, hd), bias, PROMPT_TOK // DEC_SEQ)
    o = jnp.concatenate([o_p.reshape(PROMPT_TOK, hd), o_s.reshape(SAMPLE_TOK, hd)], axis=0)
    x = _mm_res(o, w_out.astype(BF16), x, mod, 2, name="na_out")
    new_k = qkv[:PROMPT_TOK, hd:2 * hd].reshape(BATCH, SEQ, NA_HEADS, NA_HEAD_DIM)
    new_v = qkv[:PROMPT_TOK, 2 * hd:].reshape(BATCH, SEQ, NA_HEADS, NA_HEAD_DIM)
    return x, new_k, new_v


def _mla_weights(w_q_up, w_kv_up, q_g, k_g):
    wq = w_q_up.reshape(MLA_Q_LORA, MLA_HEADS, MLA_QK_DIM)
    wq = jnp.pad(wq, ((0, 0), (0, 0), (0, LANE - MLA_QK_DIM))).reshape(MLA_Q_LORA, MLA_HEADS * LANE)
    kv = w_kv_up.reshape(MLA_KV_LORA, MLA_HEADS, MLA_NOPE_DIM + MLA_V_DIM)
    wk_top = jnp.pad(kv[:, :, :MLA_NOPE_DIM], ((0, 0), (0, 0), (0, LANE - MLA_NOPE_DIM)))
    eye = jnp.eye(MLA_ROPE_DIM, dtype=F32)
    pe_rows = jnp.pad(eye, ((0, 0), (MLA_NOPE_DIM, LANE - MLA_QK_DIM)))
    pe_rows = jnp.broadcast_to(pe_rows[:, None, :], (MLA_ROPE_DIM, MLA_HEADS, LANE))
    k_in = 2 * LANE
    wk = jnp.concatenate([wk_top, pe_rows,
                          jnp.zeros((k_in - MLA_KV_LORA - MLA_ROPE_DIM, MLA_HEADS, LANE), F32)], axis=0)
    wk = wk.reshape(k_in, MLA_HEADS * LANE)
    wv = jnp.pad(kv[:, :, MLA_NOPE_DIM:].reshape(MLA_KV_LORA, MLA_HEADS * MLA_V_DIM),
                 ((0, k_in - MLA_KV_LORA), (0, 0)))
    wkv = jnp.concatenate([wk, wv], axis=1)
    pad_g = LANE - MLA_QK_DIM
    qgain = jnp.tile(jnp.pad(q_g * MLA_QK_DIM ** -0.5, (0, pad_g)), MLA_HEADS).reshape(1, -1)
    kgain = jnp.tile(jnp.pad(k_g, (0, pad_g)), MLA_HEADS)
    kvgain = jnp.concatenate([kgain, jnp.ones((MLA_HEADS * MLA_V_DIM,), F32)]).reshape(1, -1)
    return wq.astype(BF16), wkv.astype(BF16), qgain, kvgain


def _mla_layer(x, g_mix, mod, cache_ckv, cache_kpe, w_in, q_a_g, kv_a_g, w_q_up, w_kv_up, q_g, k_g,
               w_out):
    n_lat = MLA_Q_LORA + MLA_KV_LORA + MLA_ROPE_DIM
    w_in_p = jnp.pad(w_in, ((0, 0), (0, 512 - n_lat))).astype(BF16)
    proj = _modmm(x, g_mix.reshape(1, -1), mod, w_in_p, shift_idx=0, scale_idx=1, tn=512,
                  name="mla_in")
    cq, ckv, kin = _mla_latents(proj, q_a_g, kv_a_g)
    wq, wkv, qgain, kvgain = _mla_weights(w_q_up, w_kv_up, q_g, k_g)
    n_ktiles = MLA_HEADS * LANE // 256
    npt = PROMPT_TOK // TM
    spb = DEC_SEQ // TM
    q_tab = _rope_tables(DEC_SEQ, n_identity_rows_first=TM)
    q = _mmn(cq, wq, qgain, n_norm_tiles=n_ktiles, count=MLA_QK_DIM,
             rope=(q_tab, lambda i: jnp.where(i < npt, 0, 1 + (i - npt) % spb)), name="mla_q")
    kv_p = _mmn(kin[:PROMPT_TOK], wkv, kvgain, n_norm_tiles=n_ktiles, count=MLA_QK_DIM,
                name="mla_kv_prompt")
    kin_ctx = jnp.concatenate(
        [cache_ckv, cache_kpe, jnp.zeros((DEC_BATCH, PAST_LEN, 2 * LANE - MLA_KV_LORA - MLA_ROPE_DIM),
                                         F32)], axis=-1).astype(BF16)
    kin_s = jnp.concatenate([kin[PROMPT_TOK:].reshape(DEC_BATCH, DEC_SEQ, 2 * LANE), kin_ctx], axis=1)
    n_keys = DEC_SEQ + PAST_LEN
    kpb = n_keys // TM
    k_tab = _rope_tables(DEC_SEQ, n_identity_rows_last=PAST_LEN)
    kv_s = _mmn(kin_s.reshape(DEC_BATCH * n_keys, 2 * LANE), wkv, kvgain, n_norm_tiles=n_ktiles,
                count=MLA_QK_DIM, rope=(k_tab, lambda i: i % kpb), name="mla_kv_sample")
    hw = MLA_HEADS * LANE
    n_pairs = MLA_HEADS // 2
    q_by_seq = q.reshape(TOKENS // SEQ, SEQ, hw)
    q_by_dec = q.reshape(TOKENS // DEC_SEQ, DEC_SEQ, hw)
    kv_p3 = kv_p.reshape(BATCH, SEQ, -1)
    kv_s3 = kv_s.reshape(DEC_BATCH, n_keys, -1)
    o_p = _pair_attn(q_by_seq, kv_p3, kv_p3, b=BATCH, q_col=0, k_col=0, v_col=hw // LANE, slot=LANE,
                     tq=SEQ, n_pairs=n_pairs, pairs_blk=n_pairs, name="mla_context_attn")
    o_s = _pair_attn(q_by_dec, kv_s3, kv_s3, b=DEC_BATCH, q_batch0=PROMPT_TOK // DEC_SEQ, q_col=0,
                     k_col=0, v_col=hw // LANE, slot=LANE, tq=256, n_pairs=n_pairs, pairs_blk=2,
                     name="mla_latent_attn")
    ov = MLA_HEADS * MLA_V_DIM
    o = jnp.concatenate([o_p.reshape(PROMPT_TOK, ov), o_s.reshape(SAMPLE_TOK, ov)], axis=0)
    x = _mm_res(o, w_out.astype(BF16), x, mod, 2, name="mla_out")
    new_ckv = ckv[:PROMPT_TOK].reshape(BATCH, SEQ, MLA_KV_LORA)
    new_kpe = proj[:PROMPT_TOK, MLA_Q_LORA + MLA_KV_LORA:n_lat].reshape(BATCH, SEQ, MLA_ROPE_DIM)
    return x, new_ckv, new_kpe


def _gdn_layer(x, g_mix, mod, s_fwd, s_bwd, w_in, conv_w, a_log, dt_bias, norm_g, w_out):
    n_main = GDN_QKV + GDN_KEY
    gm = g_mix.reshape(1, -1)
    proj = _modmm(x, gm, mod, w_in[:, :n_main].astype(BF16), shift_idx=0, scale_idx=1, tn=1024,
                  name="gdn_in")
    w_ab = jnp.pad(w_in[:, n_main:], ((0, 0), (0, LANE - 4 * GDN_HEADS))).astype(BF16)
    ab = _modmm(x, gm, mod, w_ab, shift_idx=0, scale_idx=1, tn=LANE, name="gdn_in_ab")
    gf, gb, beta = _gdn_gates(ab[:, :4 * GDN_HEADS].T, a_log, dt_bias)
    outs = []
    states = []
    for lo, hi, b, n, s0f, s0b in (
            (0, PROMPT_TOK, BATCH, SEQ, None, None),
            (PROMPT_TOK, TOKENS, DEC_BATCH, DEC_SEQ, s_fwd, s_bwd)):
        if s0f is None:
            s0f = jnp.zeros((b, GDN_HEADS, GDN_HEAD_DIM, GDN_HEAD_DIM), F32)
            s0b = s0f
        qkv = _gdn_conv(proj.reshape(TOKENS // n, n, n_main), conv_w, b, lo // n)
        cols, rows = _gdn_gate_tables(gf[:, lo:hi], gb[:, lo:hi], beta[:, lo:hi], b, n)
        pf, pb = _gdn_prep(qkv, cols, rows)
        o_f, o_b, sf, sb = _gdn_scan(pf, pb, s0f, s0b)
        outs.append((o_f.reshape(b * n, GDN_KEY), o_b.reshape(b * n, GDN_KEY)))
        states.append((sf, sb))
    y = _gdn_out(outs[0], outs[1], proj, norm_g)
    x = _mm_res(y, w_out.astype(BF16), x, mod, 2, name="gdn_out_proj")
    return x, states[0][0], states[0][1]


def kernel(x_prompt, x_sample, cache_na_k, cache_na_v, cache_mla_ckv, cache_mla_kpe, state_gdn_fwd,
           state_gdn_bwd, c, c_ctx, ada_w, ada_b, norm_mix_g, norm_ffn_g, na_w_in, na_q_g, na_k_g,
           na_rpb, na_w_out, mla_w_in, mla_q_a_g, mla_kv_a_g, mla_w_q_up, mla_w_kv_up, mla_q_g,
           mla_k_g, mla_w_out, gdn_w_in, gdn_conv_w, gdn_a_log, gdn_dt_bias, gdn_norm_g, gdn_w_out,
           moe_w_router, moe_b_router, moe_w_gate_up, moe_b_gate_up, moe_w_down, moe_b_down):
    x = jnp.concatenate([x_prompt.reshape(PROMPT_TOK, D_MODEL), x_sample.reshape(SAMPLE_TOK, D_MODEL)],
                        axis=0)
    c_all = jnp.concatenate([c_ctx[None], c, jnp.zeros((16 - N_GROUPS, D_MODEL), F32)], axis=0)
    mods = _ada_mods(c_all, ada_w, ada_b)[:, :N_GROUPS].reshape(DEPTH, N_GROUPS, 6, 1, D_MODEL)
    new_na_k, new_na_v, new_ckv, new_kpe, new_sf, new_sb = [], [], [], [], [], []
    for i in range(DEPTH):
        kind, j = i % N_MIXERS, i // N_MIXERS
        mod = mods[i]
        if kind == 0:
            x, nk, nv = _na_layer(x, norm_mix_g[i], mod, cache_na_k[:, j], cache_na_v[:, j],
                                  na_w_in[j], na_q_g[j], na_k_g[j], na_rpb[j], na_w_out[j])
            new_na_k.append(nk)
            new_na_v.append(nv)
        elif kind == 1:
            x, ckv, kpe = _mla_layer(x, norm_mix_g[i], mod, cache_mla_ckv[:, j], cache_mla_kpe[:, j],
                                     mla_w_in[j], mla_q_a_g[j], mla_kv_a_g[j], mla_w_q_up[j],
                                     mla_w_kv_up[j], mla_q_g[j], mla_k_g[j], mla_w_out[j])
            new_ckv.append(ckv)
            new_kpe.append(kpe)
        else:
            x, sf, sb = _gdn_layer(x, norm_mix_g[i], mod, state_gdn_fwd[:, j], state_gdn_bwd[:, j],
                                   gdn_w_in[j], gdn_conv_w[j], gdn_a_log[j], gdn_dt_bias[j],
                                   gdn_norm_g[j], gdn_w_out[j])
            new_sf.append(sf)
            new_sb.append(sb)
        x = _moe_layer(x, norm_ffn_g[i], mod, i, moe_w_router[i], moe_b_router[i], moe_w_gate_up,
                       moe_b_gate_up[i], moe_w_down, moe_b_down[i])
    return (x[:PROMPT_TOK].reshape(BATCH, SEQ, D_MODEL),
            x[PROMPT_TOK:].reshape(DEC_BATCH, DEC_SEQ, D_MODEL),
            jnp.stack(new_na_k, axis=1), jnp.stack(new_na_v, axis=1),
            jnp.stack(new_ckv, axis=1), jnp.stack(new_kpe, axis=1),
            jnp.stack(new_sf, axis=1), jnp.stack(new_sb, axis=1))
```

```python
import functools
import math

import numpy as np
import jax
import jax.numpy as jnp
from jax import lax
from jax.experimental import pallas as pl
from jax.experimental.pallas import tpu as pltpu

F32 = jnp.float32
BF16 = jnp.bfloat16
HIGHEST = lax.Precision.HIGHEST

D_MODEL = 1024
BATCH = 32
SEQ = 256
DEPTH = 4
DEC_BATCH = 8
DEC_SEQ = 2048
PAST_LEN = 512
GRID_W = 64
N_MIXERS = 3
NA_HEADS = 16
NA_HEAD_DIM = 64
NA_ROWS = 8
NA_COLS = 16
MLA_HEADS = 16
MLA_NOPE_DIM = 64
MLA_ROPE_DIM = 32
MLA_QK_DIM = MLA_NOPE_DIM + MLA_ROPE_DIM
MLA_V_DIM = 64
MLA_Q_LORA = 256
MLA_KV_LORA = 128
GDN_HEADS = 8
GDN_HEAD_DIM = 128
GDN_KEY = GDN_HEADS * GDN_HEAD_DIM
GDN_QKV = 3 * GDN_KEY
GDN_CONV_W = 5
GDN_CHUNK = 64
N_EXPERTS = 32
TOP_K = 4
D_EXPERT = D_MODEL
SWIGLU_ALPHA = 1.702
SWIGLU_LIMIT = 7.0
ROPE_BASE = 10000.0
EPS = 1e-6

PROMPT_TOK = BATCH * SEQ
SAMPLE_TOK = DEC_BATCH * DEC_SEQ
TOKENS = PROMPT_TOK + SAMPLE_TOK
N_GROUPS = 1 + DEC_BATCH

LANE = 128
TM = 512
MOE_TM = 512
MOE_ROWS = TOKENS * TOP_K
MOE_ITEMS = MOE_ROWS // MOE_TM + N_EXPERTS
TOKEN_TILE_ROWS = D_MODEL // LANE
ROW_DMA_UNROLL = 8
NORM_W = 256
VMEM_LIMIT = 56 * 1024 * 1024
NEG_BIG = -1e30


def _cparams(n_axes, vmem=None):
    return pltpu.CompilerParams(dimension_semantics=("arbitrary",) * n_axes,
                                vmem_limit_bytes=vmem)


def _dot(a, b):
    return jnp.dot(a, b, preferred_element_type=F32)


def _dot_nt(a, b):
    return lax.dot_general(a, b, (((1,), (1,)), ((), ())), preferred_element_type=F32)


def _dot_tn(a, b):
    return lax.dot_general(a, b, (((0,), (0,)), ((), ())), preferred_element_type=F32)


def _dot_hi(a, b):
    return jnp.dot(a, b, preferred_element_type=F32, precision=HIGHEST)


def _group_of_tile(i, tm):
    npt = PROMPT_TOK // tm
    spb = DEC_SEQ // tm
    return jnp.where(i < npt, 0, 1 + (i - npt) // spb)


def _block_diag_ones(n, group):
    idx = np.arange(n) // group
    return jnp.asarray((idx[:, None] == idx[None, :]).astype(np.float32), BF16)


def _ada_kernel(c_ref, w_ref, b_ref, o_ref):
    c = c_ref[...]
    s = c * jax.nn.sigmoid(c)
    o_ref[0] = _dot_hi(s, w_ref[0]) + b_ref[0]


def _ada_mods(c_all, ada_w, ada_b):
    tn = 512
    n = 6 * D_MODEL
    return pl.pallas_call(
        _ada_kernel,
        grid=(DEPTH, n // tn),
        in_specs=[pl.BlockSpec((16, D_MODEL), lambda l, j: (0, 0)),
                  pl.BlockSpec((1, D_MODEL, tn), lambda l, j: (l, 0, j)),
                  pl.BlockSpec((1, 1, tn), lambda l, j: (l, 0, j))],
        out_specs=pl.BlockSpec((1, 16, tn), lambda l, j: (l, 0, j)),
        out_shape=jax.ShapeDtypeStruct((DEPTH, 16, n), F32),
        compiler_params=_cparams(2),
        name="ada_mods",
    )(c_all, ada_w, ada_b.reshape(DEPTH, 1, n))


def _modulate(x, g, mod_ref, shift_idx, scale_idx):
    ms = jnp.mean(x * x, axis=-1, keepdims=True)
    y = x * lax.rsqrt(ms + EPS) * g
    return y * (1.0 + mod_ref[0, scale_idx]) + mod_ref[0, shift_idx]


def _group_norm(acc, bd, gain, inv_count):
    sq = (acc * acc).astype(BF16)
    ss = jnp.concatenate([_dot(sq[:, s * NORM_W:(s + 1) * NORM_W], bd)
                          for s in range(acc.shape[1] // NORM_W)], axis=1)
    return acc * lax.rsqrt(ss * inv_count + EPS) * gain


def _modmm_kernel(*refs, shift_idx, scale_idx, n_norm_tiles, inv_count):
    if n_norm_tiles:
        x_ref, g_ref, mod_ref, w_ref, gain_ref, bd_ref, o_ref, h_ref = refs
    else:
        x_ref, g_ref, mod_ref, w_ref, o_ref, h_ref = refs
    j = pl.program_id(1)

    @pl.when(j == 0)
    def _():
        h_ref[...] = _modulate(x_ref[...], g_ref[...], mod_ref, shift_idx, scale_idx).astype(BF16)

    acc = _dot(h_ref[...], w_ref[...])
    if n_norm_tiles:
        @pl.when(j < n_norm_tiles)
        def _():
            o_ref[...] = _group_norm(acc, bd_ref[...], gain_ref[...], inv_count).astype(o_ref.dtype)

        @pl.when(j >= n_norm_tiles)
        def _():
            o_ref[...] = acc.astype(o_ref.dtype)
    else:
        o_ref[...] = acc.astype(o_ref.dtype)


def _modmm(x, g, mod, w, *, shift_idx, scale_idx, tn, out_dtype=F32, norm=None, name="modmm"):
    t, d = x.shape
    n = w.shape[1]
    in_specs = [pl.BlockSpec((TM, d), lambda i, j: (i, 0)),
                pl.BlockSpec((1, d), lambda i, j: (0, 0)),
                pl.BlockSpec((1, 6, 1, d), lambda i, j: (_group_of_tile(i, TM), 0, 0, 0)),
                pl.BlockSpec((d, tn), lambda i, j: (0, j))]
    args = [x, g, mod, w]
    n_norm_tiles, inv_count = 0, 1.0
    if norm is not None:
        n_norm_tiles, gains, group = norm
        inv_count = 1.0 / group
        in_specs += [pl.BlockSpec((1, tn), lambda i, j: (0, j)),
                     pl.BlockSpec((NORM_W, NORM_W), lambda i, j: (0, 0))]
        args += [gains, _block_diag_ones(NORM_W, group)]
    return pl.pallas_call(
        functools.partial(_modmm_kernel, shift_idx=shift_idx, scale_idx=scale_idx,
                          n_norm_tiles=n_norm_tiles, inv_count=inv_count),
        grid=(t // TM, n // tn),
        in_specs=in_specs,
        out_specs=pl.BlockSpec((TM, tn), lambda i, j: (i, j)),
        out_shape=jax.ShapeDtypeStruct((t, n), out_dtype),
        scratch_shapes=[pltpu.VMEM((TM, d), BF16)],
        compiler_params=_cparams(2, VMEM_LIMIT),
        name=name,
    )(*args)


def _mm_res_kernel(a_ref, w_ref, res_ref, gate_ref, o_ref):
    o_ref[...] = res_ref[...] + gate_ref[0, 0] * _dot(a_ref[...], w_ref[...])


def _mm_res(a, w, res, mod, gate_idx, *, tn=1024, name="mm_res"):
    t, k = a.shape
    n = w.shape[1]
    return pl.pallas_call(
        _mm_res_kernel,
        grid=(t // TM, n // tn),
        in_specs=[pl.BlockSpec((TM, k), lambda i, j: (i, 0)),
                  pl.BlockSpec((k, tn), lambda i, j: (0, j)),
                  pl.BlockSpec((TM, tn), lambda i, j: (i, j)),
                  pl.BlockSpec((1, 1, 1, tn), lambda i, j: (_group_of_tile(i, TM), gate_idx, 0, j))],
        out_specs=pl.BlockSpec((TM, tn), lambda i, j: (i, j)),
        out_shape=jax.ShapeDtypeStruct((t, n), F32),
        compiler_params=_cparams(2, VMEM_LIMIT),
        name=name,
    )(a, w, res, mod)


def _rope_slots(y, c, s1, s2):
    outs = []
    for s in range(y.shape[1] // LANE):
        ys = y[:, s * LANE:(s + 1) * LANE]
        outs.append(ys * c + pltpu.roll(ys, LANE - MLA_ROPE_DIM // 2, 1) * s1
                    + pltpu.roll(ys, MLA_ROPE_DIM // 2, 1) * s2)
    return jnp.concatenate(outs, axis=1)


def _mmn_kernel(*refs, n_norm_tiles, inv_count, rope):
    if rope:
        a_ref, w_ref, gain_ref, bd_ref, c_ref, s1_ref, s2_ref, o_ref = refs
    else:
        a_ref, w_ref, gain_ref, bd_ref, o_ref = refs
    j = pl.program_id(1)
    acc = _dot(a_ref[...], w_ref[...])

    @pl.when(j < n_norm_tiles)
    def _():
        y = _group_norm(acc, bd_ref[...], gain_ref[...], inv_count)
        if rope:
            y = _rope_slots(y, c_ref[0], s1_ref[0], s2_ref[0])
        o_ref[...] = y.astype(o_ref.dtype)

    @pl.when(j >= n_norm_tiles)
    def _():
        o_ref[...] = acc.astype(o_ref.dtype)


MMN_TN = 1024


def _mmn(a, w, gains, *, n_norm_tiles, count, rope=None, tn=MMN_TN, name="mmn"):
    t, k = a.shape
    n = w.shape[1]
    in_specs = [pl.BlockSpec((TM, k), lambda i, j: (i, 0)),
                pl.BlockSpec((k, tn), lambda i, j: (0, j)),
                pl.BlockSpec((1, tn), lambda i, j: (0, j)),
                pl.BlockSpec((NORM_W, NORM_W), lambda i, j: (0, 0))]
    args = [a, w, gains, _block_diag_ones(NORM_W, LANE)]
    if rope is not None:
        tables, pos_fn = rope
        for q in range(3):
            in_specs.append(pl.BlockSpec((1, TM, LANE), lambda i, j, q=q: (q, pos_fn(i), 0)))
            args.append(tables)
    return pl.pallas_call(
        functools.partial(_mmn_kernel, n_norm_tiles=n_norm_tiles, inv_count=1.0 / count,
                          rope=rope is not None),
        grid=(t // TM, n // tn),
        in_specs=in_specs,
        out_specs=pl.BlockSpec((TM, tn), lambda i, j: (i, j)),
        out_shape=jax.ShapeDtypeStruct((t, n), BF16),
        compiler_params=_cparams(2, VMEM_LIMIT),
        name=name,
    )(*args)


def _lane_half_mask(h, width=LANE):
    lane = lax.broadcasted_iota(jnp.int32, (1, width), 1)
    return (lane // (width // 2)) == h


def _pair_attn_kernel(q_ref, k_ref, v_ref, o_ref, *, slot, pairs_blk):
    q = q_ref[0]
    k = k_ref[0]
    v = v_ref[0].astype(BF16)
    masks = [_lane_half_mask(h) for h in range(2)]
    heads = [(pp, h) for pp in range(pairs_blk) for h in range(2)]
    if slot == LANE:
        ss = [_dot_nt(q[:, (2 * pp + h) * LANE:(2 * pp + h + 1) * LANE].astype(BF16),
                      k[:, (2 * pp + h) * LANE:(2 * pp + h + 1) * LANE].astype(BF16)) for pp, h in heads]
    else:
        kb = k.astype(BF16)
        ss = [_dot_nt(jnp.where(masks[h], q[:, pp * LANE:(pp + 1) * LANE], 0).astype(BF16),
                      kb[:, pp * LANE:(pp + 1) * LANE]) for pp, h in heads]
    ps = [jnp.exp(s - jnp.max(s, axis=-1, keepdims=True)) for s in ss]
    os_ = [_dot(p.astype(BF16), jnp.where(masks[h], v[:, pp * LANE:(pp + 1) * LANE], 0).astype(BF16))
           for p, (pp, h) in zip(ps, heads)]
    scaled = [o * (1.0 / jnp.sum(p, axis=-1, keepdims=True)) for o, p in zip(os_, ps)]
    for pp in range(pairs_blk):
        o_ref[0, :, pp * LANE:(pp + 1) * LANE] = (scaled[2 * pp] + scaled[2 * pp + 1]).astype(o_ref.dtype)


def _pair_attn(q_arr, k_arr, v_arr, *, b, q_batch0=0, kv_batch0=0, q_col, k_col, v_col, slot, tq,
               n_pairs, pairs_blk, name):
    nq = q_arr.shape[1]
    nk = k_arr.shape[1]
    qw = 2 * slot * pairs_blk
    vw = LANE * pairs_blk
    qc, kc, vc = q_col // pairs_blk, k_col // pairs_blk, v_col // pairs_blk
    return pl.pallas_call(
        functools.partial(_pair_attn_kernel, slot=slot, pairs_blk=pairs_blk),
        grid=(b, n_pairs // pairs_blk, nq // tq),
        in_specs=[pl.BlockSpec((1, tq, qw), lambda bi, hp, qi: (q_batch0 + bi, qi, qc + hp)),
                  pl.BlockSpec((1, nk, qw), lambda bi, hp, qi: (kv_batch0 + bi, 0, kc + hp)),
                  pl.BlockSpec((1, nk, vw), lambda bi, hp, qi: (kv_batch0 + bi, 0, vc + hp))],
        out_specs=pl.BlockSpec((1, tq, vw), lambda bi, hp, qi: (bi, qi, hp)),
        out_shape=jax.ShapeDtypeStruct((b, nq, n_pairs * LANE), BF16),
        compiler_params=_cparams(3, VMEM_LIMIT),
        name=name,
    )(q_arr, k_arr, v_arr)


NA_QBLOCK_ROWS = 8
NA_WIN = NA_ROWS * GRID_W


def _na_lat_kernel(q_ref, k_ref, v_ref, kc_ref, vc_ref, bias_ref, o_ref, kbf, vbf, *, rows):
    qi = pl.program_id(2)

    @pl.when(qi == 0)
    def _():
        kbf[...] = k_ref[0].astype(BF16)
        v = v_ref[0]
        for h in range(2):
            vbf[h] = jnp.where(_lane_half_mask(h), v, 0.0).astype(BF16)

    q = q_ref[0]
    kc = kc_ref[0].astype(BF16)
    vc = vc_ref[0]
    masks = [_lane_half_mask(h) for h in range(2)]
    qhs = [jnp.where(mask, q, 0.0).astype(BF16) for mask in masks]
    s_cs = [_dot_nt(qh, kc) for qh in qhs]
    units = []
    for rr in range(NA_QBLOCK_ROWS):
        r = qi * NA_QBLOCK_ROWS + rr
        r0 = jnp.clip(r - NA_ROWS // 2, 0, rows - NA_ROWS)
        start = pl.multiple_of(r0 * GRID_W, GRID_W)
        for h in range(2):
            units.append((h, slice(rr * GRID_W, (rr + 1) * GRID_W), start, r - r0))
    s_ls = [_dot_nt(qhs[h][sl], kbf[pl.ds(start, NA_WIN), :]) + bias_ref[pat, h]
            for h, sl, start, pat in units]
    m_cs = [jnp.max(s_c, axis=-1, keepdims=True) for s_c in s_cs]
    p_cs = [jnp.exp(s_c - m_c) for s_c, m_c in zip(s_cs, m_cs)]
    l_cs = [jnp.sum(p_c, axis=-1, keepdims=True) for p_c in p_cs]
    o_cs = [_dot(p_c.astype(BF16), jnp.where(mask, vc, 0.0).astype(BF16)) for p_c, mask in zip(p_cs, masks)]
    ms = [jnp.maximum(jnp.max(s_l, axis=-1, keepdims=True), m_cs[h][sl])
          for s_l, (h, sl, _, _) in zip(s_ls, units)]
    p_ls = [jnp.exp(s_l - m) for s_l, m in zip(s_ls, ms)]
    o_ls = [_dot(p_l.astype(BF16), vbf[h, pl.ds(start, NA_WIN), :])
            for p_l, (h, _, start, _) in zip(p_ls, units)]
    out = [jnp.zeros((GRID_W, LANE), F32) for _ in range(NA_QBLOCK_ROWS)]
    for (h, sl, _, _), m, p_l, o_l in zip(units, ms, p_ls, o_ls):
        a_c = jnp.exp(m_cs[h][sl] - m)
        l = jnp.sum(p_l, axis=-1, keepdims=True) + a_c * l_cs[h][sl]
        rr = sl.start // GRID_W
        out[rr] = out[rr] + (o_l + a_c * o_cs[h][sl]) * (1.0 / l)
    for rr in range(NA_QBLOCK_ROWS):
        o_ref[0, rr * GRID_W:(rr + 1) * GRID_W, :] = out[rr].astype(o_ref.dtype)


def _na_bias_table(rpb):
    col = np.arange(GRID_W)
    col_start = np.clip(col - NA_COLS // 2, 0, GRID_W - NA_COLS)
    kc = np.arange(GRID_W)
    inside = (kc[None, :] >= col_start[:, None]) & (kc[None, :] < col_start[:, None] + NA_COLS)
    pad = GRID_W - NA_COLS
    rpb_pad = jnp.pad(rpb, ((0, 0), (0, 0), (pad, pad + 1)))
    by_col = jnp.stack([rpb_pad[:, :, GRID_W - 1 - c:2 * GRID_W - 1 - c] for c in range(GRID_W)], axis=2)
    by_col = jnp.where(jnp.asarray(inside)[None, None], by_col, NEG_BIG)
    tab = jnp.stack([by_col[:, NA_ROWS - 1 - p:2 * NA_ROWS - 1 - p] for p in range(NA_ROWS)], axis=0)
    tab = tab.transpose(0, 1, 3, 2, 4)
    return tab.reshape(NA_ROWS, NA_HEADS, GRID_W, NA_WIN).astype(F32)


def _na_latent_attn(qkv, kctx, vctx, bias, batch0):
    n = qkv.shape[1]
    b = kctx.shape[0]
    rows = n // GRID_W
    n_pairs = NA_HEADS // 2
    tq = NA_QBLOCK_ROWS * GRID_W
    past = kctx.shape[1]
    return pl.pallas_call(
        functools.partial(_na_lat_kernel, rows=rows),
        grid=(b, n_pairs, n // tq),
        in_specs=[pl.BlockSpec((1, tq, LANE), lambda bi, hp, qi: (batch0 + bi, qi, hp)),
                  pl.BlockSpec((1, n, LANE), lambda bi, hp, qi: (batch0 + bi, 0, n_pairs + hp)),
                  pl.BlockSpec((1, n, LANE), lambda bi, hp, qi: (batch0 + bi, 0, 2 * n_pairs + hp)),
                  pl.BlockSpec((1, past, LANE), lambda bi, hp, qi: (bi, 0, hp)),
                  pl.BlockSpec((1, past, LANE), lambda bi, hp, qi: (bi, 0, hp)),
                  pl.BlockSpec((NA_ROWS, 2, GRID_W, NA_WIN), lambda bi, hp, qi: (0, hp, 0, 0))],
        out_specs=pl.BlockSpec((1, tq, LANE), lambda bi, hp, qi: (bi, qi, hp)),
        out_shape=jax.ShapeDtypeStruct((b, n, n_pairs * LANE), BF16),
        scratch_shapes=[pltpu.VMEM((n, LANE), BF16), pltpu.VMEM((2, n, LANE), BF16)],
        compiler_params=_cparams(3, VMEM_LIMIT),
        name="na_latent_attn",
    )(qkv, qkv, qkv, kctx, vctx, bias)


def _mla_lat_kernel(p_ref, qg_ref, kvg_ref, cq_ref, ckv_ref, kin_ref):
    p = p_ref[...]
    cq = p[:, :MLA_Q_LORA]
    cq = cq * lax.rsqrt(jnp.mean(cq * cq, axis=-1, keepdims=True) + EPS) * qg_ref[...]
    cq_ref[...] = cq.astype(BF16)
    ckv = p[:, MLA_Q_LORA:MLA_Q_LORA + MLA_KV_LORA]
    ckv = ckv * lax.rsqrt(jnp.mean(ckv * ckv, axis=-1, keepdims=True) + EPS) * kvg_ref[...]
    ckv_ref[...] = ckv
    tail = p[:, MLA_Q_LORA + MLA_KV_LORA:]
    kin_ref[...] = jnp.concatenate([ckv, tail], axis=1).astype(BF16)


def _mla_latents(proj, q_a_g, kv_a_g):
    t = proj.shape[0]
    w = proj.shape[1]
    return pl.pallas_call(
        _mla_lat_kernel,
        grid=(t // TM,),
        in_specs=[pl.BlockSpec((TM, w), lambda i: (i, 0)),
                  pl.BlockSpec((1, MLA_Q_LORA), lambda i: (0, 0)),
                  pl.BlockSpec((1, MLA_KV_LORA), lambda i: (0, 0))],
        out_specs=[pl.BlockSpec((TM, MLA_Q_LORA), lambda i: (i, 0)),
                   pl.BlockSpec((TM, MLA_KV_LORA), lambda i: (i, 0)),
                   pl.BlockSpec((TM, 2 * LANE), lambda i: (i, 0))],
        out_shape=[jax.ShapeDtypeStruct((t, MLA_Q_LORA), BF16),
                   jax.ShapeDtypeStruct((t, MLA_KV_LORA), F32),
                   jax.ShapeDtypeStruct((t, 2 * LANE), BF16)],
        compiler_params=_cparams(1),
        name="mla_latents",
    )(proj, q_a_g.reshape(1, -1), kv_a_g.reshape(1, -1))


def _rope_tables(n_pos, n_identity_rows_first=0, n_identity_rows_last=0):
    n_freq = MLA_ROPE_DIM // 4
    inv = ROPE_BASE ** (-jnp.arange(n_freq, dtype=F32) / n_freq)
    t = jnp.arange(n_pos)
    row = (t // GRID_W).astype(F32)
    col = (t % GRID_W).astype(F32)
    ang = jnp.concatenate([row[:, None] * inv, col[:, None] * inv], axis=-1)
    cos, sin = jnp.cos(ang), jnp.sin(ang)
    half = MLA_ROPE_DIM // 2
    ones = jnp.ones((n_pos, MLA_NOPE_DIM), F32)
    zeros = jnp.zeros((n_pos, MLA_NOPE_DIM), F32)
    pad1 = jnp.ones((n_pos, LANE - MLA_QK_DIM), F32)
    pad0 = jnp.zeros((n_pos, LANE - MLA_QK_DIM), F32)
    zh = jnp.zeros((n_pos, half), F32)
    c = jnp.concatenate([ones, cos, cos, pad1], axis=1)
    s1 = jnp.concatenate([zeros, -sin, zh, pad0], axis=1)
    s2 = jnp.concatenate([zeros, zh, sin, pad0], axis=1)
    tab = jnp.stack([c, s1, s2])

    def ident(nrows):
        return jnp.stack([jnp.ones((nrows, LANE), F32), jnp.zeros((nrows, LANE), F32),
                          jnp.zeros((nrows, LANE), F32)])

    parts = []
    if n_identity_rows_first:
        parts.append(ident(n_identity_rows_first))
    parts.append(tab)
    if n_identity_rows_last:
        parts.append(ident(n_identity_rows_last))
    return jnp.concatenate(parts, axis=1)


def _gdn_conv_kernel(x_ref, w_ref, o_ref, pad_ref, *, n):
    j = pl.program_id(1)
    halo = 8
    cw = x_ref.shape[2]
    pad_ref[0:halo, :] = jnp.zeros((halo, cw), F32)
    pad_ref[halo + n:2 * halo + n, :] = jnp.zeros((halo, cw), F32)
    pad_ref[halo:halo + n, :] = x_ref[0]
    acc = jnp.zeros((n, cw), F32)
    for t in range(GDN_CONV_W):
        off = halo - GDN_CONV_W // 2 + t
        acc = acc + pad_ref[off:off + n, :] * w_ref[t:t + 1, :]
    y = acc * jax.nn.sigmoid(acc)
    n_qk_blocks = 2 * GDN_KEY // cw
    n_q_blocks = GDN_KEY // cw
    outs = []
    for s in range(cw // LANE):
        ys = y[:, s * LANE:(s + 1) * LANE]
        ss = jnp.sum(ys * ys, axis=-1, keepdims=True)
        r = jnp.where(j < n_qk_blocks, lax.rsqrt(ss + EPS), 1.0)
        r = r * jnp.where(j < n_q_blocks, GDN_HEAD_DIM ** -0.5, 1.0)
        outs.append(ys * r)
    o_ref[0] = jnp.concatenate(outs, axis=1)


def _gdn_conv(proj3, conv_w, b, batch0):
    n = proj3.shape[1]
    cw = 1024 if n <= 512 else 512
    return pl.pallas_call(
        functools.partial(_gdn_conv_kernel, n=n),
        grid=(b, GDN_QKV // cw),
        in_specs=[pl.BlockSpec((1, n, cw), lambda bi, j: (batch0 + bi, 0, j)),
                  pl.BlockSpec((GDN_CONV_W, cw), lambda bi, j: (0, j))],
        out_specs=pl.BlockSpec((1, n, cw), lambda bi, j: (bi, 0, j)),
        out_shape=jax.ShapeDtypeStruct((b, n, GDN_QKV), F32),
        scratch_shapes=[pltpu.VMEM((n + 16, cw), F32)],
        compiler_params=_cparams(2, VMEM_LIMIT),
        name="gdn_conv",
    )(proj3, conv_w)


def _gdn_gates_kernel(ab_ref, alog_ref, dtb_ref, uf_ref, ub_ref, gf_ref, gb_ref, beta_ref):
    ab = ab_ref[...]
    a = jnp.concatenate([ab[0:8], ab[16:24]], axis=0)
    bb = jnp.concatenate([ab[8:16], ab[24:32]], axis=0)
    z = a + dtb_ref[...]
    sp = jnp.maximum(z, 0.0) + jnp.log(1.0 + jnp.exp(-jnp.abs(z)))
    g = -jnp.exp(alog_ref[...]) * sp
    beta_ref[...] = jax.nn.sigmoid(bb)
    gf_ref[...] = _dot_hi(g, uf_ref[...])
    gb_ref[...] = _dot_hi(g, ub_ref[...])


def _gdn_gates(ab_t, a_log, dt_bias):
    t = ab_t.shape[1]
    tl = 256
    idx = np.arange(tl)
    same = (idx[:, None] // GDN_CHUNK) == (idx[None, :] // GDN_CHUNK)
    uf = jnp.asarray((same & (idx[:, None] <= idx[None, :])).astype(np.float32))
    ub = jnp.asarray((same & (idx[:, None] >= idx[None, :])).astype(np.float32))
    spec = pl.BlockSpec((16, tl), lambda i: (0, i))
    return pl.pallas_call(
        _gdn_gates_kernel,
        grid=(t // tl,),
        in_specs=[pl.BlockSpec((32, tl), lambda i: (0, i)),
                  pl.BlockSpec((16, 1), lambda i: (0, 0)),
                  pl.BlockSpec((16, 1), lambda i: (0, 0)),
                  pl.BlockSpec((tl, tl), lambda i: (0, 0)),
                  pl.BlockSpec((tl, tl), lambda i: (0, 0))],
        out_specs=[spec, spec, spec],
        out_shape=[jax.ShapeDtypeStruct((16, t), F32)] * 3,
        compiler_params=_cparams(1),
        name="gdn_gates",
    )(ab_t, a_log.reshape(16, 1), dt_bias.reshape(16, 1), uf, ub)


def _tri_inverse_many(a_list, block):
    n = a_list[0].shape[0]
    ri = lax.broadcasted_iota(jnp.int32, (n, n), 0)
    ci = lax.broadcasted_iota(jnp.int32, (n, n), 1)
    eye = (ri == ci).astype(F32)
    xs = [(-a).astype(BF16) for a in a_list]
    ts = [eye - a for a in a_list]
    pws = [_dot(x, x) for x in xs]
    n_steps = int(math.log2(block)) - 1
    for step in range(n_steps):
        pwbs = [pw.astype(BF16) for pw in pws]
        if step < n_steps - 1:
            sts = [_dot(jnp.concatenate([pwb, t.astype(BF16)], axis=0), pwb) for pwb, t in zip(pwbs, ts)]
            pws = [st[:n] for st in sts]
            ts = [t + st[n:] for t, st in zip(ts, sts)]
        else:
            ts = [t + _dot(t.astype(BF16), pwb) for t, pwb in zip(ts, pwbs)]
    a_his = [a.astype(BF16) for a in a_list]
    a_los = [(a - a_hi.astype(F32)).astype(BF16) for a, a_hi in zip(a_list, a_his)]
    t_his = [t.astype(BF16) for t in ts]
    t_los = [(t - t_hi.astype(F32)).astype(BF16) for t, t_hi in zip(ts, t_his)]
    at2s = [_dot(jnp.concatenate([a_hi, a_lo], axis=0), t_hi) for a_hi, a_lo, t_hi in zip(a_his, a_los, t_his)]
    at3s = [_dot(a_hi, t_lo) for a_hi, t_lo in zip(a_his, t_los)]
    resids = [((eye - t) - (at2[:n] + at2[n:] + at3)).astype(BF16) for t, at2, at3 in zip(ts, at2s, at3s)]
    return [t + _dot(t_hi, r) for t, t_hi, r in zip(ts, t_his, resids)]


GDN_SLOT = 5 * GDN_HEAD_DIM


GDN_GROUP = 256


def _gdn_prep_kernel(q_ref, k_ref, v_ref, cols_ref, rows_ref, pf_ref, pb_ref, *, heads_blk):
    c = GDN_CHUNK
    n = GDN_GROUP
    ri = lax.broadcasted_iota(jnp.int32, (n, n), 0)
    cj = lax.broadcasted_iota(jnp.int32, (n, n), 1)
    same = (ri // c) == (cj // c)
    lane = lax.broadcasted_iota(jnp.int32, (n, LANE), 1)
    heads = []
    for hh in range(heads_blk):
        lanes = slice(hh * GDN_HEAD_DIM, (hh + 1) * GDN_HEAD_DIM)
        q = q_ref[0, :, lanes]
        k = k_ref[0, :, lanes]
        kb16 = k.astype(BF16)
        heads.append((q, k, v_ref[0, :, lanes], _dot_nt(kb16, kb16), _dot_nt(q.astype(BF16), kb16)))
    chains = []
    for hh in range(heads_blk):
        cols = cols_ref[0, hh, 0]
        rows = rows_ref[0, hh, 0]
        kk = heads[hh][3]
        for direction in range(2):
            gcol = cols[:, 2 * direction:2 * direction + 1]
            bcol = cols[:, 2 * direction + 1:2 * direction + 2]
            glast = cols[:, 4 + direction:5 + direction]
            grow = rows[direction:direction + 1, :]
            if direction == 0:
                incl, strict = same & (ri >= cj), same & (ri > cj)
            else:
                incl, strict = same & (ri <= cj), same & (ri < cj)
            decay = jnp.where(incl, jnp.exp(jnp.where(incl, gcol - grow, 0.0)), 0.0)
            chains.append((hh, direction, gcol, bcol, glast, decay,
                           jnp.where(strict, kk * bcol * decay, 0.0)))
    tmats = _tri_inverse_many([ch[6] for ch in chains], c)
    uws = []
    for (hh, direction, gcol, bcol, glast, decay, _), tmat in zip(chains, tmats):
        q, k, v, _, _ = heads[hh]
        rhs = jnp.concatenate([v * bcol, k * (bcol * jnp.exp(gcol))], axis=1).astype(BF16)
        uws.append(_dot(tmat.astype(BF16), rhs))
    for (hh, direction, gcol, bcol, glast, decay, _), uw in zip(chains, uws):
        q, k, v, _, qk = heads[hh]
        attn = qk * decay
        folded = attn[:, :LANE] + attn[:, LANE:]
        folded = folded + pltpu.roll(folded, c, 1)
        egl = jnp.exp(glast)
        egl_hi = egl.astype(BF16).astype(F32)
        tail = jnp.where(lane < c + c // 2, egl_hi, egl - egl_hi)
        a128 = jnp.where(lane < c, folded, tail)
        slot = jnp.concatenate([uw, q * jnp.exp(gcol), k * jnp.exp(glast - gcol), a128], axis=1)
        out_ref = pf_ref if direction == 0 else pb_ref
        out_ref[0, :, hh * GDN_SLOT:(hh + 1) * GDN_SLOT] = slot.astype(BF16)


def _gdn_prep(qkv, cols, rows):
    b, n, _ = qkv.shape
    hb = 4
    nhb = GDN_HEADS // hb
    tb = GDN_GROUP
    w = hb * GDN_HEAD_DIM
    p_spec = pl.BlockSpec((1, tb, hb * GDN_SLOT), lambda bi, h, t: (bi, t, h))
    return pl.pallas_call(
        functools.partial(_gdn_prep_kernel, heads_blk=hb),
        grid=(b, nhb, n // tb),
        in_specs=[pl.BlockSpec((1, tb, w), lambda bi, h, t: (bi, t, h)),
                  pl.BlockSpec((1, tb, w), lambda bi, h, t: (bi, t, nhb + h)),
                  pl.BlockSpec((1, tb, w), lambda bi, h, t: (bi, t, 2 * nhb + h)),
                  pl.BlockSpec((1, hb, 1, tb, 8), lambda bi, h, t: (bi, h, t, 0, 0)),
                  pl.BlockSpec((1, hb, 1, 8, tb), lambda bi, h, t: (bi, h, t, 0, 0))],
        out_specs=[p_spec, p_spec],
        out_shape=[jax.ShapeDtypeStruct((b, n, GDN_HEADS * GDN_SLOT), BF16)] * 2,
        compiler_params=_cparams(3, VMEM_LIMIT),
        name="gdn_prep",
    )(qkv, qkv, qkv, cols, rows)


def _gdn_scan_kernel(pf_ref, pb_ref, s0f_ref, s0b_ref, of_ref, ob_ref, sf_ref, sb_ref, s_scr,
                     *, heads_blk, chunks_blk):
    c = GDN_CHUNK
    hd = GDN_HEAD_DIM
    t = pl.program_id(2)

    @pl.when(t == 0)
    def _():
        for hh in range(heads_blk):
            s_scr[2 * hh] = s0f_ref[0, hh]
            s_scr[2 * hh + 1] = s0b_ref[0, hh]

    def body(i, carry):
        tf = pl.multiple_of(i * c, c)
        tbk = pl.multiple_of((chunks_blk - 1 - i) * c, c)
        chains = [(pf_ref, of_ref, tf, hh, 2 * hh) for hh in range(heads_blk)]
        chains += [(pb_ref, ob_ref, tbk, hh, 2 * hh + 1) for hh in range(heads_blk)]
        states = [s_scr[ci] for _, _, _, _, ci in chains]
        wss = []
        for (p_ref, _, t0, hh, _), state in zip(chains, states):
            base = hh * GDN_SLOT
            wq = jnp.concatenate([p_ref[0, pl.ds(t0, c), base + hd:base + 2 * hd],
                                  p_ref[0, pl.ds(t0, c), base + 2 * hd:base + 3 * hd]], axis=0)
            wss.append(_dot(wq, state.astype(BF16)))
        v_news = []
        for (p_ref, _, t0, hh, _), ws in zip(chains, wss):
            base = hh * GDN_SLOT
            v_news.append((p_ref[0, pl.ds(t0, c), base:base + hd].astype(F32) - ws[:c]).astype(BF16))
        a128s = [p_ref[0, pl.ds(t0, c), hh * GDN_SLOT + 4 * hd:hh * GDN_SLOT + 5 * hd]
                 for p_ref, _, t0, hh, _ in chains]
        outs = [ws[c:] + _dot(a128[:, :c], v_new) for ws, a128, v_new in zip(wss, a128s, v_news)]
        upds = [_dot_tn(p_ref[0, pl.ds(t0, c), hh * GDN_SLOT + 3 * hd:hh * GDN_SLOT + 4 * hd], v_new)
                for (p_ref, _, t0, hh, _), v_new in zip(chains, v_news)]
        for (_, o_ref, t0, hh, ci), state, a128, o, upd in zip(chains, states, a128s, outs, upds):
            egl = a128[0:1, c:c + 1].astype(F32) + a128[0:1, c + c // 2:c + c // 2 + 1].astype(F32)
            s_scr[ci] = state * egl + upd
            o_ref[0, pl.ds(t0, c), hh * hd:(hh + 1) * hd] = o
        return carry

    lax.fori_loop(0, chunks_blk, body, 0)

    @pl.when(t == pl.num_programs(2) - 1)
    def _():
        for hh in range(heads_blk):
            sf_ref[0, hh] = s_scr[2 * hh]
            sb_ref[0, hh] = s_scr[2 * hh + 1]


def _gdn_scan(pf, pb, s0f, s0b):
    b, n, _ = pf.shape
    hs = 8
    nhs = GDN_HEADS // hs
    tb = min(n, 512)
    nt = n // tb
    cb = tb // GDN_CHUNK
    st_spec = pl.BlockSpec((1, hs, GDN_HEAD_DIM, GDN_HEAD_DIM), lambda bi, h, t: (bi, h, 0, 0))
    return pl.pallas_call(
        functools.partial(_gdn_scan_kernel, heads_blk=hs, chunks_blk=cb),
        grid=(b, nhs, nt),
        in_specs=[pl.BlockSpec((1, tb, hs * GDN_SLOT), lambda bi, h, t: (bi, t, h)),
                  pl.BlockSpec((1, tb, hs * GDN_SLOT), lambda bi, h, t: (bi, nt - 1 - t, h)),
                  st_spec, st_spec],
        out_specs=[pl.BlockSpec((1, tb, hs * GDN_HEAD_DIM), lambda bi, h, t: (bi, t, h)),
                   pl.BlockSpec((1, tb, hs * GDN_HEAD_DIM), lambda bi, h, t: (bi, nt - 1 - t, h)),
                   st_spec, st_spec],
        out_shape=[jax.ShapeDtypeStruct((b, n, GDN_KEY), F32)] * 2
        + [jax.ShapeDtypeStruct((b, GDN_HEADS, GDN_HEAD_DIM, GDN_HEAD_DIM), F32)] * 2,
        scratch_shapes=[pltpu.VMEM((2 * hs, GDN_HEAD_DIM, GDN_HEAD_DIM), F32)],
        compiler_params=_cparams(3, VMEM_LIMIT),
        name="gdn_scan",
    )(pf, pb, s0f, s0b)


def _gdn_gate_tables(gf, gb, beta, b, n):
    nc = n // GDN_CHUNK
    ng = n // GDN_GROUP

    def split(x):
        return x.reshape(2, GDN_HEADS, b, nc, GDN_CHUNK)

    def group(x):
        return x.reshape(GDN_HEADS, b, ng, GDN_GROUP)

    gf5, gb5, be5 = split(gf), split(gb), split(beta)
    gcf, gcb = group(gf5[0]), group(gb5[1])
    glf = group(jnp.broadcast_to(gf5[0][..., GDN_CHUNK - 1:], gf5[0].shape))
    glb = group(jnp.broadcast_to(gb5[1][..., :1], gb5[1].shape))
    zero = jnp.zeros_like(gcf)
    col_list = [gcf, group(be5[0]), gcb, group(be5[1]), glf, glb, zero, zero]
    cols_t = jnp.stack(col_list, axis=-2).transpose(1, 0, 2, 3, 4)
    cols = jnp.swapaxes(cols_t, -1, -2)
    row_list = [gcf, gcb, zero, zero, zero, zero, zero, zero]
    rows = jnp.stack(row_list, axis=-2).transpose(1, 0, 2, 3, 4)
    return cols, rows


def _gdn_out_kernel(ofp_ref, obp_ref, ofs_ref, obs_ref, z_ref, g_ref, y_ref, o_scr, *, n_prompt_tiles):
    i = pl.program_id(0)

    @pl.when(i < n_prompt_tiles)
    def _():
        o_scr[...] = ofp_ref[...] + obp_ref[...]

    @pl.when(i >= n_prompt_tiles)
    def _():
        o_scr[...] = ofs_ref[...] + obs_ref[...]

    o = o_scr[...]
    z = z_ref[...]
    gate = z * jax.nn.sigmoid(z)
    outs = []
    for h in range(GDN_HEADS):
        sl = slice(h * GDN_HEAD_DIM, (h + 1) * GDN_HEAD_DIM)
        oh = o[:, sl]
        r = lax.rsqrt(jnp.mean(oh * oh, axis=-1, keepdims=True) + EPS)
        outs.append(oh * r * g_ref[...] * gate[:, sl])
    y_ref[...] = jnp.concatenate(outs, axis=1).astype(BF16)


def _gdn_out(o_prompt, o_sample, proj, norm_g):
    t = proj.shape[0]
    npt = o_prompt[0].shape[0] // TM
    z_block = GDN_QKV // GDN_KEY
    p_spec = pl.BlockSpec((TM, GDN_KEY), lambda i: (jnp.minimum(i, npt - 1), 0))
    s_spec = pl.BlockSpec((TM, GDN_KEY), lambda i: (jnp.maximum(i - npt, 0), 0))
    return pl.pallas_call(
        functools.partial(_gdn_out_kernel, n_prompt_tiles=npt),
        grid=(t // TM,),
        in_specs=[p_spec, p_spec, s_spec, s_spec,
                  pl.BlockSpec((TM, GDN_KEY), lambda i: (i, z_block)),
                  pl.BlockSpec((1, GDN_HEAD_DIM), lambda i: (0, 0))],
        out_specs=pl.BlockSpec((TM, GDN_KEY), lambda i: (i, 0)),
        out_shape=jax.ShapeDtypeStruct((t, GDN_KEY), BF16),
        scratch_shapes=[pltpu.VMEM((TM, GDN_KEY), F32)],
        compiler_params=_cparams(1, VMEM_LIMIT),
        name="gdn_out",
    )(*o_prompt, *o_sample, proj, norm_g.reshape(1, -1))


def _store_token_tiles(ref, x):
    for s in range(TOKEN_TILE_ROWS):
        ref[pl.ds(s, x.shape[0], stride=TOKEN_TILE_ROWS), :] = x[:, s * LANE:(s + 1) * LANE]


def _load_token_tiles(ref, rows):
    return [ref[pl.ds(s, rows, stride=TOKEN_TILE_ROWS), :] for s in range(TOKEN_TILE_ROWS)]


def _router_kernel(x_ref, g_ref, mod_ref, wr_ref, br_ref, h_ref, idx_ref, w_ref, cnt_ref):
    h = _modulate(x_ref[...], g_ref[...], mod_ref, 3, 4)
    _store_token_tiles(h_ref, h)
    logits = lax.dot_general(wr_ref[...], h, (((1,), (1,)), ((), ())),
                             preferred_element_type=F32, precision=HIGHEST) + br_ref[...]
    ne, tm = logits.shape
    iota = lax.broadcasted_iota(jnp.int32, (ne, tm), 0)
    vals, idxs = [], []
    cur = logits
    for _ in range(TOP_K):
        m = jnp.max(cur, axis=0, keepdims=True)
        idx = jnp.min(jnp.where(cur == m, iota, ne), axis=0, keepdims=True)
        vals.append(m)
        idxs.append(idx)
        cur = jnp.where(iota == idx, -jnp.inf, cur)
    es = [jnp.exp(v - vals[0]) for v in vals]
    tot = es[0] + es[1] + es[2] + es[3]
    cnt = jnp.zeros((ne, 1), jnp.int32)
    for kk in range(TOP_K):
        idx_ref[kk:kk + 1, :] = idxs[kk]
        w_ref[kk:kk + 1, :] = es[kk] / tot
        cnt = cnt + jnp.sum((iota == idxs[kk]).astype(jnp.int32), axis=1, keepdims=True)

    @pl.when(pl.program_id(0) == 0)
    def _():
        cnt_ref[...] = jnp.zeros(cnt_ref.shape, jnp.int32)

    cnt_ref[...] += cnt


def _router(x, g, mod, w_router_t, b_router):
    t, d = x.shape
    return pl.pallas_call(
        _router_kernel,
        grid=(t // TM,),
        in_specs=[pl.BlockSpec((TM, d), lambda i: (i, 0)),
                  pl.BlockSpec((1, d), lambda i: (0, 0)),
                  pl.BlockSpec((1, 6, 1, d), lambda i: (_group_of_tile(i, TM), 0, 0, 0)),
                  pl.BlockSpec((N_EXPERTS, d), lambda i: (0, 0)),
                  pl.BlockSpec((N_EXPERTS, 1), lambda i: (0, 0))],
        out_specs=[pl.BlockSpec((TM * TOKEN_TILE_ROWS, LANE), lambda i: (i, 0)),
                   pl.BlockSpec((TOP_K, TM), lambda i: (0, i)),
                   pl.BlockSpec((TOP_K, TM), lambda i: (0, i)),
                   pl.BlockSpec((N_EXPERTS, 1), lambda i: (0, 0))],
        out_shape=[jax.ShapeDtypeStruct((t * TOKEN_TILE_ROWS, LANE), F32),
                   jax.ShapeDtypeStruct((TOP_K, t), jnp.int32),
                   jax.ShapeDtypeStruct((TOP_K, t), F32),
                   jax.ShapeDtypeStruct((N_EXPERTS, 1), jnp.int32)],
        compiler_params=_cparams(1, VMEM_LIMIT),
        name="moe_router",
    )(x, g, mod, w_router_t, b_router.reshape(N_EXPERTS, 1))


def _tile_rows(i):
    return pl.ds(pl.multiple_of(i * TOKEN_TILE_ROWS, TOKEN_TILE_ROWS), TOKEN_TILE_ROWS)


ITEM_FIRST, ITEM_LAST, ITEM_VALID, ITEM_FINAL = 1, 2, 4, 8
MOE_ROW_TILES = MOE_ROWS // MOE_TM
MOE_TILE_SUBLANES = MOE_TM * TOKEN_TILE_ROWS


def _issue_token_dmas(make_copy):
    def issue(g, carry):
        for u in range(ROW_DMA_UNROLL):
            make_copy(g * ROW_DMA_UNROLL + u).start(priority=u % 2)
        return carry

    lax.fori_loop(0, MOE_TM // ROW_DMA_UNROLL, issue, 0)


def _experts_kernel(tile_ref, exp_ref, lo_ref, hi_ref, flag_ref, src_ref, src_next_ref, dst_ref,
                    h_hbm, wgu_ref, bgu_ref, wd_ref, bd_ref, wrow_ref, y_hbm,
                    wgu_bf, wd_bf, x_bf, acc, xg, yo, gsem, ssem):
    j = pl.program_id(0)
    t = tile_ref[j]
    e = exp_ref[j]
    prev = exp_ref[jnp.maximum(j - 1, 0)]
    flags = flag_ref[j]
    first = (flags & ITEM_FIRST) != 0
    last = (flags & ITEM_LAST) != 0
    odd = lax.rem(t, 2)

    def gather_copy(idx_ref, slot, r):
        return pltpu.make_async_copy(h_hbm.at[_tile_rows(idx_ref[0, 0, r])], xg.at[slot, _tile_rows(r)],
                                     gsem.at[slot])

    def gather_all(slot):
        return pltpu.make_async_copy(h_hbm.at[pl.ds(0, MOE_TILE_SUBLANES)], xg.at[slot], gsem.at[slot])

    def scatter_copy(slot, r):
        return pltpu.make_async_copy(yo.at[slot, _tile_rows(r)], y_hbm.at[_tile_rows(dst_ref[0, 0, r])],
                                     ssem.at[slot])

    def scatter_all(slot):
        return pltpu.make_async_copy(yo.at[slot], y_hbm.at[pl.ds(0, MOE_TILE_SUBLANES)], ssem.at[slot])

    @pl.when(j == 0)
    def _():
        _issue_token_dmas(functools.partial(gather_copy, src_ref, 0))

    @pl.when((j == 0) | (e != prev))
    def _():
        wgu_bf[...] = wgu_ref[0, 0].astype(BF16)
        wd_bf[...] = wd_ref[0, 0].astype(BF16)

    for slot in range(2):
        @pl.when(first & (odd == slot))
        def _(slot=slot):
            gather_all(slot).wait()

            @pl.when(t + 1 < MOE_ROW_TILES)
            def _():
                _issue_token_dmas(functools.partial(gather_copy, src_next_ref, 1 - slot))

            for s, xs in enumerate(_load_token_tiles(xg.at[slot], MOE_TM)):
                x_bf[:, s * LANE:(s + 1) * LANE] = xs.astype(BF16)

    @pl.when((flags & ITEM_VALID) != 0)
    def _():
        gu = _dot(x_bf[...], wgu_bf[...]) + bgu_ref[0]
        glu = jnp.minimum(gu[:, :D_EXPERT], SWIGLU_LIMIT)
        lin = jnp.clip(gu[:, D_EXPERT:], -SWIGLU_LIMIT, SWIGLU_LIMIT)
        act = (lin + 1.0) * glu * jax.nn.sigmoid(SWIGLU_ALPHA * glu)
        y = (_dot(act.astype(BF16), wd_bf[...]) + bd_ref[0]) * wrow_ref[:, 0:1]
        row = lax.broadcasted_iota(jnp.int32, (MOE_TM, 1), 0)
        mine = (row >= lo_ref[j]) & (row < hi_ref[j])

        @pl.when(first)
        def _():
            acc[...] = jnp.where(mine, y, 0.0)

        @pl.when(jnp.logical_not(first))
        def _():
            acc[...] = jnp.where(mine, y, acc[...])

    for slot in range(2):
        @pl.when(last & (odd == slot))
        def _(slot=slot):
            @pl.when(t >= 2)
            def _():
                scatter_all(slot).wait()

            _store_token_tiles(yo.at[slot], acc[...])
            _issue_token_dmas(functools.partial(scatter_copy, slot))

            @pl.when((flags & ITEM_FINAL) != 0)
            def _():
                scatter_all(1 - slot).wait()
                scatter_all(slot).wait()


def _experts(plan, h_tiles, src_tok, dst_tok, layer, w_gate_up, b_gate_up, w_down, b_down, w_sorted):
    d = D_MODEL
    a = src_tok.shape[0]

    def tile_map(j, tile, exp, lo, hi, flag):
        return (tile[j], 0)

    def idx_map(j, tile, exp, lo, hi, flag):
        return (tile[j], 0, 0)

    def idx_next_map(j, tile, exp, lo, hi, flag):
        return (jnp.minimum(tile[j] + 1, MOE_ROW_TILES - 1), 0, 0)

    def exp_map(j, tile, exp, lo, hi, flag):
        return (exp[j], 0, 0)

    def w_map(j, tile, exp, lo, hi, flag):
        return (layer, exp[j], 0, 0)

    idx_block = (1, 1, MOE_TM)
    src3 = src_tok.reshape(MOE_ROW_TILES, 1, MOE_TM)
    grid_spec = pltpu.PrefetchScalarGridSpec(
        num_scalar_prefetch=5,
        grid=(MOE_ITEMS,),
        in_specs=[pl.BlockSpec(idx_block, idx_map, memory_space=pltpu.SMEM),
                  pl.BlockSpec(idx_block, idx_next_map, memory_space=pltpu.SMEM),
                  pl.BlockSpec(idx_block, idx_map, memory_space=pltpu.SMEM),
                  pl.BlockSpec(memory_space=pl.ANY),
                  pl.BlockSpec((1, 1, d, 2 * D_EXPERT), w_map),
                  pl.BlockSpec((1, 1, 2 * D_EXPERT), exp_map),
                  pl.BlockSpec((1, 1, D_EXPERT, d), w_map),
                  pl.BlockSpec((1, 1, d), exp_map),
                  pl.BlockSpec((MOE_TM, LANE), tile_map)],
        out_specs=pl.BlockSpec(memory_space=pl.ANY),
        scratch_shapes=[pltpu.VMEM((d, 2 * D_EXPERT), BF16), pltpu.VMEM((D_EXPERT, d), BF16),
                        pltpu.VMEM((MOE_TM, d), BF16), pltpu.VMEM((MOE_TM, d), F32),
                        pltpu.VMEM((2, MOE_TILE_SUBLANES, LANE), F32),
                        pltpu.VMEM((2, MOE_TILE_SUBLANES, LANE), F32),
                        pltpu.SemaphoreType.DMA((2,)), pltpu.SemaphoreType.DMA((2,))],
    )
    return pl.pallas_call(
        _experts_kernel,
        grid_spec=grid_spec,
        out_shape=jax.ShapeDtypeStruct((a * TOKEN_TILE_ROWS, LANE), F32),
        compiler_params=_cparams(1, VMEM_LIMIT),
        name="moe_experts",
    )(*plan, src3, src3, dst_tok.reshape(MOE_ROW_TILES, 1, MOE_TM), h_tiles, w_gate_up,
      b_gate_up.reshape(N_EXPERTS, 1, -1), w_down, b_down.reshape(N_EXPERTS, 1, -1), w_sorted)


def _moe_sum_kernel(x_ref, gate_ref, y0, y1, y2, y3, o_ref, ysum):
    ysum[...] = y0[...] + y1[...] + y2[...] + y3[...]
    gate = gate_ref[0, 0]
    for s, ys in enumerate(_load_token_tiles(ysum, TM)):
        cols = slice(s * LANE, (s + 1) * LANE)
        o_ref[:, cols] = x_ref[:, cols] + gate[:, cols] * ys


def _moe_sum(x, mod, y_tok):
    t, d = x.shape
    nt = t // TM
    blk = TM * TOKEN_TILE_ROWS
    y_specs = [pl.BlockSpec((blk, LANE), lambda i, k=k: (k * nt + i, 0)) for k in range(TOP_K)]
    return pl.pallas_call(
        _moe_sum_kernel,
        grid=(nt,),
        in_specs=[pl.BlockSpec((TM, d), lambda i: (i, 0)),
                  pl.BlockSpec((1, 1, 1, d), lambda i: (_group_of_tile(i, TM), 5, 0, 0))] + y_specs,
        out_specs=pl.BlockSpec((TM, d), lambda i: (i, 0)),
        out_shape=jax.ShapeDtypeStruct((t, d), F32),
        scratch_shapes=[pltpu.VMEM((blk, LANE), F32)],
        compiler_params=_cparams(1, VMEM_LIMIT),
        name="moe_sum",
    )(x, mod, y_tok, y_tok, y_tok, y_tok)


def _pick(onehot, vec):
    return jnp.sum(jnp.where(onehot, vec[None, :], 0), axis=1)


def _moe_plan(counts):
    ends = jnp.cumsum(counts)
    starts = ends - counts
    first_tile = starts // MOE_TM
    n_items_e = jnp.where(counts > 0, (ends - 1) // MOE_TM - first_tile + 1, 0)
    item_ends = jnp.cumsum(n_items_e)
    item_starts = item_ends - n_items_e
    n_items = item_ends[-1]
    j = jnp.arange(MOE_ITEMS, dtype=jnp.int32)
    jc = jnp.clip(j, 0, jnp.maximum(n_items - 1, 0))
    e_j = jnp.minimum(jnp.sum(item_ends[None, :] <= jc[:, None], axis=1), N_EXPERTS - 1).astype(jnp.int32)
    onehot = e_j[:, None] == jnp.arange(N_EXPERTS, dtype=jnp.int32)[None, :]
    tile_j = _pick(onehot, first_tile) + (jc - _pick(onehot, item_starts))
    lo = jnp.maximum(_pick(onehot, starts), tile_j * MOE_TM) - tile_j * MOE_TM
    hi = jnp.minimum(_pick(onehot, ends), (tile_j + 1) * MOE_TM) - tile_j * MOE_TM
    valid = j < n_items
    prev_tile = jnp.concatenate([jnp.full((1,), -1, jnp.int32), tile_j[:-1]])
    next_tile = jnp.concatenate([tile_j[1:], jnp.full((1,), -1, jnp.int32)])
    first = valid & (tile_j != prev_tile)
    last = valid & ((tile_j != next_tile) | (j == n_items - 1))
    flags = (first * ITEM_FIRST + last * ITEM_LAST + valid * ITEM_VALID
             + (j == n_items - 1) * ITEM_FINAL)
    i32 = lambda v: v.astype(jnp.int32)
    return i32(tile_j), e_j, i32(lo), i32(jnp.where(valid, hi, lo)), i32(flags)


def _moe_layer(x, g_ffn, mod, layer, w_router, b_router, w_gate_up, b_gate_up, w_down, b_down):
    t = x.shape[0]
    a = TOP_K * t
    h, idx_t, w_t, counts = _router(x, g_ffn.reshape(1, -1), mod, w_router.T, b_router)
    _, order, w_sorted = lax.sort((idx_t.reshape(a), jnp.arange(a, dtype=jnp.int32), w_t.reshape(a)),
                                  num_keys=1, is_stable=False)
    plan = _moe_plan(counts.reshape(N_EXPERTS))
    y_tok = _experts(plan, h, order % t, order, layer, w_gate_up, b_gate_up, w_down, b_down,
                     jnp.broadcast_to(w_sorted[:, None], (a, LANE)))
    return _moe_sum(x, mod, y_tok)


def _na_layer(x, g_mix, mod, cache_k, cache_v, w_in, q_g, k_g, rpb, w_out):
    hd = NA_HEADS * NA_HEAD_DIM
    gains = jnp.concatenate([jnp.tile(q_g * NA_HEAD_DIM ** -0.5, NA_HEADS), jnp.tile(k_g, NA_HEADS),
                             jnp.ones((hd,), F32)]).reshape(1, -1)
    tn = 1024
    qkv = _modmm(x, g_mix.reshape(1, -1), mod, w_in.astype(BF16), shift_idx=0, scale_idx=1, tn=tn,
                 norm=(2 * hd // tn, gains, NA_HEAD_DIM), name="na_qkv")
    qkv_by_seq = qkv.reshape(TOKENS // SEQ, SEQ, 3 * hd)
    qkv_by_dec = qkv.reshape(TOKENS // DEC_SEQ, DEC_SEQ, 3 * hd)
    n_pairs = NA_HEADS // 2
    o_p = _pair_attn(qkv_by_seq, qkv_by_seq, qkv_by_seq, b=BATCH, q_col=0, k_col=n_pairs,
                     v_col=2 * n_pairs, slot=NA_HEAD_DIM, tq=SEQ, n_pairs=n_pairs, pairs_blk=n_pairs,
                     name="na_context_attn")
    bias = _na_bias_table(rpb)
    o_s = _na_latent_attn(qkv_by_dec, cache_k.reshape(DEC_BATCH, PAST_LEN, hd),
                          cache_v.reshape(DEC_BATCH, PAST_LEN, hd), bias, PROMPT_TOK // DEC_SEQ)
    o = jnp.concatenate([o_p.reshape(PROMPT_TOK, hd), o_s.reshape(SAMPLE_TOK, hd)], axis=0)
    x = _mm_res(o, w_out.astype(BF16), x, mod, 2, name="na_out")
    new_k = qkv[:PROMPT_TOK, hd:2 * hd].reshape(BATCH, SEQ, NA_HEADS, NA_HEAD_DIM)
    new_v = qkv[:PROMPT_TOK, 2 * hd:].reshape(BATCH, SEQ, NA_HEADS, NA_HEAD_DIM)
    return x, new_k, new_v


def _mla_weights(w_q_up, w_kv_up, q_g, k_g):
    wq = w_q_up.reshape(MLA_Q_LORA, MLA_HEADS, MLA_QK_DIM)
    wq = jnp.pad(wq, ((0, 0), (0, 0), (0, LANE - MLA_QK_DIM))).reshape(MLA_Q_LORA, MLA_HEADS * LANE)
    kv = w_kv_up.reshape(MLA_KV_LORA, MLA_HEADS, MLA_NOPE_DIM + MLA_V_DIM)
    wk_top = jnp.pad(kv[:, :, :MLA_NOPE_DIM], ((0, 0), (0, 0), (0, LANE - MLA_NOPE_DIM)))
    eye = jnp.eye(MLA_ROPE_DIM, dtype=F32)
    pe_rows = jnp.pad(eye, ((0, 0), (MLA_NOPE_DIM, LANE - MLA_QK_DIM)))
    pe_rows = jnp.broadcast_to(pe_rows[:, None, :], (MLA_ROPE_DIM, MLA_HEADS, LANE))
    k_in = 2 * LANE
    wk = jnp.concatenate([wk_top, pe_rows,
                          jnp.zeros((k_in - MLA_KV_LORA - MLA_ROPE_DIM, MLA_HEADS, LANE), F32)], axis=0)
    wk = wk.reshape(k_in, MLA_HEADS * LANE)
    wv = jnp.pad(kv[:, :, MLA_NOPE_DIM:].reshape(MLA_KV_LORA, MLA_HEADS * MLA_V_DIM),
                 ((0, k_in - MLA_KV_LORA), (0, 0)))
    wkv = jnp.concatenate([wk, wv], axis=1)
    pad_g = LANE - MLA_QK_DIM
    qgain = jnp.tile(jnp.pad(q_g * MLA_QK_DIM ** -0.5, (0, pad_g)), MLA_HEADS).reshape(1, -1)
    kgain = jnp.tile(jnp.pad(k_g, (0, pad_g)), MLA_HEADS)
    kvgain = jnp.concatenate([kgain, jnp.ones((MLA_HEADS * MLA_V_DIM,), F32)]).reshape(1, -1)
    return wq.astype(BF16), wkv.astype(BF16), qgain, kvgain


def _mla_layer(x, g_mix, mod, cache_ckv, cache_kpe, w_in, q_a_g, kv_a_g, w_q_up, w_kv_up, q_g, k_g,
               w_out):
    n_lat = MLA_Q_LORA + MLA_KV_LORA + MLA_ROPE_DIM
    w_in_p = jnp.pad(w_in, ((0, 0), (0, 512 - n_lat))).astype(BF16)
    proj = _modmm(x, g_mix.reshape(1, -1), mod, w_in_p, shift_idx=0, scale_idx=1, tn=512,
                  name="mla_in")
    cq, ckv, kin = _mla_latents(proj, q_a_g, kv_a_g)
    wq, wkv, qgain, kvgain = _mla_weights(w_q_up, w_kv_up, q_g, k_g)
    n_ktiles = MLA_HEADS * LANE // MMN_TN
    npt = PROMPT_TOK // TM
    spb = DEC_SEQ // TM
    q_tab = _rope_tables(DEC_SEQ, n_identity_rows_first=TM)
    q = _mmn(cq, wq, qgain, n_norm_tiles=n_ktiles, count=MLA_QK_DIM,
             rope=(q_tab, lambda i: jnp.where(i < npt, 0, 1 + (i - npt) % spb)), name="mla_q")
    kv_p = _mmn(kin[:PROMPT_TOK], wkv, kvgain, n_norm_tiles=n_ktiles, count=MLA_QK_DIM,
                name="mla_kv_prompt")
    kin_ctx = jnp.concatenate(
        [cache_ckv, cache_kpe, jnp.zeros((DEC_BATCH, PAST_LEN, 2 * LANE - MLA_KV_LORA - MLA_ROPE_DIM),
                                         F32)], axis=-1).astype(BF16)
    kin_s = jnp.concatenate([kin[PROMPT_TOK:].reshape(DEC_BATCH, DEC_SEQ, 2 * LANE), kin_ctx], axis=1)
    n_keys = DEC_SEQ + PAST_LEN
    kpb = n_keys // TM
    k_tab = _rope_tables(DEC_SEQ, n_identity_rows_last=PAST_LEN)
    kv_s = _mmn(kin_s.reshape(DEC_BATCH * n_keys, 2 * LANE), wkv, kvgain, n_norm_tiles=n_ktiles,
                count=MLA_QK_DIM, rope=(k_tab, lambda i: i % kpb), name="mla_kv_sample")
    hw = MLA_HEADS * LANE
    n_pairs = MLA_HEADS // 2
    q_by_seq = q.reshape(TOKENS // SEQ, SEQ, hw)
    q_by_dec = q.reshape(TOKENS // DEC_SEQ, DEC_SEQ, hw)
    kv_p3 = kv_p.reshape(BATCH, SEQ, -1)
    kv_s3 = kv_s.reshape(DEC_BATCH, n_keys, -1)
    o_p = _pair_attn(q_by_seq, kv_p3, kv_p3, b=BATCH, q_col=0, k_col=0, v_col=hw // LANE, slot=LANE,
                     tq=SEQ, n_pairs=n_pairs, pairs_blk=n_pairs, name="mla_context_attn")
    o_s = _pair_attn(q_by_dec, kv_s3, kv_s3, b=DEC_BATCH, q_batch0=PROMPT_TOK // DEC_SEQ, q_col=0,
                     k_col=0, v_col=hw // LANE, slot=LANE, tq=256, n_pairs=n_pairs, pairs_blk=2,
                     name="mla_latent_attn")
    ov = MLA_HEADS * MLA_V_DIM
    o = jnp.concatenate([o_p.reshape(PROMPT_TOK, ov), o_s.reshape(SAMPLE_TOK, ov)], axis=0)
    x = _mm_res(o, w_out.astype(BF16), x, mod, 2, name="mla_out")
    new_ckv = ckv[:PROMPT_TOK].reshape(BATCH, SEQ, MLA_KV_LORA)
    new_kpe = proj[:PROMPT_TOK, MLA_Q_LORA + MLA_KV_LORA:n_lat].reshape(BATCH, SEQ, MLA_ROPE_DIM)
    return x, new_ckv, new_kpe


def _gdn_layer(x, g_mix, mod, s_fwd, s_bwd, w_in, conv_w, a_log, dt_bias, norm_g, w_out):
    n_main = GDN_QKV + GDN_KEY
    gm = g_mix.reshape(1, -1)
    proj = _modmm(x, gm, mod, w_in[:, :n_main].astype(BF16), shift_idx=0, scale_idx=1, tn=1024,
                  name="gdn_in")
    w_ab = jnp.pad(w_in[:, n_main:], ((0, 0), (0, LANE - 4 * GDN_HEADS))).astype(BF16)
    ab = _modmm(x, gm, mod, w_ab, shift_idx=0, scale_idx=1, tn=LANE, name="gdn_in_ab")
    gf, gb, beta = _gdn_gates(ab[:, :4 * GDN_HEADS].T, a_log, dt_bias)
    outs = []
    states = []
    for lo, hi, b, n, s0f, s0b in (
            (0, PROMPT_TOK, BATCH, SEQ, None, None),
            (PROMPT_TOK, TOKENS, DEC_BATCH, DEC_SEQ, s_fwd, s_bwd)):
        if s0f is None:
            s0f = jnp.zeros((b, GDN_HEADS, GDN_HEAD_DIM, GDN_HEAD_DIM), F32)
            s0b = s0f
        qkv = _gdn_conv(proj.reshape(TOKENS // n, n, n_main), conv_w, b, lo // n)
        cols, rows = _gdn_gate_tables(gf[:, lo:hi], gb[:, lo:hi], beta[:, lo:hi], b, n)
        pf, pb = _gdn_prep(qkv, cols, rows)
        o_f, o_b, sf, sb = _gdn_scan(pf, pb, s0f, s0b)
        outs.append((o_f.reshape(b * n, GDN_KEY), o_b.reshape(b * n, GDN_KEY)))
        states.append((sf, sb))
    y = _gdn_out(outs[0], outs[1], proj, norm_g)
    x = _mm_res(y, w_out.astype(BF16), x, mod, 2, name="gdn_out_proj")
    return x, states[0][0], states[0][1]


def kernel(x_prompt, x_sample, cache_na_k, cache_na_v, cache_mla_ckv, cache_mla_kpe, state_gdn_fwd,
           state_gdn_bwd, c, c_ctx, ada_w, ada_b, norm_mix_g, norm_ffn_g, na_w_in, na_q_g, na_k_g,
           na_rpb, na_w_out, mla_w_in, mla_q_a_g, mla_kv_a_g, mla_w_q_up, mla_w_kv_up, mla_q_g,
           mla_k_g, mla_w_out, gdn_w_in, gdn_conv_w, gdn_a_log, gdn_dt_bias, gdn_norm_g, gdn_w_out,
           moe_w_router, moe_b_router, moe_w_gate_up, moe_b_gate_up, moe_w_down, moe_b_down):
    x = jnp.concatenate([x_prompt.reshape(PROMPT_TOK, D_MODEL), x_sample.reshape(SAMPLE_TOK, D_MODEL)],
                        axis=0)
    c_all = jnp.concatenate([c_ctx[None], c, jnp.zeros((16 - N_GROUPS, D_MODEL), F32)], axis=0)
    mods = _ada_mods(c_all, ada_w, ada_b)[:, :N_GROUPS].reshape(DEPTH, N_GROUPS, 6, 1, D_MODEL)
    new_na_k, new_na_v, new_ckv, new_kpe, new_sf, new_sb = [], [], [], [], [], []
    for i in range(DEPTH):
        kind, j = i % N_MIXERS, i // N_MIXERS
        mod = mods[i]
        if kind == 0:
            x, nk, nv = _na_layer(x, norm_mix_g[i], mod, cache_na_k[:, j], cache_na_v[:, j],
                                  na_w_in[j], na_q_g[j], na_k_g[j], na_rpb[j], na_w_out[j])
            new_na_k.append(nk)
            new_na_v.append(nv)
        elif kind == 1:
            x, ckv, kpe = _mla_layer(x, norm_mix_g[i], mod, cache_mla_ckv[:, j], cache_mla_kpe[:, j],
                                     mla_w_in[j], mla_q_a_g[j], mla_kv_a_g[j], mla_w_q_up[j],
                                     mla_w_kv_up[j], mla_q_g[j], mla_k_g[j], mla_w_out[j])
            new_ckv.append(ckv)
            new_kpe.append(kpe)
        else:
            x, sf, sb = _gdn_layer(x, norm_mix_g[i], mod, state_gdn_fwd[:, j], state_gdn_bwd[:, j],
                                   gdn_w_in[j], gdn_conv_w[j], gdn_a_log[j], gdn_dt_bias[j],
                                   gdn_norm_g[j], gdn_w_out[j])
            new_sf.append(sf)
            new_sb.append(sb)
        x = _moe_layer(x, norm_ffn_g[i], mod, i, moe_w_router[i], moe_b_router[i], moe_w_gate_up,
                       moe_b_gate_up[i], moe_w_down, moe_b_down[i])
    return (x[:PROMPT_TOK].reshape(BATCH, SEQ, D_MODEL),
            x[PROMPT_TOK:].reshape(DEC_BATCH, DEC_SEQ, D_MODEL),
            jnp.stack(new_na_k, axis=1), jnp.stack(new_na_v, axis=1),
            jnp.stack(new_ckv, axis=1), jnp.stack(new_kpe, axis=1),
            jnp.stack(new_sf, axis=1), jnp.stack(new_sb, axis=1))
```

```python
import functools
import math

import numpy as np
import jax
import jax.numpy as jnp
from jax import lax
from jax.experimental import pallas as pl
from jax.experimental.pallas import tpu as pltpu

F32 = jnp.float32
BF16 = jnp.bfloat16
HIGHEST = lax.Precision.HIGHEST

D_MODEL = 1024
BATCH = 32
SEQ = 256
DEPTH = 4
DEC_BATCH = 8
DEC_SEQ = 2048
PAST_LEN = 512
GRID_W = 64
N_MIXERS = 3
NA_HEADS = 16
NA_HEAD_DIM = 64
NA_ROWS = 8
NA_COLS = 16
MLA_HEADS = 16
MLA_NOPE_DIM = 64
MLA_ROPE_DIM = 32
MLA_QK_DIM = MLA_NOPE_DIM + MLA_ROPE_DIM
MLA_V_DIM = 64
MLA_Q_LORA = 256
MLA_KV_LORA = 128
GDN_HEADS = 8
GDN_HEAD_DIM = 128
GDN_KEY = GDN_HEADS * GDN_HEAD_DIM
GDN_QKV = 3 * GDN_KEY
GDN_CONV_W = 5
GDN_CHUNK = 64
N_EXPERTS = 32
TOP_K = 4
D_EXPERT = D_MODEL
SWIGLU_ALPHA = 1.702
SWIGLU_LIMIT = 7.0
ROPE_BASE = 10000.0
EPS = 1e-6

PROMPT_TOK = BATCH * SEQ
SAMPLE_TOK = DEC_BATCH * DEC_SEQ
TOKENS = PROMPT_TOK + SAMPLE_TOK
N_GROUPS = 1 + DEC_BATCH

LANE = 128
TM = 512
MOE_TM = 512
MOE_ROWS = TOKENS * TOP_K
MOE_ITEMS = MOE_ROWS // MOE_TM + N_EXPERTS
TOKEN_TILE_ROWS = D_MODEL // LANE
ROW_DMA_UNROLL = 8
NORM_W = 256
VMEM_LIMIT = 56 * 1024 * 1024
NEG_BIG = -1e30


def _cparams(n_axes, vmem=None):
    return pltpu.CompilerParams(dimension_semantics=("arbitrary",) * n_axes,
                                vmem_limit_bytes=vmem)


def _dot(a, b):
    return jnp.dot(a, b, preferred_element_type=F32)


def _dot_nt(a, b):
    return lax.dot_general(a, b, (((1,), (1,)), ((), ())), preferred_element_type=F32)


def _dot_tn(a, b):
    return lax.dot_general(a, b, (((0,), (0,)), ((), ())), preferred_element_type=F32)


def _dot_hi(a, b):
    return jnp.dot(a, b, preferred_element_type=F32, precision=HIGHEST)


def _group_of_tile(i, tm):
    npt = PROMPT_TOK // tm
    spb = DEC_SEQ // tm
    return jnp.where(i < npt, 0, 1 + (i - npt) // spb)


def _block_diag_ones(n, group):
    idx = np.arange(n) // group
    return jnp.asarray((idx[:, None] == idx[None, :]).astype(np.float32), BF16)


def _ada_kernel(c_ref, w_ref, b_ref, o_ref):
    c = c_ref[...]
    s = c * jax.nn.sigmoid(c)
    o_ref[0] = _dot_hi(s, w_ref[0]) + b_ref[0]


def _ada_mods(c_all, ada_w, ada_b):
    tn = 512
    n = 6 * D_MODEL
    return pl.pallas_call(
        _ada_kernel,
        grid=(DEPTH, n // tn),
        in_specs=[pl.BlockSpec((16, D_MODEL), lambda l, j: (0, 0)),
                  pl.BlockSpec((1, D_MODEL, tn), lambda l, j: (l, 0, j)),
                  pl.BlockSpec((1, 1, tn), lambda l, j: (l, 0, j))],
        out_specs=pl.BlockSpec((1, 16, tn), lambda l, j: (l, 0, j)),
        out_shape=jax.ShapeDtypeStruct((DEPTH, 16, n), F32),
        compiler_params=_cparams(2),
        name="ada_mods",
    )(c_all, ada_w, ada_b.reshape(DEPTH, 1, n))


def _modulate(x, g, mod_ref, shift_idx, scale_idx):
    ms = jnp.mean(x * x, axis=-1, keepdims=True)
    y = x * lax.rsqrt(ms + EPS) * g
    return y * (1.0 + mod_ref[0, scale_idx]) + mod_ref[0, shift_idx]


def _group_norm(acc, bd, gain, inv_count):
    sq = (acc * acc).astype(BF16)
    ss = jnp.concatenate([_dot(sq[:, s * NORM_W:(s + 1) * NORM_W], bd)
                          for s in range(acc.shape[1] // NORM_W)], axis=1)
    return acc * lax.rsqrt(ss * inv_count + EPS) * gain


def _modmm_kernel(*refs, shift_idx, scale_idx, n_norm_tiles, inv_count):
    if n_norm_tiles:
        x_ref, g_ref, mod_ref, w_ref, gain_ref, bd_ref, o_ref, h_ref = refs
    else:
        x_ref, g_ref, mod_ref, w_ref, o_ref, h_ref = refs
    j = pl.program_id(1)

    @pl.when(j == 0)
    def _():
        h_ref[...] = _modulate(x_ref[...], g_ref[...], mod_ref, shift_idx, scale_idx).astype(BF16)

    acc = _dot(h_ref[...], w_ref[...])
    if n_norm_tiles:
        @pl.when(j < n_norm_tiles)
        def _():
            o_ref[...] = _group_norm(acc, bd_ref[...], gain_ref[...], inv_count).astype(o_ref.dtype)

        @pl.when(j >= n_norm_tiles)
        def _():
            o_ref[...] = acc.astype(o_ref.dtype)
    else:
        o_ref[...] = acc.astype(o_ref.dtype)


def _modmm(x, g, mod, w, *, shift_idx, scale_idx, tn, out_dtype=F32, norm=None, name="modmm"):
    t, d = x.shape
    n = w.shape[1]
    in_specs = [pl.BlockSpec((TM, d), lambda i, j: (i, 0)),
                pl.BlockSpec((1, d), lambda i, j: (0, 0)),
                pl.BlockSpec((1, 6, 1, d), lambda i, j: (_group_of_tile(i, TM), 0, 0, 0)),
                pl.BlockSpec((d, tn), lambda i, j: (0, j))]
    args = [x, g, mod, w]
    n_norm_tiles, inv_count = 0, 1.0
    if norm is not None:
        n_norm_tiles, gains, group = norm
        inv_count = 1.0 / group
        in_specs += [pl.BlockSpec((1, tn), lambda i, j: (0, j)),
                     pl.BlockSpec((NORM_W, NORM_W), lambda i, j: (0, 0))]
        args += [gains, _block_diag_ones(NORM_W, group)]
    return pl.pallas_call(
        functools.partial(_modmm_kernel, shift_idx=shift_idx, scale_idx=scale_idx,
                          n_norm_tiles=n_norm_tiles, inv_count=inv_count),
        grid=(t // TM, n // tn),
        in_specs=in_specs,
        out_specs=pl.BlockSpec((TM, tn), lambda i, j: (i, j)),
        out_shape=jax.ShapeDtypeStruct((t, n), out_dtype),
        scratch_shapes=[pltpu.VMEM((TM, d), BF16)],
        compiler_params=_cparams(2, VMEM_LIMIT),
        name=name,
    )(*args)


def _mm_res_kernel(a_ref, w_ref, res_ref, gate_ref, o_ref):
    o_ref[...] = res_ref[...] + gate_ref[0, 0] * _dot(a_ref[...], w_ref[...])


def _mm_res(a, w, res, mod, gate_idx, *, tn=1024, name="mm_res"):
    t, k = a.shape
    n = w.shape[1]
    return pl.pallas_call(
        _mm_res_kernel,
        grid=(t // TM, n // tn),
        in_specs=[pl.BlockSpec((TM, k), lambda i, j: (i, 0)),
                  pl.BlockSpec((k, tn), lambda i, j: (0, j)),
                  pl.BlockSpec((TM, tn), lambda i, j: (i, j)),
                  pl.BlockSpec((1, 1, 1, tn), lambda i, j: (_group_of_tile(i, TM), gate_idx, 0, j))],
        out_specs=pl.BlockSpec((TM, tn), lambda i, j: (i, j)),
        out_shape=jax.ShapeDtypeStruct((t, n), F32),
        compiler_params=_cparams(2, VMEM_LIMIT),
        name=name,
    )(a, w, res, mod)


def _rope_slots(y, c, s1, s2):
    outs = []
    for s in range(y.shape[1] // LANE):
        ys = y[:, s * LANE:(s + 1) * LANE]
        outs.append(ys * c + pltpu.roll(ys, LANE - MLA_ROPE_DIM // 2, 1) * s1
                    + pltpu.roll(ys, MLA_ROPE_DIM // 2, 1) * s2)
    return jnp.concatenate(outs, axis=1)


def _mmn_kernel(*refs, n_norm_tiles, inv_count, rope):
    if rope:
        a_ref, w_ref, gain_ref, bd_ref, c_ref, s1_ref, s2_ref, o_ref = refs
    else:
        a_ref, w_ref, gain_ref, bd_ref, o_ref = refs
    j = pl.program_id(1)
    acc = _dot(a_ref[...], w_ref[...])

    @pl.when(j < n_norm_tiles)
    def _():
        y = _group_norm(acc, bd_ref[...], gain_ref[...], inv_count)
        if rope:
            y = _rope_slots(y, c_ref[0], s1_ref[0], s2_ref[0])
        o_ref[...] = y.astype(o_ref.dtype)

    @pl.when(j >= n_norm_tiles)
    def _():
        o_ref[...] = acc.astype(o_ref.dtype)


MMN_TN = 1024


def _mmn(a, w, gains, *, n_norm_tiles, count, rope=None, tn=MMN_TN, name="mmn"):
    t, k = a.shape
    n = w.shape[1]
    in_specs = [pl.BlockSpec((TM, k), lambda i, j: (i, 0)),
                pl.BlockSpec((k, tn), lambda i, j: (0, j)),
                pl.BlockSpec((1, tn), lambda i, j: (0, j)),
                pl.BlockSpec((NORM_W, NORM_W), lambda i, j: (0, 0))]
    args = [a, w, gains, _block_diag_ones(NORM_W, LANE)]
    if rope is not None:
        tables, pos_fn = rope
        for q in range(3):
            in_specs.append(pl.BlockSpec((1, TM, LANE), lambda i, j, q=q: (q, pos_fn(i), 0)))
            args.append(tables)
    return pl.pallas_call(
        functools.partial(_mmn_kernel, n_norm_tiles=n_norm_tiles, inv_count=1.0 / count,
                          rope=rope is not None),
        grid=(t // TM, n // tn),
        in_specs=in_specs,
        out_specs=pl.BlockSpec((TM, tn), lambda i, j: (i, j)),
        out_shape=jax.ShapeDtypeStruct((t, n), BF16),
        compiler_params=_cparams(2, VMEM_LIMIT),
        name=name,
    )(*args)


def _lane_half_mask(h, width=LANE):
    lane = lax.broadcasted_iota(jnp.int32, (1, width), 1)
    return (lane // (width // 2)) == h


def _pair_attn_kernel(q_ref, k_ref, v_ref, o_ref, *, slot, pairs_blk):
    q = q_ref[0]
    k = k_ref[0]
    v = v_ref[0].astype(BF16)
    masks = [_lane_half_mask(h) for h in range(2)]
    heads = [(pp, h) for pp in range(pairs_blk) for h in range(2)]
    if slot == LANE:
        ss = [_dot_nt(q[:, (2 * pp + h) * LANE:(2 * pp + h + 1) * LANE].astype(BF16),
                      k[:, (2 * pp + h) * LANE:(2 * pp + h + 1) * LANE].astype(BF16)) for pp, h in heads]
    else:
        kb = k.astype(BF16)
        ss = [_dot_nt(jnp.where(masks[h], q[:, pp * LANE:(pp + 1) * LANE], 0).astype(BF16),
                      kb[:, pp * LANE:(pp + 1) * LANE]) for pp, h in heads]
    ps = [jnp.exp(s - jnp.max(s, axis=-1, keepdims=True)) for s in ss]
    os_ = [_dot(p.astype(BF16), jnp.where(masks[h], v[:, pp * LANE:(pp + 1) * LANE], 0).astype(BF16))
           for p, (pp, h) in zip(ps, heads)]
    scaled = [o * (1.0 / jnp.sum(p, axis=-1, keepdims=True)) for o, p in zip(os_, ps)]
    for pp in range(pairs_blk):
        o_ref[0, :, pp * LANE:(pp + 1) * LANE] = (scaled[2 * pp] + scaled[2 * pp + 1]).astype(o_ref.dtype)


def _pair_attn(q_arr, k_arr, v_arr, *, b, q_batch0=0, kv_batch0=0, q_col, k_col, v_col, slot, tq,
               n_pairs, pairs_blk, name):
    nq = q_arr.shape[1]
    nk = k_arr.shape[1]
    qw = 2 * slot * pairs_blk
    vw = LANE * pairs_blk
    qc, kc, vc = q_col // pairs_blk, k_col // pairs_blk, v_col // pairs_blk
    return pl.pallas_call(
        functools.partial(_pair_attn_kernel, slot=slot, pairs_blk=pairs_blk),
        grid=(b, n_pairs // pairs_blk, nq // tq),
        in_specs=[pl.BlockSpec((1, tq, qw), lambda bi, hp, qi: (q_batch0 + bi, qi, qc + hp)),
                  pl.BlockSpec((1, nk, qw), lambda bi, hp, qi: (kv_batch0 + bi, 0, kc + hp)),
                  pl.BlockSpec((1, nk, vw), lambda bi, hp, qi: (kv_batch0 + bi, 0, vc + hp))],
        out_specs=pl.BlockSpec((1, tq, vw), lambda bi, hp, qi: (bi, qi, hp)),
        out_shape=jax.ShapeDtypeStruct((b, nq, n_pairs * LANE), BF16),
        compiler_params=_cparams(3, VMEM_LIMIT),
        name=name,
    )(q_arr, k_arr, v_arr)


NA_QBLOCK_ROWS = 8
NA_WIN = NA_ROWS * GRID_W


def _na_lat_kernel(q_ref, k_ref, v_ref, kc_ref, vc_ref, bias_ref, o_ref, kbf, vbf, *, rows):
    qi = pl.program_id(2)

    @pl.when(qi == 0)
    def _():
        kbf[...] = k_ref[0].astype(BF16)
        v = v_ref[0]
        for h in range(2):
            vbf[h] = jnp.where(_lane_half_mask(h), v, 0.0).astype(BF16)

    q = q_ref[0]
    kc = kc_ref[0].astype(BF16)
    vc = vc_ref[0]
    masks = [_lane_half_mask(h) for h in range(2)]
    qhs = [jnp.where(mask, q, 0.0).astype(BF16) for mask in masks]
    s_cs = [_dot_nt(qh, kc) for qh in qhs]
    units = []
    for rr in range(NA_QBLOCK_ROWS):
        r = qi * NA_QBLOCK_ROWS + rr
        r0 = jnp.clip(r - NA_ROWS // 2, 0, rows - NA_ROWS)
        start = pl.multiple_of(r0 * GRID_W, GRID_W)
        for h in range(2):
            units.append((h, slice(rr * GRID_W, (rr + 1) * GRID_W), start, r - r0))
    s_ls = [_dot_nt(qhs[h][sl], kbf[pl.ds(start, NA_WIN), :]) + bias_ref[pat, h]
            for h, sl, start, pat in units]
    m_cs = [jnp.max(s_c, axis=-1, keepdims=True) for s_c in s_cs]
    p_cs = [jnp.exp(s_c - m_c) for s_c, m_c in zip(s_cs, m_cs)]
    l_cs = [jnp.sum(p_c, axis=-1, keepdims=True) for p_c in p_cs]
    o_cs = [_dot(p_c.astype(BF16), jnp.where(mask, vc, 0.0).astype(BF16)) for p_c, mask in zip(p_cs, masks)]
    ms = [jnp.maximum(jnp.max(s_l, axis=-1, keepdims=True), m_cs[h][sl])
          for s_l, (h, sl, _, _) in zip(s_ls, units)]
    p_ls = [jnp.exp(s_l - m) for s_l, m in zip(s_ls, ms)]
    o_ls = [_dot(p_l.astype(BF16), vbf[h, pl.ds(start, NA_WIN), :])
            for p_l, (h, _, start, _) in zip(p_ls, units)]
    out = [jnp.zeros((GRID_W, LANE), F32) for _ in range(NA_QBLOCK_ROWS)]
    for (h, sl, _, _), m, p_l, o_l in zip(units, ms, p_ls, o_ls):
        a_c = jnp.exp(m_cs[h][sl] - m)
        l = jnp.sum(p_l, axis=-1, keepdims=True) + a_c * l_cs[h][sl]
        rr = sl.start // GRID_W
        out[rr] = out[rr] + (o_l + a_c * o_cs[h][sl]) * (1.0 / l)
    for rr in range(NA_QBLOCK_ROWS):
        o_ref[0, rr * GRID_W:(rr + 1) * GRID_W, :] = out[rr].astype(o_ref.dtype)


def _na_bias_table(rpb):
    col = np.arange(GRID_W)
    col_start = np.clip(col - NA_COLS // 2, 0, GRID_W - NA_COLS)
    kc = np.arange(GRID_W)
    inside = (kc[None, :] >= col_start[:, None]) & (kc[None, :] < col_start[:, None] + NA_COLS)
    pad = GRID_W - NA_COLS
    rpb_pad = jnp.pad(rpb, ((0, 0), (0, 0), (pad, pad + 1)))
    by_col = jnp.stack([rpb_pad[:, :, GRID_W - 1 - c:2 * GRID_W - 1 - c] for c in range(GRID_W)], axis=2)
    by_col = jnp.where(jnp.asarray(inside)[None, None], by_col, NEG_BIG)
    tab = jnp.stack([by_col[:, NA_ROWS - 1 - p:2 * NA_ROWS - 1 - p] for p in range(NA_ROWS)], axis=0)
    tab = tab.transpose(0, 1, 3, 2, 4)
    return tab.reshape(NA_ROWS, NA_HEADS, GRID_W, NA_WIN).astype(F32)


def _na_latent_attn(qkv, kctx, vctx, bias, batch0):
    n = qkv.shape[1]
    b = kctx.shape[0]
    rows = n // GRID_W
    n_pairs = NA_HEADS // 2
    tq = NA_QBLOCK_ROWS * GRID_W
    past = kctx.shape[1]
    return pl.pallas_call(
        functools.partial(_na_lat_kernel, rows=rows),
        grid=(b, n_pairs, n // tq),
        in_specs=[pl.BlockSpec((1, tq, LANE), lambda bi, hp, qi: (batch0 + bi, qi, hp)),
                  pl.BlockSpec((1, n, LANE), lambda bi, hp, qi: (batch0 + bi, 0, n_pairs + hp)),
                  pl.BlockSpec((1, n, LANE), lambda bi, hp, qi: (batch0 + bi, 0, 2 * n_pairs + hp)),
                  pl.BlockSpec((1, past, LANE), lambda bi, hp, qi: (bi, 0, hp)),
                  pl.BlockSpec((1, past, LANE), lambda bi, hp, qi: (bi, 0, hp)),
                  pl.BlockSpec((NA_ROWS, 2, GRID_W, NA_WIN), lambda bi, hp, qi: (0, hp, 0, 0))],
        out_specs=pl.BlockSpec((1, tq, LANE), lambda bi, hp, qi: (bi, qi, hp)),
        out_shape=jax.ShapeDtypeStruct((b, n, n_pairs * LANE), BF16),
        scratch_shapes=[pltpu.VMEM((n, LANE), BF16), pltpu.VMEM((2, n, LANE), BF16)],
        compiler_params=_cparams(3, VMEM_LIMIT),
        name="na_latent_attn",
    )(qkv, qkv, qkv, kctx, vctx, bias)


def _mla_lat_kernel(p_ref, qg_ref, kvg_ref, cq_ref, ckv_ref, kin_ref):
    p = p_ref[...]
    cq = p[:, :MLA_Q_LORA]
    cq = cq * lax.rsqrt(jnp.mean(cq * cq, axis=-1, keepdims=True) + EPS) * qg_ref[...]
    cq_ref[...] = cq.astype(BF16)
    ckv = p[:, MLA_Q_LORA:MLA_Q_LORA + MLA_KV_LORA]
    ckv = ckv * lax.rsqrt(jnp.mean(ckv * ckv, axis=-1, keepdims=True) + EPS) * kvg_ref[...]
    ckv_ref[...] = ckv
    tail = p[:, MLA_Q_LORA + MLA_KV_LORA:]
    kin_ref[...] = jnp.concatenate([ckv, tail], axis=1).astype(BF16)


def _mla_latents(proj, q_a_g, kv_a_g):
    t = proj.shape[0]
    w = proj.shape[1]
    return pl.pallas_call(
        _mla_lat_kernel,
        grid=(t // TM,),
        in_specs=[pl.BlockSpec((TM, w), lambda i: (i, 0)),
                  pl.BlockSpec((1, MLA_Q_LORA), lambda i: (0, 0)),
                  pl.BlockSpec((1, MLA_KV_LORA), lambda i: (0, 0))],
        out_specs=[pl.BlockSpec((TM, MLA_Q_LORA), lambda i: (i, 0)),
                   pl.BlockSpec((TM, MLA_KV_LORA), lambda i: (i, 0)),
                   pl.BlockSpec((TM, 2 * LANE), lambda i: (i, 0))],
        out_shape=[jax.ShapeDtypeStruct((t, MLA_Q_LORA), BF16),
                   jax.ShapeDtypeStruct((t, MLA_KV_LORA), F32),
                   jax.ShapeDtypeStruct((t, 2 * LANE), BF16)],
        compiler_params=_cparams(1),
        name="mla_latents",
    )(proj, q_a_g.reshape(1, -1), kv_a_g.reshape(1, -1))


def _rope_tables(n_pos, n_identity_rows_first=0, n_identity_rows_last=0):
    n_freq = MLA_ROPE_DIM // 4
    inv = ROPE_BASE ** (-jnp.arange(n_freq, dtype=F32) / n_freq)
    t = jnp.arange(n_pos)
    row = (t // GRID_W).astype(F32)
    col = (t % GRID_W).astype(F32)
    ang = jnp.concatenate([row[:, None] * inv, col[:, None] * inv], axis=-1)
    cos, sin = jnp.cos(ang), jnp.sin(ang)
    half = MLA_ROPE_DIM // 2
    ones = jnp.ones((n_pos, MLA_NOPE_DIM), F32)
    zeros = jnp.zeros((n_pos, MLA_NOPE_DIM), F32)
    pad1 = jnp.ones((n_pos, LANE - MLA_QK_DIM), F32)
    pad0 = jnp.zeros((n_pos, LANE - MLA_QK_DIM), F32)
    zh = jnp.zeros((n_pos, half), F32)
    c = jnp.concatenate([ones, cos, cos, pad1], axis=1)
    s1 = jnp.concatenate([zeros, -sin, zh, pad0], axis=1)
    s2 = jnp.concatenate([zeros, zh, sin, pad0], axis=1)
    tab = jnp.stack([c, s1, s2])

    def ident(nrows):
        return jnp.stack([jnp.ones((nrows, LANE), F32), jnp.zeros((nrows, LANE), F32),
                          jnp.zeros((nrows, LANE), F32)])

    parts = []
    if n_identity_rows_first:
        parts.append(ident(n_identity_rows_first))
    parts.append(tab)
    if n_identity_rows_last:
        parts.append(ident(n_identity_rows_last))
    return jnp.concatenate(parts, axis=1)


def _gdn_conv_kernel(x_ref, w_ref, o_ref, pad_ref, *, n):
    j = pl.program_id(1)
    halo = 8
    cw = x_ref.shape[2]
    pad_ref[0:halo, :] = jnp.zeros((halo, cw), F32)
    pad_ref[halo + n:2 * halo + n, :] = jnp.zeros((halo, cw), F32)
    pad_ref[halo:halo + n, :] = x_ref[0]
    acc = jnp.zeros((n, cw), F32)
    for t in range(GDN_CONV_W):
        off = halo - GDN_CONV_W // 2 + t
        acc = acc + pad_ref[off:off + n, :] * w_ref[t:t + 1, :]
    y = acc * jax.nn.sigmoid(acc)
    n_qk_blocks = 2 * GDN_KEY // cw
    n_q_blocks = GDN_KEY // cw
    outs = []
    for s in range(cw // LANE):
        ys = y[:, s * LANE:(s + 1) * LANE]
        ss = jnp.sum(ys * ys, axis=-1, keepdims=True)
        r = jnp.where(j < n_qk_blocks, lax.rsqrt(ss + EPS), 1.0)
        r = r * jnp.where(j < n_q_blocks, GDN_HEAD_DIM ** -0.5, 1.0)
        outs.append(ys * r)
    o_ref[0] = jnp.concatenate(outs, axis=1)


def _gdn_conv(proj3, conv_w, b, batch0):
    n = proj3.shape[1]
    cw = 1024 if n <= 512 else 512
    return pl.pallas_call(
        functools.partial(_gdn_conv_kernel, n=n),
        grid=(b, GDN_QKV // cw),
        in_specs=[pl.BlockSpec((1, n, cw), lambda bi, j: (batch0 + bi, 0, j)),
                  pl.BlockSpec((GDN_CONV_W, cw), lambda bi, j: (0, j))],
        out_specs=pl.BlockSpec((1, n, cw), lambda bi, j: (bi, 0, j)),
        out_shape=jax.ShapeDtypeStruct((b, n, GDN_QKV), F32),
        scratch_shapes=[pltpu.VMEM((n + 16, cw), F32)],
        compiler_params=_cparams(2, VMEM_LIMIT),
        name="gdn_conv",
    )(proj3, conv_w)


def _gdn_gates_kernel(ab_ref, alog_ref, dtb_ref, uf_ref, ub_ref, gf_ref, gb_ref, beta_ref):
    ab = ab_ref[...]
    a = jnp.concatenate([ab[0:8], ab[16:24]], axis=0)
    bb = jnp.concatenate([ab[8:16], ab[24:32]], axis=0)
    z = a + dtb_ref[...]
    sp = jnp.maximum(z, 0.0) + jnp.log(1.0 + jnp.exp(-jnp.abs(z)))
    g = -jnp.exp(alog_ref[...]) * sp
    beta_ref[...] = jax.nn.sigmoid(bb)
    gf_ref[...] = _dot_hi(g, uf_ref[...])
    gb_ref[...] = _dot_hi(g, ub_ref[...])


def _gdn_gates(ab_t, a_log, dt_bias):
    t = ab_t.shape[1]
    tl = 256
    idx = np.arange(tl)
    same = (idx[:, None] // GDN_CHUNK) == (idx[None, :] // GDN_CHUNK)
    uf = jnp.asarray((same & (idx[:, None] <= idx[None, :])).astype(np.float32))
    ub = jnp.asarray((same & (idx[:, None] >= idx[None, :])).astype(np.float32))
    spec = pl.BlockSpec((16, tl), lambda i: (0, i))
    return pl.pallas_call(
        _gdn_gates_kernel,
        grid=(t // tl,),
        in_specs=[pl.BlockSpec((32, tl), lambda i: (0, i)),
                  pl.BlockSpec((16, 1), lambda i: (0, 0)),
                  pl.BlockSpec((16, 1), lambda i: (0, 0)),
                  pl.BlockSpec((tl, tl), lambda i: (0, 0)),
                  pl.BlockSpec((tl, tl), lambda i: (0, 0))],
        out_specs=[spec, spec, spec],
        out_shape=[jax.ShapeDtypeStruct((16, t), F32)] * 3,
        compiler_params=_cparams(1),
        name="gdn_gates",
    )(ab_t, a_log.reshape(16, 1), dt_bias.reshape(16, 1), uf, ub)


def _tri_inverse_many(a_list, block):
    n = a_list[0].shape[0]
    ri = lax.broadcasted_iota(jnp.int32, (n, n), 0)
    ci = lax.broadcasted_iota(jnp.int32, (n, n), 1)
    eye = (ri == ci).astype(F32)
    xs = [(-a).astype(BF16) for a in a_list]
    ts = [eye - a for a in a_list]
    pws = [_dot(x, x) for x in xs]
    n_steps = int(math.log2(block)) - 1
    for step in range(n_steps):
        pwbs = [pw.astype(BF16) for pw in pws]
        if step < n_steps - 1:
            sts = [_dot(jnp.concatenate([pwb, t.astype(BF16)], axis=0), pwb) for pwb, t in zip(pwbs, ts)]
            pws = [st[:n] for st in sts]
            ts = [t + st[n:] for t, st in zip(ts, sts)]
        else:
            ts = [t + _dot(t.astype(BF16), pwb) for t, pwb in zip(ts, pwbs)]
    return ts


GDN_SLOT = 5 * GDN_HEAD_DIM


GDN_GROUP = 256


def _gdn_prep_kernel(q_ref, k_ref, v_ref, cols_ref, rows_ref, pf_ref, pb_ref, *, heads_blk):
    c = GDN_CHUNK
    n = GDN_GROUP
    ri = lax.broadcasted_iota(jnp.int32, (n, n), 0)
    cj = lax.broadcasted_iota(jnp.int32, (n, n), 1)
    same = (ri // c) == (cj // c)
    lane = lax.broadcasted_iota(jnp.int32, (n, LANE), 1)
    heads = []
    for hh in range(heads_blk):
        lanes = slice(hh * GDN_HEAD_DIM, (hh + 1) * GDN_HEAD_DIM)
        q = q_ref[0, :, lanes]
        k = k_ref[0, :, lanes]
        kb16 = k.astype(BF16)
        heads.append((q, k, v_ref[0, :, lanes], _dot_nt(kb16, kb16), _dot_nt(q.astype(BF16), kb16)))
    chains = []
    for hh in range(heads_blk):
        cols = cols_ref[0, hh, 0]
        rows = rows_ref[0, hh, 0]
        kk = heads[hh][3]
        for direction in range(2):
            gcol = cols[:, 2 * direction:2 * direction + 1]
            bcol = cols[:, 2 * direction + 1:2 * direction + 2]
            glast = cols[:, 4 + direction:5 + direction]
            grow = rows[direction:direction + 1, :]
            if direction == 0:
                incl, strict = same & (ri >= cj), same & (ri > cj)
            else:
                incl, strict = same & (ri <= cj), same & (ri < cj)
            decay = jnp.where(incl, jnp.exp(jnp.where(incl, gcol - grow, 0.0)), 0.0)
            chains.append((hh, direction, gcol, bcol, glast, decay,
                           jnp.where(strict, kk * bcol * decay, 0.0)))
    tmats = _tri_inverse_many([ch[6] for ch in chains], c)
    uws = []
    for (hh, direction, gcol, bcol, glast, decay, _), tmat in zip(chains, tmats):
        q, k, v, _, _ = heads[hh]
        rhs = jnp.concatenate([v * bcol, k * (bcol * jnp.exp(gcol))], axis=1).astype(BF16)
        uws.append(_dot(tmat.astype(BF16), rhs))
    for (hh, direction, gcol, bcol, glast, decay, _), uw in zip(chains, uws):
        q, k, v, _, qk = heads[hh]
        attn = qk * decay
        folded = attn[:, :LANE] + attn[:, LANE:]
        folded = folded + pltpu.roll(folded, c, 1)
        egl = jnp.exp(glast)
        egl_hi = egl.astype(BF16).astype(F32)
        tail = jnp.where(lane < c + c // 2, egl_hi, egl - egl_hi)
        a128 = jnp.where(lane < c, folded, tail)
        slot = jnp.concatenate([uw, q * jnp.exp(gcol), k * jnp.exp(glast - gcol), a128], axis=1)
        out_ref = pf_ref if direction == 0 else pb_ref
        out_ref[0, :, hh * GDN_SLOT:(hh + 1) * GDN_SLOT] = slot.astype(BF16)


def _gdn_prep(qkv, cols, rows):
    b, n, _ = qkv.shape
    hb = 4
    nhb = GDN_HEADS // hb
    tb = GDN_GROUP
    w = hb * GDN_HEAD_DIM
    p_spec = pl.BlockSpec((1, tb, hb * GDN_SLOT), lambda bi, h, t: (bi, t, h))
    return pl.pallas_call(
        functools.partial(_gdn_prep_kernel, heads_blk=hb),
        grid=(b, nhb, n // tb),
        in_specs=[pl.BlockSpec((1, tb, w), lambda bi, h, t: (bi, t, h)),
                  pl.BlockSpec((1, tb, w), lambda bi, h, t: (bi, t, nhb + h)),
                  pl.BlockSpec((1, tb, w), lambda bi, h, t: (bi, t, 2 * nhb + h)),
                  pl.BlockSpec((1, hb, 1, tb, 8), lambda bi, h, t: (bi, h, t, 0, 0)),
                  pl.BlockSpec((1, hb, 1, 8, tb), lambda bi, h, t: (bi, h, t, 0, 0))],
        out_specs=[p_spec, p_spec],
        out_shape=[jax.ShapeDtypeStruct((b, n, GDN_HEADS * GDN_SLOT), BF16)] * 2,
        compiler_params=_cparams(3, VMEM_LIMIT),
        name="gdn_prep",
    )(qkv, qkv, qkv, cols, rows)


def _gdn_scan_kernel(pf_ref, pb_ref, s0f_ref, s0b_ref, of_ref, ob_ref, sf_ref, sb_ref, s_scr,
                     *, heads_blk, chunks_blk):
    c = GDN_CHUNK
    hd = GDN_HEAD_DIM
    t = pl.program_id(2)

    @pl.when(t == 0)
    def _():
        for hh in range(heads_blk):
            s_scr[2 * hh] = s0f_ref[0, hh]
            s_scr[2 * hh + 1] = s0b_ref[0, hh]

    def body(i, carry):
        tf = pl.multiple_of(i * c, c)
        tbk = pl.multiple_of((chunks_blk - 1 - i) * c, c)
        chains = [(pf_ref, of_ref, tf, hh, 2 * hh) for hh in range(heads_blk)]
        chains += [(pb_ref, ob_ref, tbk, hh, 2 * hh + 1) for hh in range(heads_blk)]
        states = [s_scr[ci] for _, _, _, _, ci in chains]
        wss = []
        for (p_ref, _, t0, hh, _), state in zip(chains, states):
            base = hh * GDN_SLOT
            wq = jnp.concatenate([p_ref[0, pl.ds(t0, c), base + hd:base + 2 * hd],
                                  p_ref[0, pl.ds(t0, c), base + 2 * hd:base + 3 * hd]], axis=0)
            wss.append(_dot(wq, state.astype(BF16)))
        v_news = []
        for (p_ref, _, t0, hh, _), ws in zip(chains, wss):
            base = hh * GDN_SLOT
            v_news.append((p_ref[0, pl.ds(t0, c), base:base + hd].astype(F32) - ws[:c]).astype(BF16))
        a128s = [p_ref[0, pl.ds(t0, c), hh * GDN_SLOT + 4 * hd:hh * GDN_SLOT + 5 * hd]
                 for p_ref, _, t0, hh, _ in chains]
        outs = [ws[c:] + _dot(a128[:, :c], v_new) for ws, a128, v_new in zip(wss, a128s, v_news)]
        upds = [_dot_tn(p_ref[0, pl.ds(t0, c), hh * GDN_SLOT + 3 * hd:hh * GDN_SLOT + 4 * hd], v_new)
                for (p_ref, _, t0, hh, _), v_new in zip(chains, v_news)]
        for (_, o_ref, t0, hh, ci), state, a128, o, upd in zip(chains, states, a128s, outs, upds):
            egl = a128[0:1, c:c + 1].astype(F32) + a128[0:1, c + c // 2:c + c // 2 + 1].astype(F32)
            s_scr[ci] = state * egl + upd
            o_ref[0, pl.ds(t0, c), hh * hd:(hh + 1) * hd] = o
        return carry

    lax.fori_loop(0, chunks_blk, body, 0)

    @pl.when(t == pl.num_programs(2) - 1)
    def _():
        for hh in range(heads_blk):
            sf_ref[0, hh] = s_scr[2 * hh]
            sb_ref[0, hh] = s_scr[2 * hh + 1]


def _gdn_scan(pf, pb, s0f, s0b):
    b, n, _ = pf.shape
    hs = 8
    nhs = GDN_HEADS // hs
    tb = min(n, 512)
    nt = n // tb
    cb = tb // GDN_CHUNK
    st_spec = pl.BlockSpec((1, hs, GDN_HEAD_DIM, GDN_HEAD_DIM), lambda bi, h, t: (bi, h, 0, 0))
    return pl.pallas_call(
        functools.partial(_gdn_scan_kernel, heads_blk=hs, chunks_blk=cb),
        grid=(b, nhs, nt),
        in_specs=[pl.BlockSpec((1, tb, hs * GDN_SLOT), lambda bi, h, t: (bi, t, h)),
                  pl.BlockSpec((1, tb, hs * GDN_SLOT), lambda bi, h, t: (bi, nt - 1 - t, h)),
                  st_spec, st_spec],
        out_specs=[pl.BlockSpec((1, tb, hs * GDN_HEAD_DIM), lambda bi, h, t: (bi, t, h)),
                   pl.BlockSpec((1, tb, hs * GDN_HEAD_DIM), lambda bi, h, t: (bi, nt - 1 - t, h)),
                   st_spec, st_spec],
        out_shape=[jax.ShapeDtypeStruct((b, n, GDN_KEY), F32)] * 2
        + [jax.ShapeDtypeStruct((b, GDN_HEADS, GDN_HEAD_DIM, GDN_HEAD_DIM), F32)] * 2,
        scratch_shapes=[pltpu.VMEM((2 * hs, GDN_HEAD_DIM, GDN_HEAD_DIM), F32)],
        compiler_params=_cparams(3, VMEM_LIMIT),
        name="gdn_scan",
    )(pf, pb, s0f, s0b)


def _gdn_gate_tables(gf, gb, beta, b, n):
    nc = n // GDN_CHUNK
    ng = n // GDN_GROUP

    def split(x):
        return x.reshape(2, GDN_HEADS, b, nc, GDN_CHUNK)

    def group(x):
        return x.reshape(GDN_HEADS, b, ng, GDN_GROUP)

    gf5, gb5, be5 = split(gf), split(gb), split(beta)
    gcf, gcb = group(gf5[0]), group(gb5[1])
    glf = group(jnp.broadcast_to(gf5[0][..., GDN_CHUNK - 1:], gf5[0].shape))
    glb = group(jnp.broadcast_to(gb5[1][..., :1], gb5[1].shape))
    zero = jnp.zeros_like(gcf)
    col_list = [gcf, group(be5[0]), gcb, group(be5[1]), glf, glb, zero, zero]
    cols_t = jnp.stack(col_list, axis=-2).transpose(1, 0, 2, 3, 4)
    cols = jnp.swapaxes(cols_t, -1, -2)
    row_list = [gcf, gcb, zero, zero, zero, zero, zero, zero]
    rows = jnp.stack(row_list, axis=-2).transpose(1, 0, 2, 3, 4)
    return cols, rows


def _gdn_out_kernel(ofp_ref, obp_ref, ofs_ref, obs_ref, z_ref, g_ref, y_ref, o_scr, *, n_prompt_tiles):
    i = pl.program_id(0)

    @pl.when(i < n_prompt_tiles)
    def _():
        o_scr[...] = ofp_ref[...] + obp_ref[...]

    @pl.when(i >= n_prompt_tiles)
    def _():
        o_scr[...] = ofs_ref[...] + obs_ref[...]

    o = o_scr[...]
    z = z_ref[...]
    gate = z * jax.nn.sigmoid(z)
    outs = []
    for h in range(GDN_HEADS):
        sl = slice(h * GDN_HEAD_DIM, (h + 1) * GDN_HEAD_DIM)
        oh = o[:, sl]
        r = lax.rsqrt(jnp.mean(oh * oh, axis=-1, keepdims=True) + EPS)
        outs.append(oh * r * g_ref[...] * gate[:, sl])
    y_ref[...] = jnp.concatenate(outs, axis=1).astype(BF16)


def _gdn_out(o_prompt, o_sample, proj, norm_g):
    t = proj.shape[0]
    npt = o_prompt[0].shape[0] // TM
    z_block = GDN_QKV // GDN_KEY
    p_spec = pl.BlockSpec((TM, GDN_KEY), lambda i: (jnp.minimum(i, npt - 1), 0))
    s_spec = pl.BlockSpec((TM, GDN_KEY), lambda i: (jnp.maximum(i - npt, 0), 0))
    return pl.pallas_call(
        functools.partial(_gdn_out_kernel, n_prompt_tiles=npt),
        grid=(t // TM,),
        in_specs=[p_spec, p_spec, s_spec, s_spec,
                  pl.BlockSpec((TM, GDN_KEY), lambda i: (i, z_block)),
                  pl.BlockSpec((1, GDN_HEAD_DIM), lambda i: (0, 0))],
        out_specs=pl.BlockSpec((TM, GDN_KEY), lambda i: (i, 0)),
        out_shape=jax.ShapeDtypeStruct((t, GDN_KEY), BF16),
        scratch_shapes=[pltpu.VMEM((TM, GDN_KEY), F32)],
        compiler_params=_cparams(1, VMEM_LIMIT),
        name="gdn_out",
    )(*o_prompt, *o_sample, proj, norm_g.reshape(1, -1))


def _store_token_tiles(ref, x):
    for s in range(TOKEN_TILE_ROWS):
        ref[pl.ds(s, x.shape[0], stride=TOKEN_TILE_ROWS), :] = x[:, s * LANE:(s + 1) * LANE]


def _load_token_tiles(ref, rows):
    return [ref[pl.ds(s, rows, stride=TOKEN_TILE_ROWS), :] for s in range(TOKEN_TILE_ROWS)]


def _router_kernel(x_ref, g_ref, mod_ref, wr_ref, br_ref, h_ref, idx_ref, w_ref, cnt_ref):
    h = _modulate(x_ref[...], g_ref[...], mod_ref, 3, 4)
    _store_token_tiles(h_ref, h)
    logits = lax.dot_general(wr_ref[...], h, (((1,), (1,)), ((), ())),
                             preferred_element_type=F32, precision=HIGHEST) + br_ref[...]
    ne, tm = logits.shape
    iota = lax.broadcasted_iota(jnp.int32, (ne, tm), 0)
    vals, idxs = [], []
    cur = logits
    for _ in range(TOP_K):
        m = jnp.max(cur, axis=0, keepdims=True)
        idx = jnp.min(jnp.where(cur == m, iota, ne), axis=0, keepdims=True)
        vals.append(m)
        idxs.append(idx)
        cur = jnp.where(iota == idx, -jnp.inf, cur)
    es = [jnp.exp(v - vals[0]) for v in vals]
    tot = es[0] + es[1] + es[2] + es[3]
    cnt = jnp.zeros((ne, 1), jnp.int32)
    for kk in range(TOP_K):
        idx_ref[kk:kk + 1, :] = idxs[kk]
        w_ref[kk:kk + 1, :] = es[kk] / tot
        cnt = cnt + jnp.sum((iota == idxs[kk]).astype(jnp.int32), axis=1, keepdims=True)

    @pl.when(pl.program_id(0) == 0)
    def _():
        cnt_ref[...] = jnp.zeros(cnt_ref.shape, jnp.int32)

    cnt_ref[...] += cnt


def _router(x, g, mod, w_router_t, b_router):
    t, d = x.shape
    return pl.pallas_call(
        _router_kernel,
        grid=(t // TM,),
        in_specs=[pl.BlockSpec((TM, d), lambda i: (i, 0)),
                  pl.BlockSpec((1, d), lambda i: (0, 0)),
                  pl.BlockSpec((1, 6, 1, d), lambda i: (_group_of_tile(i, TM), 0, 0, 0)),
                  pl.BlockSpec((N_EXPERTS, d), lambda i: (0, 0)),
                  pl.BlockSpec((N_EXPERTS, 1), lambda i: (0, 0))],
        out_specs=[pl.BlockSpec((TM * TOKEN_TILE_ROWS, LANE), lambda i: (i, 0)),
                   pl.BlockSpec((TOP_K, TM), lambda i: (0, i)),
                   pl.BlockSpec((TOP_K, TM), lambda i: (0, i)),
                   pl.BlockSpec((N_EXPERTS, 1), lambda i: (0, 0))],
        out_shape=[jax.ShapeDtypeStruct((t * TOKEN_TILE_ROWS, LANE), F32),
                   jax.ShapeDtypeStruct((TOP_K, t), jnp.int32),
                   jax.ShapeDtypeStruct((TOP_K, t), F32),
                   jax.ShapeDtypeStruct((N_EXPERTS, 1), jnp.int32)],
        compiler_params=_cparams(1, VMEM_LIMIT),
        name="moe_router",
    )(x, g, mod, w_router_t, b_router.reshape(N_EXPERTS, 1))


def _tile_rows(i):
    return pl.ds(pl.multiple_of(i * TOKEN_TILE_ROWS, TOKEN_TILE_ROWS), TOKEN_TILE_ROWS)


ITEM_FIRST, ITEM_LAST, ITEM_VALID, ITEM_FINAL = 1, 2, 4, 8
MOE_ROW_TILES = MOE_ROWS // MOE_TM
MOE_TILE_SUBLANES = MOE_TM * TOKEN_TILE_ROWS


def _issue_token_dmas(make_copy):
    def issue(g, carry):
        for u in range(ROW_DMA_UNROLL):
            make_copy(g * ROW_DMA_UNROLL + u).start(priority=u % 2)
        return carry

    lax.fori_loop(0, MOE_TM // ROW_DMA_UNROLL, issue, 0)


def _experts_kernel(tile_ref, exp_ref, lo_ref, hi_ref, flag_ref, src_ref, src_next_ref, dst_ref,
                    h_hbm, wgu_ref, bgu_ref, wd_ref, bd_ref, wrow_ref, y_hbm,
                    wgu_bf, wd_bf, x_bf, acc, xg, yo, gsem, ssem):
    j = pl.program_id(0)
    t = tile_ref[j]
    e = exp_ref[j]
    prev = exp_ref[jnp.maximum(j - 1, 0)]
    flags = flag_ref[j]
    first = (flags & ITEM_FIRST) != 0
    last = (flags & ITEM_LAST) != 0
    odd = lax.rem(t, 2)

    def gather_copy(idx_ref, slot, r):
        return pltpu.make_async_copy(h_hbm.at[_tile_rows(idx_ref[0, 0, r])], xg.at[slot, _tile_rows(r)],
                                     gsem.at[slot])

    def gather_all(slot):
        return pltpu.make_async_copy(h_hbm.at[pl.ds(0, MOE_TILE_SUBLANES)], xg.at[slot], gsem.at[slot])

    def scatter_copy(slot, r):
        return pltpu.make_async_copy(yo.at[slot, _tile_rows(r)], y_hbm.at[_tile_rows(dst_ref[0, 0, r])],
                                     ssem.at[slot])

    def scatter_all(slot):
        return pltpu.make_async_copy(yo.at[slot], y_hbm.at[pl.ds(0, MOE_TILE_SUBLANES)], ssem.at[slot])

    @pl.when(j == 0)
    def _():
        _issue_token_dmas(functools.partial(gather_copy, src_ref, 0))

    @pl.when((j == 0) | (e != prev))
    def _():
        wgu_bf[...] = wgu_ref[0, 0].astype(BF16)
        wd_bf[...] = wd_ref[0, 0].astype(BF16)

    for slot in range(2):
        @pl.when(first & (odd == slot))
        def _(slot=slot):
            gather_all(slot).wait()

            @pl.when(t + 1 < MOE_ROW_TILES)
            def _():
                _issue_token_dmas(functools.partial(gather_copy, src_next_ref, 1 - slot))

            for s, xs in enumerate(_load_token_tiles(xg.at[slot], MOE_TM)):
                x_bf[:, s * LANE:(s + 1) * LANE] = xs.astype(BF16)

    @pl.when((flags & ITEM_VALID) != 0)
    def _():
        gu = _dot(x_bf[...], wgu_bf[...]) + bgu_ref[0]
        glu = jnp.minimum(gu[:, :D_EXPERT], SWIGLU_LIMIT)
        lin = jnp.clip(gu[:, D_EXPERT:], -SWIGLU_LIMIT, SWIGLU_LIMIT)
        act = (lin + 1.0) * glu * jax.nn.sigmoid(SWIGLU_ALPHA * glu)
        y = (_dot(act.astype(BF16), wd_bf[...]) + bd_ref[0]) * wrow_ref[:, 0:1]
        row = lax.broadcasted_iota(jnp.int32, (MOE_TM, 1), 0)
        mine = (row >= lo_ref[j]) & (row < hi_ref[j])

        @pl.when(first)
        def _():
            acc[...] = jnp.where(mine, y, 0.0)

        @pl.when(jnp.logical_not(first))
        def _():
            acc[...] = jnp.where(mine, y, acc[...])

    for slot in range(2):
        @pl.when(last & (odd == slot))
        def _(slot=slot):
            @pl.when(t >= 2)
            def _():
                scatter_all(slot).wait()

            _store_token_tiles(yo.at[slot], acc[...])
            _issue_token_dmas(functools.partial(scatter_copy, slot))

            @pl.when((flags & ITEM_FINAL) != 0)
            def _():
                scatter_all(1 - slot).wait()
                scatter_all(slot).wait()


def _experts(plan, h_tiles, src_tok, dst_tok, layer, w_gate_up, b_gate_up, w_down, b_down, w_sorted):
    d = D_MODEL
    a = src_tok.shape[0]

    def tile_map(j, tile, exp, lo, hi, flag):
        return (tile[j], 0)

    def idx_map(j, tile, exp, lo, hi, flag):
        return (tile[j], 0, 0)

    def idx_next_map(j, tile, exp, lo, hi, flag):
        return (jnp.minimum(tile[j] + 1, MOE_ROW_TILES - 1), 0, 0)

    def exp_map(j, tile, exp, lo, hi, flag):
        return (exp[j], 0, 0)

    def w_map(j, tile, exp, lo, hi, flag):
        return (layer, exp[j], 0, 0)

    idx_block = (1, 1, MOE_TM)
    src3 = src_tok.reshape(MOE_ROW_TILES, 1, MOE_TM)
    grid_spec = pltpu.PrefetchScalarGridSpec(
        num_scalar_prefetch=5,
        grid=(MOE_ITEMS,),
        in_specs=[pl.BlockSpec(idx_block, idx_map, memory_space=pltpu.SMEM),
                  pl.BlockSpec(idx_block, idx_next_map, memory_space=pltpu.SMEM),
                  pl.BlockSpec(idx_block, idx_map, memory_space=pltpu.SMEM),
                  pl.BlockSpec(memory_space=pl.ANY),
                  pl.BlockSpec((1, 1, d, 2 * D_EXPERT), w_map),
                  pl.BlockSpec((1, 1, 2 * D_EXPERT), exp_map),
                  pl.BlockSpec((1, 1, D_EXPERT, d), w_map),
                  pl.BlockSpec((1, 1, d), exp_map),
                  pl.BlockSpec((MOE_TM, LANE), tile_map)],
        out_specs=pl.BlockSpec(memory_space=pl.ANY),
        scratch_shapes=[pltpu.VMEM((d, 2 * D_EXPERT), BF16), pltpu.VMEM((D_EXPERT, d), BF16),
                        pltpu.VMEM((MOE_TM, d), BF16), pltpu.VMEM((MOE_TM, d), F32),
                        pltpu.VMEM((2, MOE_TILE_SUBLANES, LANE), F32),
                        pltpu.VMEM((2, MOE_TILE_SUBLANES, LANE), F32),
                        pltpu.SemaphoreType.DMA((2,)), pltpu.SemaphoreType.DMA((2,))],
    )
    return pl.pallas_call(
        _experts_kernel,
        grid_spec=grid_spec,
        out_shape=jax.ShapeDtypeStruct((a * TOKEN_TILE_ROWS, LANE), F32),
        compiler_params=_cparams(1, VMEM_LIMIT),
        name="moe_experts",
    )(*plan, src3, src3, dst_tok.reshape(MOE_ROW_TILES, 1, MOE_TM), h_tiles, w_gate_up,
      b_gate_up.reshape(N_EXPERTS, 1, -1), w_down, b_down.reshape(N_EXPERTS, 1, -1), w_sorted)


def _moe_sum_kernel(x_ref, gate_ref, y0, y1, y2, y3, o_ref, ysum):
    ysum[...] = y0[...] + y1[...] + y2[...] + y3[...]
    gate = gate_ref[0, 0]
    for s, ys in enumerate(_load_token_tiles(ysum, TM)):
        cols = slice(s * LANE, (s + 1) * LANE)
        o_ref[:, cols] = x_ref[:, cols] + gate[:, cols] * ys


def _moe_sum(x, mod, y_tok):
    t, d = x.shape
    nt = t // TM
    blk = TM * TOKEN_TILE_ROWS
    y_specs = [pl.BlockSpec((blk, LANE), lambda i, k=k: (k * nt + i, 0)) for k in range(TOP_K)]
    return pl.pallas_call(
        _moe_sum_kernel,
        grid=(nt,),
        in_specs=[pl.BlockSpec((TM, d), lambda i: (i, 0)),
                  pl.BlockSpec((1, 1, 1, d), lambda i: (_group_of_tile(i, TM), 5, 0, 0))] + y_specs,
        out_specs=pl.BlockSpec((TM, d), lambda i: (i, 0)),
        out_shape=jax.ShapeDtypeStruct((t, d), F32),
        scratch_shapes=[pltpu.VMEM((blk, LANE), F32)],
        compiler_params=_cparams(1, VMEM_LIMIT),
        name="moe_sum",
    )(x, mod, y_tok, y_tok, y_tok, y_tok)


def _pick(onehot, vec):
    return jnp.sum(jnp.where(onehot, vec[None, :], 0), axis=1)


def _moe_plan(counts):
    ends = jnp.cumsum(counts)
    starts = ends - counts
    first_tile = starts // MOE_TM
    n_items_e = jnp.where(counts > 0, (ends - 1) // MOE_TM - first_tile + 1, 0)
    item_ends = jnp.cumsum(n_items_e)
    item_starts = item_ends - n_items_e
    n_items = item_ends[-1]
    j = jnp.arange(MOE_ITEMS, dtype=jnp.int32)
    jc = jnp.clip(j, 0, jnp.maximum(n_items - 1, 0))
    e_j = jnp.minimum(jnp.sum(item_ends[None, :] <= jc[:, None], axis=1), N_EXPERTS - 1).astype(jnp.int32)
    onehot = e_j[:, None] == jnp.arange(N_EXPERTS, dtype=jnp.int32)[None, :]
    tile_j = _pick(onehot, first_tile) + (jc - _pick(onehot, item_starts))
    lo = jnp.maximum(_pick(onehot, starts), tile_j * MOE_TM) - tile_j * MOE_TM
    hi = jnp.minimum(_pick(onehot, ends), (tile_j + 1) * MOE_TM) - tile_j * MOE_TM
    valid = j < n_items
    prev_tile = jnp.concatenate([jnp.full((1,), -1, jnp.int32), tile_j[:-1]])
    next_tile = jnp.concatenate([tile_j[1:], jnp.full((1,), -1, jnp.int32)])
    first = valid & (tile_j != prev_tile)
    last = valid & ((tile_j != next_tile) | (j == n_items - 1))
    flags = (first * ITEM_FIRST + last * ITEM_LAST + valid * ITEM_VALID
             + (j == n_items - 1) * ITEM_FINAL)
    i32 = lambda v: v.astype(jnp.int32)
    return i32(tile_j), e_j, i32(lo), i32(jnp.where(valid, hi, lo)), i32(flags)


def _moe_layer(x, g_ffn, mod, layer, w_router, b_router, w_gate_up, b_gate_up, w_down, b_down):
    t = x.shape[0]
    a = TOP_K * t
    h, idx_t, w_t, counts = _router(x, g_ffn.reshape(1, -1), mod, w_router.T, b_router)
    _, order, w_sorted = lax.sort((idx_t.reshape(a), jnp.arange(a, dtype=jnp.int32), w_t.reshape(a)),
                                  num_keys=1, is_stable=False)
    plan = _moe_plan(counts.reshape(N_EXPERTS))
    y_tok = _experts(plan, h, order % t, order, layer, w_gate_up, b_gate_up, w_down, b_down,
                     jnp.broadcast_to(w_sorted[:, None], (a, LANE)))
    return _moe_sum(x, mod, y_tok)


def _na_layer(x, g_mix, mod, cache_k, cache_v, w_in, q_g, k_g, rpb, w_out):
    hd = NA_HEADS * NA_HEAD_DIM
    gains = jnp.concatenate([jnp.tile(q_g * NA_HEAD_DIM ** -0.5, NA_HEADS), jnp.tile(k_g, NA_HEADS),
                             jnp.ones((hd,), F32)]).reshape(1, -1)
    tn = 1024
    qkv = _modmm(x, g_mix.reshape(1, -1), mod, w_in.astype(BF16), shift_idx=0, scale_idx=1, tn=tn,
                 norm=(2 * hd // tn, gains, NA_HEAD_DIM), name="na_qkv")
    qkv_by_seq = qkv.reshape(TOKENS // SEQ, SEQ, 3 * hd)
    qkv_by_dec = qkv.reshape(TOKENS // DEC_SEQ, DEC_SEQ, 3 * hd)
    n_pairs = NA_HEADS // 2
    o_p = _pair_attn(qkv_by_seq, qkv_by_seq, qkv_by_seq, b=BATCH, q_col=0, k_col=n_pairs,
                     v_col=2 * n_pairs, slot=NA_HEAD_DIM, tq=SEQ, n_pairs=n_pairs, pairs_blk=n_pairs,
                     name="na_context_attn")
    bias = _na_bias_table(rpb)
    o_s = _na_latent_attn(qkv_by_dec, cache_k.reshape(DEC_BATCH, PAST_LEN, hd),
                          cache_v.reshape(DEC_BATCH, PAST_LEN, hd), bias, PROMPT_TOK // DEC_SEQ)
    o = jnp.concatenate([o_p.reshape(PROMPT_TOK, hd), o_s.reshape(SAMPLE_TOK, hd)], axis=0)
    x = _mm_res(o, w_out.astype(BF16), x, mod, 2, name="na_out")
    new_k = qkv[:PROMPT_TOK, hd:2 * hd].reshape(BATCH, SEQ, NA_HEADS, NA_HEAD_DIM)
    new_v = qkv[:PROMPT_TOK, 2 * hd:].reshape(BATCH, SEQ, NA_HEADS, NA_HEAD_DIM)
    return x, new_k, new_v


def _mla_weights(w_q_up, w_kv_up, q_g, k_g):
    wq = w_q_up.reshape(MLA_Q_LORA, MLA_HEADS, MLA_QK_DIM)
    wq = jnp.pad(wq, ((0, 0), (0, 0), (0, LANE - MLA_QK_DIM))).reshape(MLA_Q_LORA, MLA_HEADS * LANE)
    kv = w_kv_up.reshape(MLA_KV_LORA, MLA_HEADS, MLA_NOPE_DIM + MLA_V_DIM)
    wk_top = jnp.pad(kv[:, :, :MLA_NOPE_DIM], ((0, 0), (0, 0), (0, LANE - MLA_NOPE_DIM)))
    eye = jnp.eye(MLA_ROPE_DIM, dtype=F32)
    pe_rows = jnp.pad(eye, ((0, 0), (MLA_NOPE_DIM, LANE - MLA_QK_DIM)))
    pe_rows = jnp.broadcast_to(pe_rows[:, None, :], (MLA_ROPE_DIM, MLA_HEADS, LANE))
    k_in = 2 * LANE
    wk = jnp.concatenate([wk_top, pe_rows,
                          jnp.zeros((k_in - MLA_KV_LORA - MLA_ROPE_DIM, MLA_HEADS, LANE), F32)], axis=0)
    wk = wk.reshape(k_in, MLA_HEADS * LANE)
    wv = jnp.pad(kv[:, :, MLA_NOPE_DIM:].reshape(MLA_KV_LORA, MLA_HEADS * MLA_V_DIM),
                 ((0, k_in - MLA_KV_LORA), (0, 0)))
    wkv = jnp.concatenate([wk, wv], axis=1)
    pad_g = LANE - MLA_QK_DIM
    qgain = jnp.tile(jnp.pad(q_g * MLA_QK_DIM ** -0.5, (0, pad_g)), MLA_HEADS).reshape(1, -1)
    kgain = jnp.tile(jnp.pad(k_g, (0, pad_g)), MLA_HEADS)
    kvgain = jnp.concatenate([kgain, jnp.ones((MLA_HEADS * MLA_V_DIM,), F32)]).reshape(1, -1)
    return wq.astype(BF16), wkv.astype(BF16), qgain, kvgain


def _mla_layer(x, g_mix, mod, cache_ckv, cache_kpe, w_in, q_a_g, kv_a_g, w_q_up, w_kv_up, q_g, k_g,
               w_out):
    n_lat = MLA_Q_LORA + MLA_KV_LORA + MLA_ROPE_DIM
    w_in_p = jnp.pad(w_in, ((0, 0), (0, 512 - n_lat))).astype(BF16)
    proj = _modmm(x, g_mix.reshape(1, -1), mod, w_in_p, shift_idx=0, scale_idx=1, tn=512,
                  name="mla_in")
    cq, ckv, kin = _mla_latents(proj, q_a_g, kv_a_g)
    wq, wkv, qgain, kvgain = _mla_weights(w_q_up, w_kv_up, q_g, k_g)
    n_ktiles = MLA_HEADS * LANE // MMN_TN
    npt = PROMPT_TOK // TM
    spb = DEC_SEQ // TM
    q_tab = _rope_tables(DEC_SEQ, n_identity_rows_first=TM)
    q = _mmn(cq, wq, qgain, n_norm_tiles=n_ktiles, count=MLA_QK_DIM,
             rope=(q_tab, lambda i: jnp.where(i < npt, 0, 1 + (i - npt) % spb)), name="mla_q")
    kv_p = _mmn(kin[:PROMPT_TOK], wkv, kvgain, n_norm_tiles=n_ktiles, count=MLA_QK_DIM,
                name="mla_kv_prompt")
    kin_ctx = jnp.concatenate(
        [cache_ckv, cache_kpe, jnp.zeros((DEC_BATCH, PAST_LEN, 2 * LANE - MLA_KV_LORA - MLA_ROPE_DIM),
                                         F32)], axis=-1).astype(BF16)
    kin_s = jnp.concatenate([kin[PROMPT_TOK:].reshape(DEC_BATCH, DEC_SEQ, 2 * LANE), kin_ctx], axis=1)
    n_keys = DEC_SEQ + PAST_LEN
    kpb = n_keys // TM
    k_tab = _rope_tables(DEC_SEQ, n_identity_rows_last=PAST_LEN)
    kv_s = _mmn(kin_s.reshape(DEC_BATCH * n_keys, 2 * LANE), wkv, kvgain, n_norm_tiles=n_ktiles,
                count=MLA_QK_DIM, rope=(k_tab, lambda i: i % kpb), name="mla_kv_sample")
    hw = MLA_HEADS * LANE
    n_pairs = MLA_HEADS // 2
    q_by_seq = q.reshape(TOKENS // SEQ, SEQ, hw)
    q_by_dec = q.reshape(TOKENS // DEC_SEQ, DEC_SEQ, hw)
    kv_p3 = kv_p.reshape(BATCH, SEQ, -1)
    kv_s3 = kv_s.reshape(DEC_BATCH, n_keys, -1)
    o_p = _pair_attn(q_by_seq, kv_p3, kv_p3, b=BATCH, q_col=0, k_col=0, v_col=hw // LANE, slot=LANE,
                     tq=SEQ, n_pairs=n_pairs, pairs_blk=n_pairs, name="mla_context_attn")
    o_s = _pair_attn(q_by_dec, kv_s3, kv_s3, b=DEC_BATCH, q_batch0=PROMPT_TOK // DEC_SEQ, q_col=0,
                     k_col=0, v_col=hw // LANE, slot=LANE, tq=256, n_pairs=n_pairs, pairs_blk=2,
                     name="mla_latent_attn")
    ov = MLA_HEADS * MLA_V_DIM
    o = jnp.concatenate([o_p.reshape(PROMPT_TOK, ov), o_s.reshape(SAMPLE_TOK, ov)], axis=0)
    x = _mm_res(o, w_out.astype(BF16), x, mod, 2, name="mla_out")
    new_ckv = ckv[:PROMPT_TOK].reshape(BATCH, SEQ, MLA_KV_LORA)
    new_kpe = proj[:PROMPT_TOK, MLA_Q_LORA + MLA_KV_LORA:n_lat].reshape(BATCH, SEQ, MLA_ROPE_DIM)
    return x, new_ckv, new_kpe


def _gdn_layer(x, g_mix, mod, s_fwd, s_bwd, w_in, conv_w, a_log, dt_bias, norm_g, w_out):
    n_main = GDN_QKV + GDN_KEY
    gm = g_mix.reshape(1, -1)
    proj = _modmm(x, gm, mod, w_in[:, :n_main].astype(BF16), shift_idx=0, scale_idx=1, tn=1024,
                  name="gdn_in")
    w_ab = jnp.pad(w_in[:, n_main:], ((0, 0), (0, LANE - 4 * GDN_HEADS))).astype(BF16)
    ab = _modmm(x, gm, mod, w_ab, shift_idx=0, scale_idx=1, tn=LANE, name="gdn_in_ab")
    gf, gb, beta = _gdn_gates(ab[:, :4 * GDN_HEADS].T, a_log, dt_bias)
    outs = []
    states = []
    for lo, hi, b, n, s0f, s0b in (
            (0, PROMPT_TOK, BATCH, SEQ, None, None),
            (PROMPT_TOK, TOKENS, DEC_BATCH, DEC_SEQ, s_fwd, s_bwd)):
        if s0f is None:
            s0f = jnp.zeros((b, GDN_HEADS, GDN_HEAD_DIM, GDN_HEAD_DIM), F32)
            s0b = s0f
        qkv = _gdn_conv(proj.reshape(TOKENS // n, n, n_main), conv_w, b, lo // n)
        cols, rows = _gdn_gate_tables(gf[:, lo:hi], gb[:, lo:hi], beta[:, lo:hi], b, n)
        pf, pb = _gdn_prep(qkv, cols, rows)
        o_f, o_b, sf, sb = _gdn_scan(pf, pb, s0f, s0b)
        outs.append((o_f.reshape(b * n, GDN_KEY), o_b.reshape(b * n, GDN_KEY)))
        states.append((sf, sb))
    y = _gdn_out(outs[0], outs[1], proj, norm_g)
    x = _mm_res(y, w_out.astype(BF16), x, mod, 2, name="gdn_out_proj")
    return x, states[0][0], states[0][1]


def kernel(x_prompt, x_sample, cache_na_k, cache_na_v, cache_mla_ckv, cache_mla_kpe, state_gdn_fwd,
           state_gdn_bwd, c, c_ctx, ada_w, ada_b, norm_mix_g, norm_ffn_g, na_w_in, na_q_g, na_k_g,
           na_rpb, na_w_out, mla_w_in, mla_q_a_g, mla_kv_a_g, mla_w_q_up, mla_w_kv_up, mla_q_g,
           mla_k_g, mla_w_out, gdn_w_in, gdn_conv_w, gdn_a_log, gdn_dt_bias, gdn_norm_g, gdn_w_out,
           moe_w_router, moe_b_router, moe_w_gate_up, moe_b_gate_up, moe_w_down, moe_b_down):
    x = jnp.concatenate([x_prompt.reshape(PROMPT_TOK, D_MODEL), x_sample.reshape(SAMPLE_TOK, D_MODEL)],
                        axis=0)
    c_all = jnp.concatenate([c_ctx[None], c, jnp.zeros((16 - N_GROUPS, D_MODEL), F32)], axis=0)
    mods = _ada_mods(c_all, ada_w, ada_b)[:, :N_GROUPS].reshape(DEPTH, N_GROUPS, 6, 1, D_MODEL)
    new_na_k, new_na_v, new_ckv, new_kpe, new_sf, new_sb = [], [], [], [], [], []
    for i in range(DEPTH):
        kind, j = i % N_MIXERS, i // N_MIXERS
        mod = mods[i]
        if kind == 0:
            x, nk, nv = _na_layer(x, norm_mix_g[i], mod, cache_na_k[:, j], cache_na_v[:, j],
                                  na_w_in[j], na_q_g[j], na_k_g[j], na_rpb[j], na_w_out[j])
            new_na_k.append(nk)
            new_na_v.append(nv)
        elif kind == 1:
            x, ckv, kpe = _mla_layer(x, norm_mix_g[i], mod, cache_mla_ckv[:, j], cache_mla_kpe[:, j],
                                     mla_w_in[j], mla_q_a_g[j], mla_kv_a_g[j], mla_w_q_up[j],
                                     mla_w_kv_up[j], mla_q_g[j], mla_k_g[j], mla_w_out[j])
            new_ckv.append(ckv)
            new_kpe.append(kpe)
        else:
            x, sf, sb = _gdn_layer(x, norm_mix_g[i], mod, state_gdn_fwd[:, j], state_gdn_bwd[:, j],
                                   gdn_w_in[j], gdn_conv_w[j], gdn_a_log[j], gdn_dt_bias[j],
                                   gdn_norm_g[j], gdn_w_out[j])
            new_sf.append(sf)
            new_sb.append(sb)
        x = _moe_layer(x, norm_ffn_g[i], mod, i, moe_w_router[i], moe_b_router[i], moe_w_gate_up,
                       moe_b_gate_up[i], moe_w_down, moe_b_down[i])
    return (x[:PROMPT_TOK].reshape(BATCH, SEQ, D_MODEL),
            x[PROMPT_TOK:].reshape(DEC_BATCH, DEC_SEQ, D_MODEL),
            jnp.stack(new_na_k, axis=1), jnp.stack(new_na_v, axis=1),
            jnp.stack(new_ckv, axis=1), jnp.stack(new_kpe, axis=1),
            jnp.stack(new_sf, axis=1), jnp.stack(new_sb, axis=1))
```
